```python
import math
import jax, jax.numpy as jnp
from jax import lax
import numpy as np

D_MODEL = 1024
BATCH = 4
SEQ = 4096
DEPTH = 2

N_MIXERS = 2
N_POOL_LAYERS = (DEPTH + 1) // 2
N_ATTN_LAYERS = DEPTH // 2
ALPHA = (2.0 * DEPTH) ** 0.25
BETA = (8.0 * DEPTH) ** -0.25
LN_EPS = 1e-5
POOL_WINDOWS = (2, 4, 8, 16)
N_GROUPS = len(POOL_WINDOWS)
GROUP_W = D_MODEL // N_GROUPS
HEAD_DIM = 64
N_HEADS = D_MODEL // (2 * HEAD_DIM)
V_DIM = 2 * HEAD_DIM
D_ATTN = N_HEADS * V_DIM
Q_BLOCK = 128
NUM_BUCKETS = 32
MAX_EXACT = NUM_BUCKETS // 2
MAX_DISTANCE = 128
D_FF = ((int(math.ceil(8 * D_MODEL / 3)) + 255) // 256) * 256

kernel_name = "hybrid_pool_diffattn_deepnorm"


def _layernorm(x, g, b):
    xf = x.astype(jnp.float32)
    mu = jnp.mean(xf, axis=-1, keepdims=True)
    var = jnp.mean(jnp.square(xf - mu), axis=-1, keepdims=True)
    y = (xf - mu) * lax.rsqrt(var + LN_EPS)
    return (y * g.astype(jnp.float32) + b.astype(jnp.float32)).astype(x.dtype)


def _multiscale_pool(x, w, scale):
    B, S, D = x.shape
    xg = x.astype(jnp.float32).reshape(B, S, N_GROUPS, GROUP_W)
    cs = jnp.pad(jnp.cumsum(xg, axis=1), ((0, 0), (1, 0), (0, 0), (0, 0)))
    t = jnp.arange(S)
    pooled = []
    for g, win in enumerate(POOL_WINDOWS):
        hi = cs[:, 1:, g]
        lo = jnp.pad(cs[:, :S + 1 - win, g], ((0, 0), (win - 1, 0), (0, 0)))
        cnt = jnp.minimum(t + 1, win).astype(jnp.float32)[None, :, None]
        pooled.append((hi - lo) / cnt)
    pooled = jnp.stack(pooled, axis=2)
    mixed = jnp.einsum('bsgc,gcd->bsgd', pooled - xg, w.astype(jnp.float32))
    return (mixed.reshape(B, S, D) * scale.astype(jnp.float32)).astype(x.dtype)


def _rel_bucket(rel):
    n = jnp.maximum(rel, 0)
    is_small = n < MAX_EXACT
    nf = jnp.maximum(n, 1).astype(jnp.float32)
    large = MAX_EXACT + (jnp.log(nf / MAX_EXACT) / math.log(MAX_DISTANCE / MAX_EXACT)
                         * (NUM_BUCKETS - MAX_EXACT)).astype(jnp.int32)
    large = jnp.minimum(large, NUM_BUCKETS - 1)
    return jnp.where(is_small, n, large)


def _diff_attention(x, w_qkv, w_o, lam_p, subln_g, rel_table, lambda_init):
    B, S, D = x.shape
    qkv = x @ w_qkv
    q, k, v = jnp.split(qkv, 3, axis=-1)
    q = q.reshape(B, S, N_HEADS, 2, HEAD_DIM).transpose(0, 2, 3, 1, 4)
    k = k.reshape(B, S, N_HEADS, 2, HEAD_DIM).transpose(0, 2, 3, 1, 4)
    v = v.reshape(B, S, N_HEADS, V_DIM).transpose(0, 2, 1, 3)
    lp = lam_p.astype(jnp.float32)
    lam = (jnp.exp(jnp.sum(lp[0] * lp[1])) - jnp.exp(jnp.sum(lp[2] * lp[3])) + lambda_init)
    sm_scale = HEAD_DIM ** -0.5
    outs = []
    for qb in range(S // Q_BLOCK):
        q0 = qb * Q_BLOCK
        kend = q0 + Q_BLOCK
        s = jnp.einsum('bhcqd,bhckd->bhcqk', q[:, :, :, q0:kend], k[:, :, :, :kend]).astype(jnp.float32) * sm_scale
        rel = (q0 + jnp.arange(Q_BLOCK))[:, None] - jnp.arange(kend)[None, :]
        bias = rel_table[_rel_bucket(rel)].astype(jnp.float32).transpose(2, 0, 1)
        s = jnp.where(rel >= 0, s + bias[None, :, None], -jnp.inf)
        p = jax.nn.softmax(s, axis=-1)
        a = p[:, :, 0] - lam * p[:, :, 1]
        outs.append(jnp.einsum('bhqk,bhkv->bhqv', a.astype(v.dtype), v[:, :, :kend]))
    o = jnp.concatenate(outs, axis=2).astype(jnp.float32)
    o = o * lax.rsqrt(jnp.mean(jnp.square(o), axis=-1, keepdims=True) + LN_EPS)
    o = o * subln_g.astype(jnp.float32) * (1.0 - lambda_init)
    o = o.transpose(0, 2, 1, 3).reshape(B, S, D_ATTN).astype(x.dtype)
    return o @ w_o


def _swiglu(x, w_gate, w_up, w_down):
    return (jax.nn.silu(x @ w_gate) * (x @ w_up)) @ w_down


def setup_inputs(seed: int = 0) -> dict:
    key = jax.random.key(seed)
    ks = jax.random.split(key, 20)
    f32 = jnp.float32
    nrm = lambda k, s: jax.random.normal(k, s, f32)
    x = nrm(ks[0], (BATCH, SEQ, D_MODEL))
    pool_w = nrm(ks[1], (N_POOL_LAYERS, N_GROUPS, GROUP_W, GROUP_W)) * (GROUP_W ** -0.5) * BETA
    pool_scale = 1.0 + 0.02 * nrm(ks[2], (N_POOL_LAYERS, D_MODEL))
    w_qk = nrm(ks[3], (N_ATTN_LAYERS, D_MODEL, 2 * D_ATTN)) * (D_MODEL ** -0.5)
    w_v = nrm(ks[4], (N_ATTN_LAYERS, D_MODEL, D_ATTN)) * (D_MODEL ** -0.5) * BETA
    w_qkv = jnp.concatenate([w_qk, w_v], axis=-1)
    w_o = nrm(ks[5], (N_ATTN_LAYERS, D_ATTN, D_MODEL)) * (D_ATTN ** -0.5) * BETA
    lam_p = 0.1 * nrm(ks[6], (N_ATTN_LAYERS, 4, HEAD_DIM))
    subln_g = 1.0 + 0.02 * nrm(ks[7], (N_ATTN_LAYERS, V_DIM))
    rel_table = 0.5 * nrm(ks[8], (NUM_BUCKETS, N_HEADS))
    w_gate = nrm(ks[9], (DEPTH, D_MODEL, D_FF)) * (D_MODEL ** -0.5) * BETA
    w_up = nrm(ks[10], (DEPTH, D_MODEL, D_FF)) * (D_MODEL ** -0.5) * BETA
    w_down = nrm(ks[11], (DEPTH, D_FF, D_MODEL)) * (D_FF ** -0.5) * BETA
    ln_mix_g = 1.0 + 0.02 * nrm(ks[12], (DEPTH, D_MODEL))
    ln_mix_b = 0.02 * nrm(ks[13], (DEPTH, D_MODEL))
    ln_ffn_g = 1.0 + 0.02 * nrm(ks[14], (DEPTH, D_MODEL))
    ln_ffn_b = 0.02 * nrm(ks[15], (DEPTH, D_MODEL))
    return {"x": x, "pool_w": pool_w, "pool_scale": pool_scale, "w_qkv": w_qkv, "w_o": w_o,
            "lam_p": lam_p, "subln_g": subln_g, "rel_table": rel_table, "w_gate": w_gate,
            "w_up": w_up, "w_down": w_down, "ln_mix_g": ln_mix_g, "ln_mix_b": ln_mix_b,
            "ln_ffn_g": ln_ffn_g, "ln_ffn_b": ln_ffn_b}


def reference(x, pool_w, pool_scale, w_qkv, w_o, lam_p, subln_g, rel_table, w_gate,
              w_up, w_down, ln_mix_g, ln_mix_b, ln_ffn_g, ln_ffn_b):
    for i in range(DEPTH):
        j = i // N_MIXERS
        if i % N_MIXERS == 0:
            h = _multiscale_pool(x, pool_w[j], pool_scale[j])
        else:
            lambda_init = 0.8 - 0.6 * math.exp(-0.3 * i)
            h = _diff_attention(x, w_qkv[j], w_o[j], lam_p[j], subln_g[j], rel_table, lambda_init)
        x = _layernorm(ALPHA * x + h, ln_mix_g[i], ln_mix_b[i])
        x = _layernorm(ALPHA * x + _swiglu(x, w_gate[i], w_up[i], w_down[i]), ln_ffn_g[i], ln_ffn_b[i])
    return x
```

```python
import functools
import math

import jax
import jax.numpy as jnp
from jax import lax
from jax.experimental import pallas as pl
from jax.experimental.pallas import tpu as pltpu

D_MODEL = 1024
DEPTH = 2
ALPHA = (2.0 * DEPTH) ** 0.25
LN_EPS = 1e-5
POOL_WINDOWS = (2, 4, 8, 16)
N_GROUPS = len(POOL_WINDOWS)
GROUP_W = D_MODEL // N_GROUPS
POOL_HALO = 16
HEAD_DIM = 64
N_HEADS = D_MODEL // (2 * HEAD_DIM)
V_DIM = 2 * HEAD_DIM
D_ATTN = N_HEADS * V_DIM
NUM_BUCKETS = 32
MAX_EXACT = NUM_BUCKETS // 2
MAX_DISTANCE = 128
D_FF = 2816
SM_SCALE = HEAD_DIM ** -0.5

FF_CHUNK = 256
ROW_TILE = 512
POOL_TILE = 512
ATT_TILE = 256
MASK_VALUE = -1e30
V7X_VMEM_BYTES = 64 * 1024 * 1024

F32 = jnp.float32
BF16 = jnp.bfloat16


def _params(semantics, vmem_bytes):
    limit = min(int(vmem_bytes * 1.25) + (4 << 20), V7X_VMEM_BYTES - (6 << 20))
    return pltpu.CompilerParams(dimension_semantics=semantics, vmem_limit_bytes=limit)


def _layernorm(z, g, b):
    mu = jnp.mean(z, axis=-1, keepdims=True)
    zc = z - mu
    var = jnp.mean(zc * zc, axis=-1, keepdims=True)
    return zc * lax.rsqrt(var + LN_EPS) * g + b


def _pool_ln_kernel(x_ref, halo_ref, w_ref, scale_ref, g_ref, b_ref, o_ref):
    s = pl.program_id(1)
    ts = x_ref.shape[1]
    x = x_ref[0]
    halo = jnp.where(s > 0, halo_ref[0], 0.0)
    xh = jnp.concatenate([halo, x], axis=0)
    t = s * ts + lax.broadcasted_iota(jnp.int32, (ts, 1), 0)
    mixed = []
    for g, win in enumerate(POOL_WINDOWS):
        a = xh[:, g * GROUP_W:(g + 1) * GROUP_W]
        shift = 1
        while shift < win:
            a = a + pltpu.roll(a, shift, 0)
            shift *= 2
        cnt = jnp.minimum(t + 1, win).astype(F32)
        d = a[POOL_HALO:] / cnt - x[:, g * GROUP_W:(g + 1) * GROUP_W]
        mixed.append(jnp.dot(d.astype(BF16), w_ref[g], preferred_element_type=F32))
    h = jnp.concatenate(mixed, axis=-1) * scale_ref[...]
    o_ref[0] = _layernorm(ALPHA * x + h, g_ref[...], b_ref[...])


def _pool_ln(x, w, scale, g, b):
    B, S, D = x.shape
    ts = POOL_TILE
    halo_blocks = ts // POOL_HALO
    vmem = 4 * ts * D * 4 + 12 * ts * D * 4
    return pl.pallas_call(
        _pool_ln_kernel,
        grid=(B, S // ts),
        in_specs=[
            pl.BlockSpec((1, ts, D), lambda b_, s_: (b_, s_, 0)),
            pl.BlockSpec((1, POOL_HALO, D), lambda b_, s_: (b_, jnp.maximum(s_ * halo_blocks - 1, 0), 0)),
            pl.BlockSpec((N_GROUPS, GROUP_W, GROUP_W), lambda b_, s_: (0, 0, 0)),
            pl.BlockSpec((1, D), lambda b_, s_: (0, 0)),
            pl.BlockSpec((1, D), lambda b_, s_: (0, 0)),
            pl.BlockSpec((1, D), lambda b_, s_: (0, 0)),
        ],
        out_specs=pl.BlockSpec((1, ts, D), lambda b_, s_: (b_, s_, 0)),
        out_shape=jax.ShapeDtypeStruct((B, S, D), F32),
        compiler_params=_params(("parallel", "parallel"), vmem),
        name="pool_ln",
    )(x, x, w, scale, g, b)


def _ffn_ln_kernel(x_ref, wg_ref, wu_ref, wd_ref, g_ref, b_ref, o_ref, xb_ref, acc_ref):
    xb_ref[...] = x_ref[...].astype(BF16)
    acc_ref[...] = jnp.zeros_like(acc_ref)

    def chunk(c, carry):
        xb = xb_ref[...]
        gate = jnp.dot(xb, wg_ref[c], preferred_element_type=F32)
        up = jnp.dot(xb, wu_ref[c], preferred_element_type=F32)
        act = gate * jax.nn.sigmoid(gate) * up
        acc_ref[...] += jnp.dot(act.astype(BF16), wd_ref[c], preferred_element_type=F32)
        return carry

    lax.fori_loop(0, wg_ref.shape[0], chunk, 0)
    o_ref[...] = _layernorm(ALPHA * x_ref[...] + acc_ref[...], g_ref[...], b_ref[...])


def _ffn_ln(x2d, wg, wu, wd, g, b):
    M, D = x2d.shape
    tm = ROW_TILE
    nc = wg.shape[0]
    w_bytes = 3 * nc * D * FF_CHUNK * 2
    vmem = w_bytes + 4 * tm * D * 4 + tm * D * 6 + 6 * tm * FF_CHUNK * 4
    resident = functools.partial(pl.BlockSpec, pipeline_mode=pl.Buffered(1))
    return pl.pallas_call(
        _ffn_ln_kernel,
        grid=(M // tm,),
        in_specs=[
            pl.BlockSpec((tm, D), lambda i: (i, 0)),
            resident((nc, D, FF_CHUNK), lambda i: (0, 0, 0)),
            resident((nc, D, FF_CHUNK), lambda i: (0, 0, 0)),
            resident((nc, FF_CHUNK, D), lambda i: (0, 0, 0)),
            pl.BlockSpec((1, D), lambda i: (0, 0)),
            pl.BlockSpec((1, D), lambda i: (0, 0)),
        ],
        out_specs=pl.BlockSpec((tm, D), lambda i: (i, 0)),
        out_shape=jax.ShapeDtypeStruct((M, D), F32),
        scratch_shapes=[pltpu.VMEM((tm, D), BF16), pltpu.VMEM((tm, D), F32)],
        compiler_params=_params(("parallel",), vmem),
        name="ffn_ln",
    )(x2d, wg, wu, wd, g, b)


def _qkv_kernel(x_ref, w_ref, o_ref):
    xb = x_ref[...].astype(BF16)
    for part in range(3):
        cols = slice(part * D_ATTN, (part + 1) * D_ATTN)
        y = jnp.dot(xb, w_ref[:, cols], preferred_element_type=F32)
        if part == 0:
            y = y * SM_SCALE
        o_ref[:, cols] = y.astype(BF16)


def _qkv(x2d, w):
    M, D = x2d.shape
    tm = ROW_TILE
    N = w.shape[1]
    vmem = D * N * 2 + 2 * tm * D * 4 + 2 * tm * N * 2 + tm * D * 2 + 2 * tm * D_ATTN * 4
    return pl.pallas_call(
        _qkv_kernel,
        grid=(M // tm,),
        in_specs=[
            pl.BlockSpec((tm, D), lambda i: (i, 0)),
            pl.BlockSpec((D, N), lambda i: (0, 0), pipeline_mode=pl.Buffered(1)),
        ],
        out_specs=pl.BlockSpec((tm, N), lambda i: (i, 0)),
        out_shape=jax.ShapeDtypeStruct((M, N), BF16),
        compiler_params=_params(("parallel",), vmem),
        name="qkv_proj",
    )(x2d, w)


def _attn_kernel(lambda_init, q_ref, k_ref, v_ref, bias_ref, lam_ref, sg_ref, o_ref,
                 qs_ref, m_ref, l_ref, acc_ref):
    i = pl.program_id(2)
    tq = q_ref.shape[1]
    tk = tq

    q = q_ref[0]
    lane = lax.broadcasted_iota(jnp.int32, q.shape, 1)
    zero = jnp.zeros_like(q)
    qs_ref[:tq] = jnp.where(lane < HEAD_DIM, q, zero)
    qs_ref[tq:] = jnp.where(lane >= HEAD_DIM, q, zero)

    def scores(kc, bias):
        start = pl.multiple_of(kc * tk, tk)
        k = k_ref[0, pl.ds(start, tk), :]
        v = v_ref[0, pl.ds(start, tk), :]
        s = lax.dot_general(qs_ref[...], k, (((1,), (1,)), ((), ())), preferred_element_type=F32)
        if bias is not None:
            s = (s.reshape(2, tq, tk) + bias[None]).reshape(2 * tq, tk)
        return s, v

    s, v = scores(i, bias_ref[0, :, tk:])
    m = jnp.max(s, axis=-1, keepdims=True)
    p = jnp.exp(s - m)
    m_ref[...] = m
    l_ref[...] = jnp.sum(p, axis=-1, keepdims=True)
    acc_ref[...] = jnp.dot(p.astype(BF16), v, preferred_element_type=F32)

    def update(kc, bias):
        s, v = scores(kc, bias)
        m_prev = m_ref[...]
        m_new = jnp.maximum(m_prev, jnp.max(s, axis=-1, keepdims=True))
        alpha = jnp.exp(m_prev - m_new)
        p = jnp.exp(s - m_new)
        l_ref[...] = alpha * l_ref[...] + jnp.sum(p, axis=-1, keepdims=True)
        acc_ref[...] = alpha * acc_ref[...] + jnp.dot(p.astype(BF16), v, preferred_element_type=F32)
        m_ref[...] = m_new

    @pl.when(i > 0)
    def _():
        update(i - 1, bias_ref[0, :, :tk])

    def far(kc, carry):
        update(kc, None)
        return carry

    lax.fori_loop(0, i - 1, far, 0)

    lp = lam_ref[...]
    lam = (jnp.exp(jnp.sum(lp[0:1] * lp[1:2], axis=-1, keepdims=True))
           - jnp.exp(jnp.sum(lp[2:3] * lp[3:4], axis=-1, keepdims=True)) + lambda_init)
    out = acc_ref[...] / l_ref[...]
    o = out[:tq] - lam * out[tq:]
    o = o * lax.rsqrt(jnp.mean(o * o, axis=-1, keepdims=True) + LN_EPS)
    o_ref[0] = (o * (sg_ref[...] * (1.0 - lambda_init))).astype(BF16)


def _rel_bucket(rel):
    n = jnp.maximum(rel, 0)
    nf = jnp.maximum(n, 1).astype(F32)
    large = MAX_EXACT + (jnp.log(nf / MAX_EXACT) / math.log(MAX_DISTANCE / MAX_EXACT)
                         * (NUM_BUCKETS - MAX_EXACT)).astype(jnp.int32)
    return jnp.where(n < MAX_EXACT, n, jnp.minimum(large, NUM_BUCKETS - 1))


def _near_bias(rel_table):
    tq = ATT_TILE
    rel = (jnp.arange(tq)[:, None] + tq) - jnp.arange(2 * tq)[None, :]
    table = rel_table.astype(F32)
    far = table[_rel_bucket(jnp.full((), 2 * tq, jnp.int32))]
    bias = table[_rel_bucket(rel)] - far
    bias = jnp.where((rel >= 0)[..., None], bias, MASK_VALUE)
    return bias.transpose(2, 0, 1)


def _attention(qkv, bias, lam_p, subln_g, lambda_init):
    B, S, _ = qkv.shape
    tq = ATT_TILE
    H = N_HEADS
    vmem = 4 * S * V_DIM * 2 + 2 * tq * 2 * tq * 4 + 8 * 2 * tq * tq * 4 + 6 * 2 * tq * V_DIM * 4
    return pl.pallas_call(
        functools.partial(_attn_kernel, lambda_init),
        grid=(B, H, S // tq),
        in_specs=[
            pl.BlockSpec((1, tq, V_DIM), lambda b, h, i: (b, i, h)),
            pl.BlockSpec((1, S, V_DIM), lambda b, h, i: (b, 0, H + h)),
            pl.BlockSpec((1, S, V_DIM), lambda b, h, i: (b, 0, 2 * H + h)),
            pl.BlockSpec((1, tq, 2 * tq), lambda b, h, i: (h, 0, 0)),
            pl.BlockSpec((4, HEAD_DIM), lambda b, h, i: (0, 0)),
            pl.BlockSpec((1, V_DIM), lambda b, h, i: (0, 0)),
        ],
        out_specs=pl.BlockSpec((1, tq, V_DIM), lambda b, h, i: (b, i, h)),
        out_shape=jax.ShapeDtypeStruct((B, S, D_ATTN), BF16),
        scratch_shapes=[
            pltpu.VMEM((2 * tq, V_DIM), BF16),
            pltpu.VMEM((2 * tq, 1), F32),
            pltpu.VMEM((2 * tq, 1), F32),
            pltpu.VMEM((2 * tq, V_DIM), F32),
        ],
        compiler_params=_params(("parallel", "parallel", "arbitrary"), vmem),
        name="diff_attn",
    )(qkv, qkv, qkv, bias, lam_p, subln_g)


def _oproj_ln_kernel(x_ref, o_ref_in, w_ref, g_ref, b_ref, out_ref):
    h = jnp.dot(o_ref_in[...], w_ref[...], preferred_element_type=F32)
    out_ref[...] = _layernorm(ALPHA * x_ref[...] + h, g_ref[...], b_ref[...])


def _oproj_ln(x2d, o2d, w, g, b):
    M, D = x2d.shape
    tm = ROW_TILE
    vmem = 2 * D * D * 2 + 4 * tm * D * 4 + 2 * tm * D * 2 + 4 * tm * D * 4
    return pl.pallas_call(
        _oproj_ln_kernel,
        grid=(M // tm,),
        in_specs=[
            pl.BlockSpec((tm, D), lambda i: (i, 0)),
            pl.BlockSpec((tm, D), lambda i: (i, 0)),
            pl.BlockSpec((D, D), lambda i: (0, 0)),
            pl.BlockSpec((1, D), lambda i: (0, 0)),
            pl.BlockSpec((1, D), lambda i: (0, 0)),
        ],
        out_specs=pl.BlockSpec((tm, D), lambda i: (i, 0)),
        out_shape=jax.ShapeDtypeStruct((M, D), F32),
        compiler_params=_params(("parallel",), vmem),
        name="oproj_ln",
    )(x2d, o2d, w, g, b)


def _chunk_cols(w):
    D, F = w.shape
    return w.astype(BF16).reshape(D, F // FF_CHUNK, FF_CHUNK).transpose(1, 0, 2)


def _chunk_rows(w):
    F, D = w.shape
    return w.astype(BF16).reshape(F // FF_CHUNK, FF_CHUNK, D)


def kernel(x, pool_w, pool_scale, w_qkv, w_o, lam_p, subln_g, rel_table, w_gate, w_up, w_down,
           ln_mix_g, ln_mix_b, ln_ffn_g, ln_ffn_b):
    B, S, D = x.shape
    M = B * S
    row = lambda a: a.reshape(1, -1).astype(F32)

    def ffn(x2d, i):
        return _ffn_ln(x2d, _chunk_cols(w_gate[i]), _chunk_cols(w_up[i]), _chunk_rows(w_down[i]),
                       row(ln_ffn_g[i]), row(ln_ffn_b[i]))

    x1 = _pool_ln(x, pool_w[0].astype(BF16), row(pool_scale[0]), row(ln_mix_g[0]), row(ln_mix_b[0]))
    x2 = ffn(x1.reshape(M, D), 0)

    lambda_init = 0.8 - 0.6 * math.exp(-0.3 * 1)
    qkv = _qkv(x2, w_qkv[0].astype(BF16))
    o = _attention(qkv.reshape(B, S, 3 * D_ATTN), _near_bias(rel_table), lam_p[0].astype(F32),
                   row(subln_g[0]), lambda_init)
    x3 = _oproj_ln(x2, o.reshape(M, D_ATTN), w_o[0].astype(BF16), row(ln_mix_g[1]), row(ln_mix_b[1]))
    x4 = ffn(x3, 1)
    return x4.reshape(B, S, D)
```

```python
import functools
import math

import jax
import jax.numpy as jnp
from jax import lax
from jax.experimental import pallas as pl
from jax.experimental.pallas import tpu as pltpu

D_MODEL = 1024
DEPTH = 2
ALPHA = (2.0 * DEPTH) ** 0.25
LN_EPS = 1e-5
POOL_WINDOWS = (2, 4, 8, 16)
N_GROUPS = len(POOL_WINDOWS)
GROUP_W = D_MODEL // N_GROUPS
POOL_HALO = 16
HEAD_DIM = 64
N_HEADS = D_MODEL // (2 * HEAD_DIM)
V_DIM = 2 * HEAD_DIM
D_ATTN = N_HEADS * V_DIM
NUM_BUCKETS = 32
MAX_EXACT = NUM_BUCKETS // 2
MAX_DISTANCE = 128
D_FF = 2816
SM_SCALE = HEAD_DIM ** -0.5

FF_CHUNK = 256
ROW_TILE = 512
POOL_TILE = 512
ATT_BLOCK = 256
ATT_TILE = 512
SUM_ROWS = 16
MASK_VALUE = -1e30
V7X_VMEM_BYTES = 64 * 1024 * 1024

F32 = jnp.float32
BF16 = jnp.bfloat16
NT_DIMS = (((1,), (1,)), ((), ()))


def _params(semantics, vmem_bytes):
    limit = min(int(vmem_bytes * 1.25) + (4 << 20), V7X_VMEM_BYTES - (6 << 20))
    return pltpu.CompilerParams(dimension_semantics=semantics, vmem_limit_bytes=limit)


def _layernorm(z, g, b):
    mu = jnp.mean(z, axis=-1, keepdims=True)
    zc = z - mu
    var = jnp.mean(zc * zc, axis=-1, keepdims=True)
    return zc * lax.rsqrt(var + LN_EPS) * g + b


def _pool_ln_kernel(x_ref, halo_ref, w_ref, scale_ref, g_ref, b_ref, o_ref):
    s = pl.program_id(1)
    ts = x_ref.shape[1]
    x = x_ref[0]
    halo = jnp.where(s > 0, halo_ref[0], 0.0)
    xh = jnp.concatenate([halo, x], axis=0)
    t = s * ts + lax.broadcasted_iota(jnp.int32, (ts, 1), 0)
    mixed = []
    for g, win in enumerate(POOL_WINDOWS):
        a = xh[:, g * GROUP_W:(g + 1) * GROUP_W]
        shift = 1
        while shift < win:
            a = a + pltpu.roll(a, shift, 0)
            shift *= 2
        cnt = jnp.minimum(t + 1, win).astype(F32)
        d = a[POOL_HALO:] / cnt - x[:, g * GROUP_W:(g + 1) * GROUP_W]
        mixed.append(jnp.dot(d.astype(BF16), w_ref[g], preferred_element_type=F32))
    h = jnp.concatenate(mixed, axis=-1) * scale_ref[...]
    o_ref[0] = _layernorm(ALPHA * x + h, g_ref[...], b_ref[...])


def _pool_ln(x, w, scale, g, b):
    B, S, D = x.shape
    ts = POOL_TILE
    halo_blocks = ts // POOL_HALO
    vmem = 4 * ts * D * 4 + 12 * ts * D * 4
    return pl.pallas_call(
        _pool_ln_kernel,
        grid=(B, S // ts),
        in_specs=[
            pl.BlockSpec((1, ts, D), lambda b_, s_: (b_, s_, 0)),
            pl.BlockSpec((1, POOL_HALO, D), lambda b_, s_: (b_, jnp.maximum(s_ * halo_blocks - 1, 0), 0)),
            pl.BlockSpec((N_GROUPS, GROUP_W, GROUP_W), lambda b_, s_: (0, 0, 0)),
            pl.BlockSpec((1, D), lambda b_, s_: (0, 0)),
            pl.BlockSpec((1, D), lambda b_, s_: (0, 0)),
            pl.BlockSpec((1, D), lambda b_, s_: (0, 0)),
        ],
        out_specs=pl.BlockSpec((1, ts, D), lambda b_, s_: (b_, s_, 0)),
        out_shape=jax.ShapeDtypeStruct((B, S, D), F32),
        compiler_params=_params(("parallel", "parallel"), vmem),
        name="pool_ln",
    )(x, x, w, scale, g, b)


def _ffn_ln_kernel(x_ref, wg_ref, wu_ref, wd_ref, g_ref, b_ref, o_ref, xb_ref, acc_ref):
    xb_ref[...] = x_ref[...].astype(BF16)
    acc_ref[...] = jnp.zeros_like(acc_ref)

    def chunk(c, carry):
        xb = xb_ref[...]
        gate = jnp.dot(xb, wg_ref[c], preferred_element_type=F32)
        up = jnp.dot(xb, wu_ref[c], preferred_element_type=F32)
        act = gate * jax.nn.sigmoid(gate) * up
        acc_ref[...] += jnp.dot(act.astype(BF16), wd_ref[c], preferred_element_type=F32)
        return carry

    lax.fori_loop(0, wg_ref.shape[0], chunk, 0)
    o_ref[...] = _layernorm(ALPHA * x_ref[...] + acc_ref[...], g_ref[...], b_ref[...])


def _ffn_ln(x2d, wg, wu, wd, g, b):
    M, D = x2d.shape
    tm = ROW_TILE
    nc = wg.shape[0]
    w_bytes = 3 * nc * D * FF_CHUNK * 2
    vmem = w_bytes + 4 * tm * D * 4 + tm * D * 6 + 6 * tm * FF_CHUNK * 4
    resident = functools.partial(pl.BlockSpec, pipeline_mode=pl.Buffered(1))
    return pl.pallas_call(
        _ffn_ln_kernel,
        grid=(M // tm,),
        in_specs=[
            pl.BlockSpec((tm, D), lambda i: (i, 0)),
            resident((nc, D, FF_CHUNK), lambda i: (0, 0, 0)),
            resident((nc, D, FF_CHUNK), lambda i: (0, 0, 0)),
            resident((nc, FF_CHUNK, D), lambda i: (0, 0, 0)),
            pl.BlockSpec((1, D), lambda i: (0, 0)),
            pl.BlockSpec((1, D), lambda i: (0, 0)),
        ],
        out_specs=pl.BlockSpec((tm, D), lambda i: (i, 0)),
        out_shape=jax.ShapeDtypeStruct((M, D), F32),
        scratch_shapes=[pltpu.VMEM((tm, D), BF16), pltpu.VMEM((tm, D), F32)],
        compiler_params=_params(("parallel",), vmem),
        name="ffn_ln",
    )(x2d, wg, wu, wd, g, b)


def _qkv_kernel(x_ref, wq_ref, wk_ref, wvt_ref, q_ref, k_ref, vt_ref):
    xb = x_ref[0].astype(BF16)
    q = jnp.dot(xb, wq_ref[...], preferred_element_type=F32) * SM_SCALE
    q_ref[0] = q.astype(BF16)
    k_ref[0] = jnp.dot(xb, wk_ref[...], preferred_element_type=F32).astype(BF16)
    vt = lax.dot_general(wvt_ref[...], xb, NT_DIMS, preferred_element_type=F32)
    vt_ref[0] = vt.astype(BF16)


def _qkv(x, wq, wk, wvt):
    B, S, D = x.shape
    tm = ROW_TILE
    vmem = 3 * D * D_ATTN * 2 + 2 * tm * D * 4 + 6 * tm * D_ATTN * 2 + tm * D * 2 + 3 * tm * D_ATTN * 4
    resident = functools.partial(pl.BlockSpec, pipeline_mode=pl.Buffered(1))
    return pl.pallas_call(
        _qkv_kernel,
        grid=(B, S // tm),
        in_specs=[
            pl.BlockSpec((1, tm, D), lambda b, i: (b, i, 0)),
            resident((D, D_ATTN), lambda b, i: (0, 0)),
            resident((D, D_ATTN), lambda b, i: (0, 0)),
            resident((D_ATTN, D), lambda b, i: (0, 0)),
        ],
        out_specs=[
            pl.BlockSpec((1, tm, D_ATTN), lambda b, i: (b, i, 0)),
            pl.BlockSpec((1, tm, D_ATTN), lambda b, i: (b, i, 0)),
            pl.BlockSpec((1, D_ATTN, tm), lambda b, i: (b, 0, i)),
        ],
        out_shape=[
            jax.ShapeDtypeStruct((B, S, D_ATTN), BF16),
            jax.ShapeDtypeStruct((B, S, D_ATTN), BF16),
            jax.ShapeDtypeStruct((B, D_ATTN, S), BF16),
        ],
        compiler_params=_params(("parallel", "parallel"), vmem),
        name="qkv_proj",
    )(x, wq, wk, wvt)


def _attn_kernel(lambda_init, q_ref, k_ref, vt_ref, bias_ref, lam_ref, sg_ref, o_ref,
                 qs_ref, m_ref, acc_ref, s_ref):
    i = pl.program_id(2)
    blk = ATT_BLOCK
    nq = q_ref.shape[1] // blk
    assert nq == 2, "the key-block pipeline below pairs blocks and needs an even count per tile"
    diag_bias = bias_ref[0, 1]
    prev_bias = bias_ref[0, 0]

    lane = lax.broadcasted_iota(jnp.int32, (blk, V_DIM), 1)
    for hq in range(nq):
        q = q_ref[0, hq * blk:(hq + 1) * blk, :]
        zero = jnp.zeros_like(q)
        qs_ref[2 * hq] = jnp.where(lane < HEAD_DIM, q, zero)
        qs_ref[2 * hq + 1] = jnp.where(lane >= HEAD_DIM, q, zero)

    def scores(kc, cb):
        start = pl.multiple_of(kc * blk, blk)
        k = k_ref[0, pl.ds(start, blk), :]
        return lax.dot_general(k, qs_ref[cb], NT_DIMS, preferred_element_type=F32)

    def values(kc):
        start = pl.multiple_of(kc * blk, blk)
        vt = vt_ref[0, :, pl.ds(start, blk)]
        return jnp.concatenate([vt, jnp.ones((SUM_ROWS, blk), BF16)], axis=0)

    def update(cb, s, vt, bias, first):
        if bias is not None:
            s = s + bias
        m_new = jnp.max(s, axis=0, keepdims=True)
        if not first:
            m_prev = m_ref[cb]
            m_new = jnp.maximum(m_prev, m_new)
        p = jnp.exp(s - m_new)
        pv = jnp.dot(vt, p.astype(BF16), preferred_element_type=F32)
        if first:
            acc_ref[cb] = pv
        else:
            acc_ref[cb] = acc_ref[cb] * jnp.exp(m_prev - m_new) + pv
        m_ref[cb] = m_new

    base = nq * i
    for d in range(nq - 1, -1, -1):
        vt = values(base + d)
        for hq in range(d, nq):
            bias = diag_bias if hq == d else prev_bias if hq == d + 1 else None
            for c in range(2):
                update(2 * hq + c, scores(base + d, 2 * hq + c), vt, bias, hq == d)

    def issue(kc, slot):
        for cb in range(2 * nq):
            s_ref[slot, cb] = scores(kc, cb)

    def consume(kc, slot, last):
        vt = values(kc)
        for cb in range(2 * nq):
            bias = prev_bias if (last and cb < 2) else None
            update(cb, s_ref[slot, cb], vt, bias, False)

    @pl.when(i > 0)
    def _():
        issue(0, 0)

        def pair(t, carry):
            issue(2 * t + 1, 1)
            consume(2 * t, 0, False)
            issue(2 * t + 2, 0)
            consume(2 * t + 1, 1, False)
            return carry

        lax.fori_loop(0, i - 1, pair, 0)
        issue(base - 1, 1)
        consume(base - 2, 0, False)
        consume(base - 1, 1, True)


    lp = lam_ref[...]
    lam = (jnp.exp(jnp.sum(lp[0:1] * lp[1:2], axis=-1, keepdims=True))
           - jnp.exp(jnp.sum(lp[2:3] * lp[3:4], axis=-1, keepdims=True)) + lambda_init)
    gain = sg_ref[...] * (1.0 - lambda_init)
    for hq in range(nq):
        a0 = acc_ref[2 * hq]
        a1 = acc_ref[2 * hq + 1]
        ot = a0[:V_DIM] / a0[V_DIM:V_DIM + 1] - lam * (a1[:V_DIM] / a1[V_DIM:V_DIM + 1])
        o = ot.T
        o = o * lax.rsqrt(jnp.mean(o * o, axis=-1, keepdims=True) + LN_EPS)
        o_ref[0, hq * blk:(hq + 1) * blk, :] = (o * gain).astype(BF16)


def _rel_bucket(rel):
    n = jnp.maximum(rel, 0)
    nf = jnp.maximum(n, 1).astype(F32)
    large = MAX_EXACT + (jnp.log(nf / MAX_EXACT) / math.log(MAX_DISTANCE / MAX_EXACT)
                         * (NUM_BUCKETS - MAX_EXACT)).astype(jnp.int32)
    return jnp.where(n < MAX_EXACT, n, jnp.minimum(large, NUM_BUCKETS - 1))


def _near_bias(rel_table):
    blk = ATT_BLOCK
    key = jnp.arange(blk)[:, None]
    query = jnp.arange(blk)[None, :]
    rel = jnp.stack([query - key + blk, query - key])
    bucket = _rel_bucket(rel)
    table = rel_table.astype(F32)
    far = table[NUM_BUCKETS - 1]
    bias = jnp.zeros((N_HEADS,) + rel.shape, F32)
    for b in range(NUM_BUCKETS):
        bias = jnp.where(bucket[None] == b, (table[b] - far)[:, None, None, None], bias)
    return jnp.where(rel[None] >= 0, bias, MASK_VALUE)


def _attention(q, k, vt, bias, lam_p, subln_g, lambda_init):
    B, S, _ = q.shape
    tq = ATT_TILE
    blk = ATT_BLOCK
    nchain = 2 * (tq // blk)
    H = N_HEADS
    rows = V_DIM + SUM_ROWS
    vmem = 4 * S * V_DIM * 2 + 2 * 2 * blk * blk * 4 + 4 * nchain * blk * blk * 4 + 2 * nchain * rows * blk * 4
    return pl.pallas_call(
        functools.partial(_attn_kernel, lambda_init),
        grid=(B, H, S // tq),
        in_specs=[
            pl.BlockSpec((1, tq, V_DIM), lambda b, h, i: (b, i, h)),
            pl.BlockSpec((1, S, V_DIM), lambda b, h, i: (b, 0, h)),
            pl.BlockSpec((1, V_DIM, S), lambda b, h, i: (b, h, 0)),
            pl.BlockSpec((1, 2, blk, blk), lambda b, h, i: (h, 0, 0, 0)),
            pl.BlockSpec((4, HEAD_DIM), lambda b, h, i: (0, 0)),
            pl.BlockSpec((1, V_DIM), lambda b, h, i: (0, 0)),
        ],
        out_specs=pl.BlockSpec((1, tq, V_DIM), lambda b, h, i: (b, i, h)),
        out_shape=jax.ShapeDtypeStruct((B, S, D_ATTN), BF16),
        scratch_shapes=[
            pltpu.VMEM((nchain, blk, V_DIM), BF16),
            pltpu.VMEM((nchain, 1, blk), F32),
            pltpu.VMEM((nchain, rows, blk), F32),
            pltpu.VMEM((2, nchain, blk, blk), F32),
        ],
        compiler_params=_params(("parallel", "parallel", "arbitrary"), vmem),
        name="diff_attn",
    )(q, k, vt, bias, lam_p, subln_g)


def _oproj_ln_kernel(x_ref, o_ref_in, w_ref, g_ref, b_ref, out_ref):
    h = jnp.dot(o_ref_in[...], w_ref[...], preferred_element_type=F32)
    out_ref[...] = _layernorm(ALPHA * x_ref[...] + h, g_ref[...], b_ref[...])


def _oproj_ln(x2d, o2d, w, g, b):
    M, D = x2d.shape
    tm = ROW_TILE
    vmem = 2 * D * D * 2 + 4 * tm * D * 4 + 2 * tm * D * 2 + 4 * tm * D * 4
    return pl.pallas_call(
        _oproj_ln_kernel,
        grid=(M // tm,),
        in_specs=[
            pl.BlockSpec((tm, D), lambda i: (i, 0)),
            pl.BlockSpec((tm, D), lambda i: (i, 0)),
            pl.BlockSpec((D, D), lambda i: (0, 0)),
            pl.BlockSpec((1, D), lambda i: (0, 0)),
            pl.BlockSpec((1, D), lambda i: (0, 0)),
        ],
        out_specs=pl.BlockSpec((tm, D), lambda i: (i, 0)),
        out_shape=jax.ShapeDtypeStruct((M, D), F32),
        compiler_params=_params(("parallel",), vmem),
        name="oproj_ln",
    )(x2d, o2d, w, g, b)


def _chunk_cols(w):
    D, F = w.shape
    return w.astype(BF16).reshape(D, F // FF_CHUNK, FF_CHUNK).transpose(1, 0, 2)


def _chunk_rows(w):
    F, D = w.shape
    return w.astype(BF16).reshape(F // FF_CHUNK, FF_CHUNK, D)


def kernel(x, pool_w, pool_scale, w_qkv, w_o, lam_p, subln_g, rel_table, w_gate, w_up, w_down,
           ln_mix_g, ln_mix_b, ln_ffn_g, ln_ffn_b):
    B, S, D = x.shape
    M = B * S
    row = lambda a: a.reshape(1, -1).astype(F32)

    def ffn(x2d, i):
        return _ffn_ln(x2d, _chunk_cols(w_gate[i]), _chunk_cols(w_up[i]), _chunk_rows(w_down[i]),
                       row(ln_ffn_g[i]), row(ln_ffn_b[i]))

    x1 = _pool_ln(x, pool_w[0].astype(BF16), row(pool_scale[0]), row(ln_mix_g[0]), row(ln_mix_b[0]))
    x2 = ffn(x1.reshape(M, D), 0)

    lambda_init = 0.8 - 0.6 * math.exp(-0.3 * 1)
    wqkv = w_qkv[0].astype(BF16)
    q, k, vt = _qkv(x2.reshape(B, S, D), wqkv[:, :D_ATTN], wqkv[:, D_ATTN:2 * D_ATTN],
                    wqkv[:, 2 * D_ATTN:].T)
    o = _attention(q, k, vt, _near_bias(rel_table), lam_p[0].astype(F32), row(subln_g[0]), lambda_init)
    x3 = _oproj_ln(x2, o.reshape(M, D_ATTN), w_o[0].astype(BF16), row(ln_mix_g[1]), row(ln_mix_b[1]))
    x4 = ffn(x3, 1)
    return x4.reshape(B, S, D)
```

```python
import functools
import math

import jax
import jax.numpy as jnp
from jax import lax
from jax.experimental import pallas as pl
from jax.experimental.pallas import tpu as pltpu

D_MODEL = 1024
DEPTH = 2
ALPHA = (2.0 * DEPTH) ** 0.25
LN_EPS = 1e-5
POOL_WINDOWS = (2, 4, 8, 16)
N_GROUPS = len(POOL_WINDOWS)
GROUP_W = D_MODEL // N_GROUPS
POOL_HALO = 16
HEAD_DIM = 64
N_HEADS = D_MODEL // (2 * HEAD_DIM)
V_DIM = 2 * HEAD_DIM
D_ATTN = N_HEADS * V_DIM
NUM_BUCKETS = 32
MAX_EXACT = NUM_BUCKETS // 2
MAX_DISTANCE = 128
D_FF = 2816
SM_SCALE = HEAD_DIM ** -0.5

FF_CHUNK = 256
ROW_TILE = 512
POOL_TILE = 512
ATT_BLOCK = 256
ATT_TILE = 512
SUM_ROWS = 16
MASK_VALUE = -1e30
V7X_VMEM_BYTES = 64 * 1024 * 1024

F32 = jnp.float32
BF16 = jnp.bfloat16
NT_DIMS = (((1,), (1,)), ((), ()))


def _params(semantics, vmem_bytes):
    limit = min(int(vmem_bytes * 1.25) + (4 << 20), V7X_VMEM_BYTES - (6 << 20))
    return pltpu.CompilerParams(dimension_semantics=semantics, vmem_limit_bytes=limit)


def _layernorm(z, g, b):
    mu = jnp.mean(z, axis=-1, keepdims=True)
    zc = z - mu
    var = jnp.mean(zc * zc, axis=-1, keepdims=True)
    return zc * lax.rsqrt(var + LN_EPS) * g + b


def _pool_ln_kernel(x_ref, halo_ref, w_ref, scale_ref, g_ref, b_ref, o_ref):
    s = pl.program_id(1)
    ts = x_ref.shape[1]
    x = x_ref[0]
    halo = jnp.where(s > 0, halo_ref[0], 0.0)
    xh = jnp.concatenate([halo, x], axis=0)
    t = s * ts + lax.broadcasted_iota(jnp.int32, (ts, 1), 0)
    mixed = []
    for g, win in enumerate(POOL_WINDOWS):
        a = xh[:, g * GROUP_W:(g + 1) * GROUP_W]
        shift = 1
        while shift < win:
            a = a + pltpu.roll(a, shift, 0)
            shift *= 2
        cnt = jnp.minimum(t + 1, win).astype(F32)
        d = a[POOL_HALO:] / cnt - x[:, g * GROUP_W:(g + 1) * GROUP_W]
        mixed.append(jnp.dot(d.astype(BF16), w_ref[g], preferred_element_type=F32))
    h = jnp.concatenate(mixed, axis=-1) * scale_ref[...]
    o_ref[0] = _layernorm(ALPHA * x + h, g_ref[...], b_ref[...])


def _pool_ln(x, w, scale, g, b):
    B, S, D = x.shape
    ts = POOL_TILE
    halo_blocks = ts // POOL_HALO
    vmem = 4 * ts * D * 4 + 12 * ts * D * 4
    return pl.pallas_call(
        _pool_ln_kernel,
        grid=(B, S // ts),
        in_specs=[
            pl.BlockSpec((1, ts, D), lambda b_, s_: (b_, s_, 0)),
            pl.BlockSpec((1, POOL_HALO, D), lambda b_, s_: (b_, jnp.maximum(s_ * halo_blocks - 1, 0), 0)),
            pl.BlockSpec((N_GROUPS, GROUP_W, GROUP_W), lambda b_, s_: (0, 0, 0)),
            pl.BlockSpec((1, D), lambda b_, s_: (0, 0)),
            pl.BlockSpec((1, D), lambda b_, s_: (0, 0)),
            pl.BlockSpec((1, D), lambda b_, s_: (0, 0)),
        ],
        out_specs=pl.BlockSpec((1, ts, D), lambda b_, s_: (b_, s_, 0)),
        out_shape=jax.ShapeDtypeStruct((B, S, D), F32),
        compiler_params=_params(("parallel", "parallel"), vmem),
        name="pool_ln",
    )(x, x, w, scale, g, b)


def _ffn_ln_kernel(x_ref, wg_ref, wu_ref, wd_ref, g_ref, b_ref, o_ref, xb_ref, acc_ref, h_ref):
    xb_ref[...] = x_ref[...].astype(BF16)
    nc = wg_ref.shape[0]

    def hidden(c):
        xb = xb_ref[...]
        gate = jnp.dot(xb, wg_ref[c], preferred_element_type=F32)
        up = jnp.dot(xb, wu_ref[c], preferred_element_type=F32)
        h_ref[c % 2] = (gate * jax.nn.sigmoid(gate) * up).astype(BF16)

    def down(c):
        y = jnp.dot(h_ref[c % 2], wd_ref[c], preferred_element_type=F32)
        if c == 0:
            acc_ref[...] = y
        else:
            acc_ref[...] += y

    hidden(0)
    for c in range(nc):
        if c + 1 < nc:
            hidden(c + 1)
        down(c)
    o_ref[...] = _layernorm(ALPHA * x_ref[...] + acc_ref[...], g_ref[...], b_ref[...])


def _ffn_ln(x2d, wg, wu, wd, g, b):
    M, D = x2d.shape
    tm = ROW_TILE
    nc = wg.shape[0]
    w_bytes = 3 * nc * D * FF_CHUNK * 2
    vmem = w_bytes + 4 * tm * D * 4 + tm * D * 6 + 6 * tm * FF_CHUNK * 4
    resident = functools.partial(pl.BlockSpec, pipeline_mode=pl.Buffered(1))
    return pl.pallas_call(
        _ffn_ln_kernel,
        grid=(M // tm,),
        in_specs=[
            pl.BlockSpec((tm, D), lambda i: (i, 0)),
            resident((nc, D, FF_CHUNK), lambda i: (0, 0, 0)),
            resident((nc, D, FF_CHUNK), lambda i: (0, 0, 0)),
            resident((nc, FF_CHUNK, D), lambda i: (0, 0, 0)),
            pl.BlockSpec((1, D), lambda i: (0, 0)),
            pl.BlockSpec((1, D), lambda i: (0, 0)),
        ],
        out_specs=pl.BlockSpec((tm, D), lambda i: (i, 0)),
        out_shape=jax.ShapeDtypeStruct((M, D), F32),
        scratch_shapes=[pltpu.VMEM((tm, D), BF16), pltpu.VMEM((tm, D), F32),
                        pltpu.VMEM((2, tm, FF_CHUNK), BF16)],
        compiler_params=_params(("parallel",), vmem),
        name="ffn_ln",
    )(x2d, wg, wu, wd, g, b)


def _qkv_kernel(x_ref, wq_ref, wk_ref, wvt_ref, q_ref, k_ref, vt_ref):
    xb = x_ref[0].astype(BF16)
    q = jnp.dot(xb, wq_ref[...], preferred_element_type=F32) * SM_SCALE
    q_ref[0] = q.astype(BF16)
    k_ref[0] = jnp.dot(xb, wk_ref[...], preferred_element_type=F32).astype(BF16)
    vt = lax.dot_general(wvt_ref[...], xb, NT_DIMS, preferred_element_type=F32)
    vt_ref[0] = vt.astype(BF16)


def _qkv(x, wq, wk, wvt):
    B, S, D = x.shape
    tm = ROW_TILE
    vmem = 3 * D * D_ATTN * 2 + 2 * tm * D * 4 + 6 * tm * D_ATTN * 2 + tm * D * 2 + 3 * tm * D_ATTN * 4
    resident = functools.partial(pl.BlockSpec, pipeline_mode=pl.Buffered(1))
    return pl.pallas_call(
        _qkv_kernel,
        grid=(B, S // tm),
        in_specs=[
            pl.BlockSpec((1, tm, D), lambda b, i: (b, i, 0)),
            resident((D, D_ATTN), lambda b, i: (0, 0)),
            resident((D, D_ATTN), lambda b, i: (0, 0)),
            resident((D_ATTN, D), lambda b, i: (0, 0)),
        ],
        out_specs=[
            pl.BlockSpec((1, tm, D_ATTN), lambda b, i: (b, i, 0)),
            pl.BlockSpec((1, tm, D_ATTN), lambda b, i: (b, i, 0)),
            pl.BlockSpec((1, D_ATTN, tm), lambda b, i: (b, 0, i)),
        ],
        out_shape=[
            jax.ShapeDtypeStruct((B, S, D_ATTN), BF16),
            jax.ShapeDtypeStruct((B, S, D_ATTN), BF16),
            jax.ShapeDtypeStruct((B, D_ATTN, S), BF16),
        ],
        compiler_params=_params(("parallel", "parallel"), vmem),
        name="qkv_proj",
    )(x, wq, wk, wvt)


def _attn_kernel(lambda_init, q_ref, k_ref, vt_ref, bias_ref, lam_ref, sg_ref, o_ref,
                 qs_ref, m_ref, acc_ref, s_ref, ns_ref):
    i = pl.program_id(2)
    blk = ATT_BLOCK
    nq = q_ref.shape[1] // blk
    assert nq == 2, "the key-block pipeline below pairs blocks and needs an even count per tile"
    diag_bias = bias_ref[0, 1]
    prev_bias = bias_ref[0, 0]

    lane = lax.broadcasted_iota(jnp.int32, (blk, V_DIM), 1)
    for hq in range(nq):
        q = q_ref[0, hq * blk:(hq + 1) * blk, :]
        zero = jnp.zeros_like(q)
        qs_ref[2 * hq] = jnp.where(lane < HEAD_DIM, q, zero)
        qs_ref[2 * hq + 1] = jnp.where(lane >= HEAD_DIM, q, zero)

    def scores(kc, cb):
        start = pl.multiple_of(kc * blk, blk)
        k = k_ref[0, pl.ds(start, blk), :]
        return lax.dot_general(k, qs_ref[cb], NT_DIMS, preferred_element_type=F32)

    def values(kc):
        start = pl.multiple_of(kc * blk, blk)
        vt = vt_ref[0, :, pl.ds(start, blk)]
        return jnp.concatenate([vt, jnp.ones((SUM_ROWS, blk), BF16)], axis=0)

    def update(cb, s, vt, bias, first):
        if bias is not None:
            s = s + bias
        m_new = jnp.max(s, axis=0, keepdims=True)
        if not first:
            m_prev = m_ref[cb]
            m_new = jnp.maximum(m_prev, m_new)
        p = jnp.exp(s - m_new)
        pv = jnp.dot(vt, p.astype(BF16), preferred_element_type=F32)
        if first:
            acc_ref[cb] = pv
        else:
            acc_ref[cb] = acc_ref[cb] * jnp.exp(m_prev - m_new) + pv
        m_ref[cb] = m_new

    base = nq * i
    near = [(d, hq, c) for d in range(nq - 1, -1, -1) for hq in range(d, nq) for c in range(2)]

    def near_scores():
        for j, (d, hq, c) in enumerate(near):
            ns_ref[j] = scores(base + d, 2 * hq + c)

    def near_updates():
        vts = {d: values(base + d) for d in range(nq)}
        for j, (d, hq, c) in enumerate(near):
            bias = diag_bias if hq == d else prev_bias if hq == d + 1 else None
            update(2 * hq + c, ns_ref[j], vts[d], bias, hq == d)

    @pl.when(i == 0)
    def _():
        near_scores()
        near_updates()

    def issue(kc, slot):
        for cb in range(2 * nq):
            s_ref[slot, cb] = scores(kc, cb)

    def consume(kc, slot, last):
        vt = values(kc)
        for cb in range(2 * nq):
            bias = prev_bias if (last and cb < 2) else None
            update(cb, s_ref[slot, cb], vt, bias, False)

    @pl.when(i > 0)
    def _():
        near_scores()
        issue(0, 0)
        near_updates()

        def pair(t, carry):
            issue(2 * t + 1, 1)
            consume(2 * t, 0, False)
            issue(2 * t + 2, 0)
            consume(2 * t + 1, 1, False)
            return carry

        lax.fori_loop(0, i - 1, pair, 0)
        issue(base - 1, 1)
        consume(base - 2, 0, False)
        consume(base - 1, 1, True)


    lp = lam_ref[...]
    lam = (jnp.exp(jnp.sum(lp[0:1] * lp[1:2], axis=-1, keepdims=True))
           - jnp.exp(jnp.sum(lp[2:3] * lp[3:4], axis=-1, keepdims=True)) + lambda_init)
    gain = sg_ref[...] * (1.0 - lambda_init)
    for hq in range(nq):
        a0 = acc_ref[2 * hq]
        a1 = acc_ref[2 * hq + 1]
        ot = a0[:V_DIM] / a0[V_DIM:V_DIM + 1] - lam * (a1[:V_DIM] / a1[V_DIM:V_DIM + 1])
        o = ot.T
        o = o * lax.rsqrt(jnp.mean(o * o, axis=-1, keepdims=True) + LN_EPS)
        o_ref[0, hq * blk:(hq + 1) * blk, :] = (o * gain).astype(BF16)


def _rel_bucket(rel):
    n = jnp.maximum(rel, 0)
    nf = jnp.maximum(n, 1).astype(F32)
    large = MAX_EXACT + (jnp.log(nf / MAX_EXACT) / math.log(MAX_DISTANCE / MAX_EXACT)
                         * (NUM_BUCKETS - MAX_EXACT)).astype(jnp.int32)
    return jnp.where(n < MAX_EXACT, n, jnp.minimum(large, NUM_BUCKETS - 1))


def _near_bias(rel_table):
    blk = ATT_BLOCK
    key = jnp.arange(blk)[:, None]
    query = jnp.arange(blk)[None, :]
    rel = jnp.stack([query - key + blk, query - key])
    bucket = _rel_bucket(rel)
    table = rel_table.astype(F32)
    far = table[NUM_BUCKETS - 1]
    bias = jnp.zeros((N_HEADS,) + rel.shape, F32)
    for b in range(NUM_BUCKETS):
        bias = jnp.where(bucket[None] == b, (table[b] - far)[:, None, None, None], bias)
    return jnp.where(rel[None] >= 0, bias, MASK_VALUE)


def _attention(q, k, vt, bias, lam_p, subln_g, lambda_init):
    B, S, _ = q.shape
    tq = ATT_TILE
    blk = ATT_BLOCK
    nchain = 2 * (tq // blk)
    H = N_HEADS
    rows = V_DIM + SUM_ROWS
    vmem = 4 * S * V_DIM * 2 + 2 * 2 * blk * blk * 4 + 6 * nchain * blk * blk * 4 + 2 * nchain * rows * blk * 4
    return pl.pallas_call(
        functools.partial(_attn_kernel, lambda_init),
        grid=(B, H, S // tq),
        in_specs=[
            pl.BlockSpec((1, tq, V_DIM), lambda b, h, i: (b, i, h)),
            pl.BlockSpec((1, S, V_DIM), lambda b, h, i: (b, 0, h)),
            pl.BlockSpec((1, V_DIM, S), lambda b, h, i: (b, h, 0)),
            pl.BlockSpec((1, 2, blk, blk), lambda b, h, i: (h, 0, 0, 0)),
            pl.BlockSpec((4, HEAD_DIM), lambda b, h, i: (0, 0)),
            pl.BlockSpec((1, V_DIM), lambda b, h, i: (0, 0)),
        ],
        out_specs=pl.BlockSpec((1, tq, V_DIM), lambda b, h, i: (b, i, h)),
        out_shape=jax.ShapeDtypeStruct((B, S, D_ATTN), BF16),
        scratch_shapes=[
            pltpu.VMEM((nchain, blk, V_DIM), BF16),
            pltpu.VMEM((nchain, 1, blk), F32),
            pltpu.VMEM((nchain, rows, blk), F32),
            pltpu.VMEM((2, nchain, blk, blk), F32),
            pltpu.VMEM((nchain * 3 // 2, blk, blk), F32),
        ],
        compiler_params=_params(("parallel", "parallel", "arbitrary"), vmem),
        name="diff_attn",
    )(q, k, vt, bias, lam_p, subln_g)


def _oproj_ln_kernel(x_ref, o_ref_in, w_ref, g_ref, b_ref, out_ref):
    h = jnp.dot(o_ref_in[...], w_ref[...], preferred_element_type=F32)
    out_ref[...] = _layernorm(ALPHA * x_ref[...] + h, g_ref[...], b_ref[...])


def _oproj_ln(x2d, o2d, w, g, b):
    M, D = x2d.shape
    tm = ROW_TILE
    vmem = 2 * D * D * 2 + 4 * tm * D * 4 + 2 * tm * D * 2 + 4 * tm * D * 4
    return pl.pallas_call(
        _oproj_ln_kernel,
        grid=(M // tm,),
        in_specs=[
            pl.BlockSpec((tm, D), lambda i: (i, 0)),
            pl.BlockSpec((tm, D), lambda i: (i, 0)),
            pl.BlockSpec((D, D), lambda i: (0, 0)),
            pl.BlockSpec((1, D), lambda i: (0, 0)),
            pl.BlockSpec((1, D), lambda i: (0, 0)),
        ],
        out_specs=pl.BlockSpec((tm, D), lambda i: (i, 0)),
        out_shape=jax.ShapeDtypeStruct((M, D), F32),
        compiler_params=_params(("parallel",), vmem),
        name="oproj_ln",
    )(x2d, o2d, w, g, b)


def _chunk_cols(w):
    D, F = w.shape
    return w.astype(BF16).reshape(D, F // FF_CHUNK, FF_CHUNK).transpose(1, 0, 2)


def _chunk_rows(w):
    F, D = w.shape
    return w.astype(BF16).reshape(F // FF_CHUNK, FF_CHUNK, D)


def kernel(x, pool_w, pool_scale, w_qkv, w_o, lam_p, subln_g, rel_table, w_gate, w_up, w_down,
           ln_mix_g, ln_mix_b, ln_ffn_g, ln_ffn_b):
    B, S, D = x.shape
    M = B * S
    row = lambda a: a.reshape(1, -1).astype(F32)

    def ffn(x2d, i):
        return _ffn_ln(x2d, _chunk_cols(w_gate[i]), _chunk_cols(w_up[i]), _chunk_rows(w_down[i]),
                       row(ln_ffn_g[i]), row(ln_ffn_b[i]))

    x1 = _pool_ln(x, pool_w[0].astype(BF16), row(pool_scale[0]), row(ln_mix_g[0]), row(ln_mix_b[0]))
    x2 = ffn(x1.reshape(M, D), 0)

    lambda_init = 0.8 - 0.6 * math.exp(-0.3 * 1)
    wqkv = w_qkv[0].astype(BF16)
    q, k, vt = _qkv(x2.reshape(B, S, D), wqkv[:, :D_ATTN], wqkv[:, D_ATTN:2 * D_ATTN],
                    wqkv[:, 2 * D_ATTN:].T)
    o = _attention(q, k, vt, _near_bias(rel_table), lam_p[0].astype(F32), row(subln_g[0]), lambda_init)
    x3 = _oproj_ln(x2, o.reshape(M, D_ATTN), w_o[0].astype(BF16), row(ln_mix_g[1]), row(ln_mix_b[1]))
    x4 = ffn(x3, 1)
    return x4.reshape(B, S, D)
```

```python
import functools
import math

import jax
import jax.numpy as jnp
from jax import lax
from jax.experimental import pallas as pl
from jax.experimental.pallas import tpu as pltpu

D_MODEL = 1024
DEPTH = 2
ALPHA = (2.0 * DEPTH) ** 0.25
LN_EPS = 1e-5
POOL_WINDOWS = (2, 4, 8, 16)
N_GROUPS = len(POOL_WINDOWS)
GROUP_W = D_MODEL // N_GROUPS
POOL_HALO = 16
HEAD_DIM = 64
N_HEADS = D_MODEL // (2 * HEAD_DIM)
V_DIM = 2 * HEAD_DIM
D_ATTN = N_HEADS * V_DIM
NUM_BUCKETS = 32
MAX_EXACT = NUM_BUCKETS // 2
MAX_DISTANCE = 128
D_FF = 2816
SM_SCALE = HEAD_DIM ** -0.5
LOG2_E = math.log2(math.e)

FF_CHUNK = 256
ROW_TILE = 512
ATT_BLOCK = 256
ATT_TILE = 1024
SUM_ROWS = 16
MASK_VALUE = -1e30
V7X_VMEM_BYTES = 64 * 1024 * 1024

F32 = jnp.float32
BF16 = jnp.bfloat16
NT_DIMS = (((1,), (1,)), ((), ()))


def _params(semantics, vmem_bytes):
    limit = min(int(vmem_bytes * 1.25) + (4 << 20), V7X_VMEM_BYTES - (6 << 20))
    return pltpu.CompilerParams(dimension_semantics=semantics, vmem_limit_bytes=limit)


def _resident(shape):
    return pl.BlockSpec(shape, lambda *_: (0,) * len(shape), pipeline_mode=pl.Buffered(1))


def _layernorm(z, g, b):
    mu = jnp.mean(z, axis=-1, keepdims=True)
    zc = z - mu
    var = jnp.mean(zc * zc, axis=-1, keepdims=True)
    return zc * lax.rsqrt(var + LN_EPS) * g + b


def _ffn_ln(y_ref, wg_ref, wu_ref, wd_ref, g_ref, b_ref, xb_ref, acc_ref, h_ref):
    xb_ref[...] = y_ref[...].astype(BF16)
    nc = wg_ref.shape[1] // FF_CHUNK

    def hidden(c):
        cols = slice(c * FF_CHUNK, (c + 1) * FF_CHUNK)
        xb = xb_ref[...]
        gate = jnp.dot(xb, wg_ref[:, cols], preferred_element_type=F32)
        up = jnp.dot(xb, wu_ref[:, cols], preferred_element_type=F32)
        h_ref[c % 2] = (gate * jax.nn.sigmoid(gate) * up).astype(BF16)

    def down(c):
        rows = slice(c * FF_CHUNK, (c + 1) * FF_CHUNK)
        y = jnp.dot(h_ref[c % 2], wd_ref[rows, :], preferred_element_type=F32)
        if c == 0:
            acc_ref[...] = y
        else:
            acc_ref[...] += y

    hidden(0)
    for c in range(nc):
        if c + 1 < nc:
            hidden(c + 1)
        down(c)
    return _layernorm(ALPHA * y_ref[...] + acc_ref[...], g_ref[...], b_ref[...])


def _ffn_specs():
    return [_resident((D_MODEL, D_FF)), _resident((D_MODEL, D_FF)), _resident((D_FF, D_MODEL)),
            _resident((1, D_MODEL)), _resident((1, D_MODEL))]


def _ffn_scratch(tm):
    return [pltpu.VMEM((tm, D_MODEL), F32), pltpu.VMEM((tm, D_MODEL), BF16), pltpu.VMEM((tm, D_MODEL), F32),
            pltpu.VMEM((2, tm, FF_CHUNK), BF16)]


def _ffn_vmem(tm):
    return 3 * D_MODEL * D_FF * 2 + 4 * tm * D_MODEL * 4 + tm * D_MODEL * 10 + 8 * tm * FF_CHUNK * 4


def _layer0_kernel(x_ref, halo_ref, pw_ref, ps_ref, g0_ref, b0_ref, wg_ref, wu_ref, wd_ref, g1_ref, b1_ref,
                   o_ref, y_ref, xb_ref, acc_ref, h_ref):
    s = pl.program_id(1)
    ts = x_ref.shape[1]
    x = x_ref[0]
    halo = jnp.where(s > 0, halo_ref[0], 0.0)
    xh = jnp.concatenate([halo, x], axis=0)
    t = s * ts + lax.broadcasted_iota(jnp.int32, (ts, 1), 0)
    mixed = []
    for g, win in enumerate(POOL_WINDOWS):
        a = xh[:, g * GROUP_W:(g + 1) * GROUP_W]
        shift = 1
        while shift < win:
            a = a + pltpu.roll(a, shift, 0)
            shift *= 2
        cnt = jnp.minimum(t + 1, win).astype(F32)
        d = a[POOL_HALO:] / cnt - x[:, g * GROUP_W:(g + 1) * GROUP_W]
        mixed.append(jnp.dot(d.astype(BF16), pw_ref[g], preferred_element_type=F32))
    h = jnp.concatenate(mixed, axis=-1) * ps_ref[...]
    y_ref[...] = _layernorm(ALPHA * x + h, g0_ref[...], b0_ref[...])
    o_ref[0] = _ffn_ln(y_ref, wg_ref, wu_ref, wd_ref, g1_ref, b1_ref, xb_ref, acc_ref, h_ref)


def _layer0(x, pw, ps, g0, b0, wg, wu, wd, g1, b1):
    B, S, D = x.shape
    ts = ROW_TILE
    halo_blocks = ts // POOL_HALO
    vmem = _ffn_vmem(ts) + 10 * ts * D * 4
    return pl.pallas_call(
        _layer0_kernel,
        grid=(B, S // ts),
        in_specs=[
            pl.BlockSpec((1, ts, D), lambda b_, s_: (b_, s_, 0)),
            pl.BlockSpec((1, POOL_HALO, D), lambda b_, s_: (b_, jnp.maximum(s_ * halo_blocks - 1, 0), 0)),
            _resident((N_GROUPS, GROUP_W, GROUP_W)), _resident((1, D)), _resident((1, D)), _resident((1, D)),
        ] + _ffn_specs(),
        out_specs=pl.BlockSpec((1, ts, D), lambda b_, s_: (b_, s_, 0)),
        out_shape=jax.ShapeDtypeStruct((B, S, D), F32),
        scratch_shapes=_ffn_scratch(ts),
        compiler_params=_params(("parallel", "parallel"), vmem),
        name="layer0_pool_ffn",
    )(x, x, pw, ps, g0, b0, wg, wu, wd, g1, b1)


def _qkv_kernel(x_ref, wq_ref, wk_ref, wvt_ref, q_ref, k_ref, vt_ref):
    xb = x_ref[0].astype(BF16)
    q = jnp.dot(xb, wq_ref[...], preferred_element_type=F32) * (SM_SCALE * LOG2_E)
    q_ref[0] = q.astype(BF16)
    k_ref[0] = jnp.dot(xb, wk_ref[...], preferred_element_type=F32).astype(BF16)
    vt = lax.dot_general(wvt_ref[...], xb, NT_DIMS, preferred_element_type=F32)
    vt_ref[0] = vt.astype(BF16)


def _qkv(x, wq, wk, wvt):
    B, S, D = x.shape
    tm = ROW_TILE
    vmem = 3 * D * D_ATTN * 2 + 2 * tm * D * 4 + 6 * tm * D_ATTN * 2 + tm * D * 2 + 3 * tm * D_ATTN * 4
    return pl.pallas_call(
        _qkv_kernel,
        grid=(B, S // tm),
        in_specs=[
            pl.BlockSpec((1, tm, D), lambda b, i: (b, i, 0)),
            _resident((D, D_ATTN)), _resident((D, D_ATTN)), _resident((D_ATTN, D)),
        ],
        out_specs=[
            pl.BlockSpec((1, tm, D_ATTN), lambda b, i: (b, i, 0)),
            pl.BlockSpec((1, tm, D_ATTN), lambda b, i: (b, i, 0)),
            pl.BlockSpec((1, D_ATTN, tm), lambda b, i: (b, 0, i)),
        ],
        out_shape=[
            jax.ShapeDtypeStruct((B, S, D_ATTN), BF16),
            jax.ShapeDtypeStruct((B, S, D_ATTN), BF16),
            jax.ShapeDtypeStruct((B, D_ATTN, S), BF16),
        ],
        compiler_params=_params(("parallel", "parallel"), vmem),
        name="qkv_proj",
    )(x, wq, wk, wvt)


def _attn_kernel(lambda_init, q_ref, k_ref, vt_ref, bias_ref, lam_ref, sg_ref, o_ref,
                 qs_ref, m_ref, acc_ref, s_ref, ns_ref):
    i = pl.program_id(2)
    blk = ATT_BLOCK
    nq = q_ref.shape[1] // blk
    assert nq % 2 == 0, "the key-block pipeline below pairs blocks and needs an even count per tile"
    diag_bias = bias_ref[0, 1]
    prev_bias = bias_ref[0, 0]

    lane = lax.broadcasted_iota(jnp.int32, (blk, V_DIM), 1)
    for hq in range(nq):
        q = q_ref[0, hq * blk:(hq + 1) * blk, :]
        zero = jnp.zeros_like(q)
        qs_ref[2 * hq] = jnp.where(lane < HEAD_DIM, q, zero)
        qs_ref[2 * hq + 1] = jnp.where(lane >= HEAD_DIM, q, zero)

    def scores(kc, cb):
        start = pl.multiple_of(kc * blk, blk)
        k = k_ref[0, pl.ds(start, blk), :]
        return lax.dot_general(k, qs_ref[cb], NT_DIMS, preferred_element_type=F32)

    def values(kc):
        start = pl.multiple_of(kc * blk, blk)
        vt = vt_ref[0, :, pl.ds(start, blk)]
        return jnp.concatenate([vt, jnp.ones((SUM_ROWS, blk), BF16)], axis=0)

    def update(cb, s, vt, bias, first):
        if bias is not None:
            s = s + bias
        m_new = jnp.max(s, axis=0, keepdims=True)
        if not first:
            m_prev = m_ref[cb]
            m_new = jnp.maximum(m_prev, m_new)
        p = jnp.exp2(s - m_new)
        pv = jnp.dot(vt, p.astype(BF16), preferred_element_type=F32)
        if first:
            acc_ref[cb] = pv
        else:
            acc_ref[cb] = acc_ref[cb] * jnp.exp2(m_prev - m_new) + pv
        m_ref[cb] = m_new

    base = nq * i
    near = [(d, hq, c) for d in range(nq - 1, -1, -1) for hq in range(d, nq) for c in range(2)]

    def near_scores():
        for j, (d, hq, c) in enumerate(near):
            ns_ref[j] = scores(base + d, 2 * hq + c)

    def near_updates():
        vts = {d: values(base + d) for d in range(nq)}
        for j, (d, hq, c) in enumerate(near):
            bias = diag_bias if hq == d else prev_bias if hq == d + 1 else None
            update(2 * hq + c, ns_ref[j], vts[d], bias, hq == d)

    @pl.when(i == 0)
    def _():
        near_scores()
        near_updates()

    def issue(kc, slot):
        for cb in range(2 * nq):
            s_ref[slot, cb] = scores(kc, cb)

    def consume(kc, slot, last):
        vt = values(kc)
        for cb in range(2 * nq):
            bias = prev_bias if (last and cb < 2) else None
            update(cb, s_ref[slot, cb], vt, bias, False)

    @pl.when(i > 0)
    def _():
        near_scores()
        issue(0, 0)
        near_updates()

        def pair(t, carry):
            issue(2 * t + 1, 1)
            consume(2 * t, 0, False)
            issue(2 * t + 2, 0)
            consume(2 * t + 1, 1, False)
            return carry

        lax.fori_loop(0, (nq // 2) * i - 1, pair, 0)
        issue(base - 1, 1)
        consume(base - 2, 0, False)
        consume(base - 1, 1, True)

    lp = lam_ref[...]
    lam = (jnp.exp(jnp.sum(lp[0:1] * lp[1:2], axis=-1, keepdims=True))
           - jnp.exp(jnp.sum(lp[2:3] * lp[3:4], axis=-1, keepdims=True)) + lambda_init)
    gain = sg_ref[...] * (1.0 - lambda_init)
    for hq in range(nq):
        a0 = acc_ref[2 * hq]
        a1 = acc_ref[2 * hq + 1]
        ot = a0[:V_DIM] / a0[V_DIM:V_DIM + 1] - lam * (a1[:V_DIM] / a1[V_DIM:V_DIM + 1])
        o = ot.T
        o = o * lax.rsqrt(jnp.mean(o * o, axis=-1, keepdims=True) + LN_EPS)
        o_ref[0, hq * blk:(hq + 1) * blk, :] = (o * gain).astype(BF16)


def _rel_bucket(rel):
    n = jnp.maximum(rel, 0)
    nf = jnp.maximum(n, 1).astype(F32)
    large = MAX_EXACT + (jnp.log(nf / MAX_EXACT) / math.log(MAX_DISTANCE / MAX_EXACT)
                         * (NUM_BUCKETS - MAX_EXACT)).astype(jnp.int32)
    return jnp.where(n < MAX_EXACT, n, jnp.minimum(large, NUM_BUCKETS - 1))


def _near_bias(rel_table):
    blk = ATT_BLOCK
    key = jnp.arange(blk)[:, None]
    query = jnp.arange(blk)[None, :]
    rel = jnp.stack([query - key + blk, query - key])
    bucket = _rel_bucket(rel)
    table = rel_table.astype(F32)
    table = (table - table[NUM_BUCKETS - 1]) * LOG2_E
    bias = jnp.zeros((N_HEADS,) + rel.shape, F32)
    for b in range(NUM_BUCKETS):
        bias = jnp.where(bucket[None] == b, table[b][:, None, None, None], bias)
    return jnp.where(rel[None] >= 0, bias, MASK_VALUE)


def _attention(q, k, vt, bias, lam_p, subln_g, lambda_init):
    B, S, _ = q.shape
    tq = ATT_TILE
    blk = ATT_BLOCK
    nq = tq // blk
    nchain = 2 * nq
    nnear = nq * (nq + 1)
    H = N_HEADS
    rows = V_DIM + SUM_ROWS
    vmem = (4 * S * V_DIM * 2 + 2 * 2 * blk * blk * 4 + (3 * nchain + nnear) * blk * blk * 4
            + 2 * nchain * rows * blk * 4 + 4 * tq * V_DIM * 2)
    return pl.pallas_call(
        functools.partial(_attn_kernel, lambda_init),
        grid=(B, H, S // tq),
        in_specs=[
            pl.BlockSpec((1, tq, V_DIM), lambda b, h, i: (b, i, h)),
            pl.BlockSpec((1, S, V_DIM), lambda b, h, i: (b, 0, h)),
            pl.BlockSpec((1, V_DIM, S), lambda b, h, i: (b, h, 0)),
            pl.BlockSpec((1, 2, blk, blk), lambda b, h, i: (h, 0, 0, 0)),
            pl.BlockSpec((4, HEAD_DIM), lambda b, h, i: (0, 0)),
            pl.BlockSpec((1, V_DIM), lambda b, h, i: (0, 0)),
        ],
        out_specs=pl.BlockSpec((1, tq, V_DIM), lambda b, h, i: (b, i, h)),
        out_shape=jax.ShapeDtypeStruct((B, S, D_ATTN), BF16),
        scratch_shapes=[
            pltpu.VMEM((nchain, blk, V_DIM), BF16),
            pltpu.VMEM((nchain, 1, blk), F32),
            pltpu.VMEM((nchain, rows, blk), F32),
            pltpu.VMEM((2, nchain, blk, blk), F32),
            pltpu.VMEM((nnear, blk, blk), F32),
        ],
        compiler_params=_params(("parallel", "parallel", "arbitrary"), vmem),
        name="diff_attn",
    )(q, k, vt, bias, lam_p, subln_g)


def _layer1_kernel(x_ref, a_ref, wo_ref, g0_ref, b0_ref, wg_ref, wu_ref, wd_ref, g1_ref, b1_ref,
                   o_ref, y_ref, xb_ref, acc_ref, h_ref):
    h = jnp.dot(a_ref[...], wo_ref[...], preferred_element_type=F32)
    y_ref[...] = _layernorm(ALPHA * x_ref[...] + h, g0_ref[...], b0_ref[...])
    o_ref[...] = _ffn_ln(y_ref, wg_ref, wu_ref, wd_ref, g1_ref, b1_ref, xb_ref, acc_ref, h_ref)


def _layer1(x2d, a2d, wo, g0, b0, wg, wu, wd, g1, b1):
    M, D = x2d.shape
    tm = ROW_TILE
    vmem = _ffn_vmem(tm) + D * D_ATTN * 2 + 2 * tm * D_ATTN * 2 + 4 * tm * D * 4
    return pl.pallas_call(
        _layer1_kernel,
        grid=(M // tm,),
        in_specs=[
            pl.BlockSpec((tm, D), lambda i: (i, 0)),
            pl.BlockSpec((tm, D_ATTN), lambda i: (i, 0)),
            _resident((D_ATTN, D)), _resident((1, D)), _resident((1, D)),
        ] + _ffn_specs(),
        out_specs=pl.BlockSpec((tm, D), lambda i: (i, 0)),
        out_shape=jax.ShapeDtypeStruct((M, D), F32),
        scratch_shapes=_ffn_scratch(tm),
        compiler_params=_params(("parallel",), vmem),
        name="layer1_oproj_ffn",
    )(x2d, a2d, wo, g0, b0, wg, wu, wd, g1, b1)


def kernel(x, pool_w, pool_scale, w_qkv, w_o, lam_p, subln_g, rel_table, w_gate, w_up, w_down,
           ln_mix_g, ln_mix_b, ln_ffn_g, ln_ffn_b):
    B, S, D = x.shape
    M = B * S
    row = lambda a: a.reshape(1, -1).astype(F32)
    ffn_w = lambda i: (w_gate[i].astype(BF16), w_up[i].astype(BF16), w_down[i].astype(BF16),
                       row(ln_ffn_g[i]), row(ln_ffn_b[i]))

    x2 = _layer0(x, pool_w[0].astype(BF16), row(pool_scale[0]), row(ln_mix_g[0]), row(ln_mix_b[0]), *ffn_w(0))

    lambda_init = 0.8 - 0.6 * math.exp(-0.3 * 1)
    wqkv = w_qkv[0].astype(BF16)
    q, k, vt = _qkv(x2, wqkv[:, :D_ATTN], wqkv[:, D_ATTN:2 * D_ATTN], wqkv[:, 2 * D_ATTN:].T)
    a = _attention(q, k, vt, _near_bias(rel_table), lam_p[0].astype(F32), row(subln_g[0]), lambda_init)
    x4 = _layer1(x2.reshape(M, D), a.reshape(M, D_ATTN), w_o[0].astype(BF16), row(ln_mix_g[1]),
                 row(ln_mix_b[1]), *ffn_w(1))
    return x4.reshape(B, S, D)
```

```python
import functools
import math

import jax
import jax.numpy as jnp
from jax import lax
from jax.experimental import pallas as pl
from jax.experimental.pallas import tpu as pltpu

D_MODEL = 1024
DEPTH = 2
ALPHA = (2.0 * DEPTH) ** 0.25
LN_EPS = 1e-5
POOL_WINDOWS = (2, 4, 8, 16)
N_GROUPS = len(POOL_WINDOWS)
GROUP_W = D_MODEL // N_GROUPS
POOL_HALO = 16
HEAD_DIM = 64
N_HEADS = D_MODEL // (2 * HEAD_DIM)
V_DIM = 2 * HEAD_DIM
D_ATTN = N_HEADS * V_DIM
NUM_BUCKETS = 32
MAX_EXACT = NUM_BUCKETS // 2
MAX_DISTANCE = 128
D_FF = 2816
SM_SCALE = HEAD_DIM ** -0.5
LOG2_E = math.log2(math.e)

FF_CHUNK = 256
ROW_TILE = 512
ATT_BLOCK = 256
ATT_TILE = 1024
SUM_ROWS = 16
MASK_VALUE = -1e30
V7X_VMEM_BYTES = 64 * 1024 * 1024

F32 = jnp.float32
BF16 = jnp.bfloat16
NT_DIMS = (((1,), (1,)), ((), ()))


def _params(semantics, vmem_bytes):
    limit = min(int(vmem_bytes * 1.25) + (4 << 20), V7X_VMEM_BYTES - (6 << 20))
    return pltpu.CompilerParams(dimension_semantics=semantics, vmem_limit_bytes=limit)


def _resident(shape):
    return pl.BlockSpec(shape, lambda *_: (0,) * len(shape), pipeline_mode=pl.Buffered(1))


def _layernorm(z, g, b):
    mu = jnp.mean(z, axis=-1, keepdims=True)
    zc = z - mu
    var = jnp.mean(zc * zc, axis=-1, keepdims=True)
    return zc * lax.rsqrt(var + LN_EPS) * g + b


def _ffn_ln(y_ref, wg_ref, wu_ref, wd_ref, g_ref, b_ref, xb_ref, acc_ref, h_ref):
    xb_ref[...] = y_ref[...].astype(BF16)
    nc = wg_ref.shape[1] // FF_CHUNK

    def hidden(c):
        cols = slice(c * FF_CHUNK, (c + 1) * FF_CHUNK)
        xb = xb_ref[...]
        gate = jnp.dot(xb, wg_ref[:, cols], preferred_element_type=F32)
        up = jnp.dot(xb, wu_ref[:, cols], preferred_element_type=F32)
        h_ref[c % 2] = (gate * jax.nn.sigmoid(gate) * up).astype(BF16)

    def down(c):
        rows = slice(c * FF_CHUNK, (c + 1) * FF_CHUNK)
        y = jnp.dot(h_ref[c % 2], wd_ref[rows, :], preferred_element_type=F32)
        if c == 0:
            acc_ref[...] = y
        else:
            acc_ref[...] += y

    hidden(0)
    for c in range(nc):
        if c + 1 < nc:
            hidden(c + 1)
        down(c)
    return _layernorm(ALPHA * y_ref[...] + acc_ref[...], g_ref[...], b_ref[...])


def _ffn_specs():
    return [_resident((D_MODEL, D_FF)), _resident((D_MODEL, D_FF)), _resident((D_FF, D_MODEL)),
            _resident((1, D_MODEL)), _resident((1, D_MODEL))]


def _ffn_scratch(tm):
    return [pltpu.VMEM((tm, D_MODEL), F32), pltpu.VMEM((tm, D_MODEL), BF16), pltpu.VMEM((tm, D_MODEL), F32),
            pltpu.VMEM((2, tm, FF_CHUNK), BF16)]


def _ffn_vmem(tm):
    return 3 * D_MODEL * D_FF * 2 + 4 * tm * D_MODEL * 4 + tm * D_MODEL * 10 + 8 * tm * FF_CHUNK * 4


def _layer0_kernel(x_ref, halo_ref, pw_ref, ps_ref, g0_ref, b0_ref, wg_ref, wu_ref, wd_ref, g1_ref, b1_ref,
                   o_ref, y_ref, xb_ref, acc_ref, h_ref):
    s = pl.program_id(1)
    ts = x_ref.shape[1]
    x = x_ref[0]
    halo = jnp.where(s > 0, halo_ref[0], 0.0)
    xh = jnp.concatenate([halo, x], axis=0)
    t = s * ts + lax.broadcasted_iota(jnp.int32, (ts, 1), 0)
    mixed = []
    for g, win in enumerate(POOL_WINDOWS):
        a = xh[:, g * GROUP_W:(g + 1) * GROUP_W]
        shift = 1
        while shift < win:
            a = a + pltpu.roll(a, shift, 0)
            shift *= 2
        cnt = jnp.minimum(t + 1, win).astype(F32)
        d = a[POOL_HALO:] / cnt - x[:, g * GROUP_W:(g + 1) * GROUP_W]
        mixed.append(jnp.dot(d.astype(BF16), pw_ref[g], preferred_element_type=F32))
    h = jnp.concatenate(mixed, axis=-1) * ps_ref[...]
    y_ref[...] = _layernorm(ALPHA * x + h, g0_ref[...], b0_ref[...])
    o_ref[0] = _ffn_ln(y_ref, wg_ref, wu_ref, wd_ref, g1_ref, b1_ref, xb_ref, acc_ref, h_ref)


def _layer0(x, pw, ps, g0, b0, wg, wu, wd, g1, b1):
    B, S, D = x.shape
    ts = ROW_TILE
    halo_blocks = ts // POOL_HALO
    vmem = _ffn_vmem(ts) + 10 * ts * D * 4
    return pl.pallas_call(
        _layer0_kernel,
        grid=(B, S // ts),
        in_specs=[
            pl.BlockSpec((1, ts, D), lambda b_, s_: (b_, s_, 0)),
            pl.BlockSpec((1, POOL_HALO, D), lambda b_, s_: (b_, jnp.maximum(s_ * halo_blocks - 1, 0), 0)),
            _resident((N_GROUPS, GROUP_W, GROUP_W)), _resident((1, D)), _resident((1, D)), _resident((1, D)),
        ] + _ffn_specs(),
        out_specs=pl.BlockSpec((1, ts, D), lambda b_, s_: (b_, s_, 0)),
        out_shape=jax.ShapeDtypeStruct((B, S, D), F32),
        scratch_shapes=_ffn_scratch(ts),
        compiler_params=_params(("parallel", "parallel"), vmem),
        name="layer0_pool_ffn",
    )(x, x, pw, ps, g0, b0, wg, wu, wd, g1, b1)


def _qkv_kernel(x_ref, wq_ref, wk_ref, wvt_ref, q_ref, k_ref, vt_ref):
    xb = x_ref[0].astype(BF16)
    q = jnp.dot(xb, wq_ref[...], preferred_element_type=F32) * (SM_SCALE * LOG2_E)
    q_ref[0] = q.astype(BF16)
    k_ref[0] = jnp.dot(xb, wk_ref[...], preferred_element_type=F32).astype(BF16)
    vt = lax.dot_general(wvt_ref[...], xb, NT_DIMS, preferred_element_type=F32)
    vt_ref[0] = vt.astype(BF16)


def _qkv(x, wq, wk, wvt):
    B, S, D = x.shape
    tm = ROW_TILE
    vmem = 3 * D * D_ATTN * 2 + 2 * tm * D * 4 + 6 * tm * D_ATTN * 2 + tm * D * 2 + 3 * tm * D_ATTN * 4
    return pl.pallas_call(
        _qkv_kernel,
        grid=(B, S // tm),
        in_specs=[
            pl.BlockSpec((1, tm, D), lambda b, i: (b, i, 0)),
            _resident((D, D_ATTN)), _resident((D, D_ATTN)), _resident((D_ATTN, D)),
        ],
        out_specs=[
            pl.BlockSpec((1, tm, D_ATTN), lambda b, i: (b, i, 0)),
            pl.BlockSpec((1, tm, D_ATTN), lambda b, i: (b, i, 0)),
            pl.BlockSpec((1, D_ATTN, tm), lambda b, i: (b, 0, i)),
        ],
        out_shape=[
            jax.ShapeDtypeStruct((B, S, D_ATTN), BF16),
            jax.ShapeDtypeStruct((B, S, D_ATTN), BF16),
            jax.ShapeDtypeStruct((B, D_ATTN, S), BF16),
        ],
        compiler_params=_params(("parallel", "parallel"), vmem),
        name="qkv_proj",
    )(x, wq, wk, wvt)


def _attn_kernel(lambda_init, q_ref, k_ref, vt_ref, bias_ref, lam_ref, sg_ref, o_ref,
                 qs_ref, m_ref, acc_ref, s_ref, ns_ref):
    i = pl.program_id(2)
    blk = ATT_BLOCK
    nq = q_ref.shape[1] // blk
    assert nq % 2 == 0, "the key-block pipeline below pairs blocks and needs an even count per tile"
    diag_bias = bias_ref[0, 1]
    prev_bias = bias_ref[0, 0]

    lane = lax.broadcasted_iota(jnp.int32, (blk, V_DIM), 1)
    for hq in range(nq):
        q = q_ref[0, hq * blk:(hq + 1) * blk, :]
        zero = jnp.zeros_like(q)
        qs_ref[2 * hq] = jnp.where(lane < HEAD_DIM, q, zero)
        qs_ref[2 * hq + 1] = jnp.where(lane >= HEAD_DIM, q, zero)

    def scores(kc, cb):
        start = pl.multiple_of(kc * blk, blk)
        k = k_ref[0, pl.ds(start, blk), :]
        return lax.dot_general(k, qs_ref[cb], NT_DIMS, preferred_element_type=F32)

    def values(kc):
        start = pl.multiple_of(kc * blk, blk)
        vt = vt_ref[0, :, pl.ds(start, blk)]
        return jnp.concatenate([vt, jnp.ones((SUM_ROWS, blk), BF16)], axis=0)

    def update(cb, s, vt, bias, first):
        if bias is not None:
            s = s + bias
        m_new = jnp.max(s, axis=0, keepdims=True)
        if not first:
            m_prev = m_ref[cb]
            m_new = jnp.maximum(m_prev, m_new)
        p = jnp.exp2(s - m_new)
        pv = jnp.dot(vt, p.astype(BF16), preferred_element_type=F32)
        if first:
            acc_ref[cb] = pv
        else:
            acc_ref[cb] = acc_ref[cb] * jnp.exp2(m_prev - m_new) + pv
        m_ref[cb] = m_new

    base = nq * i
    near = [(d, hq, c) for d in range(nq - 1, -1, -1) for hq in range(d, nq) for c in range(2)]

    def near_scores():
        for j, (d, hq, c) in enumerate(near):
            ns_ref[j] = scores(base + d, 2 * hq + c)

    def near_updates():
        vts = {d: values(base + d) for d in range(nq)}
        for j, (d, hq, c) in enumerate(near):
            bias = diag_bias if hq == d else prev_bias if hq == d + 1 else None
            update(2 * hq + c, ns_ref[j], vts[d], bias, hq == d)

    @pl.when(i == 0)
    def _():
        near_scores()
        near_updates()

    def issue(kc, slot):
        for cb in range(2 * nq):
            s_ref[slot, cb] = scores(kc, cb)

    def consume(kc, slot, last):
        vt = values(kc)
        for cb in range(2 * nq):
            bias = prev_bias if (last and cb < 2) else None
            update(cb, s_ref[slot, cb], vt, bias, False)

    @pl.when(i > 0)
    def _():
        near_scores()
        issue(0, 0)
        near_updates()

        def pair(t, carry):
            issue(2 * t + 1, 1)
            consume(2 * t, 0, False)
            issue(2 * t + 2, 0)
            consume(2 * t + 1, 1, False)
            return carry

        lax.fori_loop(0, (nq // 2) * i - 1, pair, 0)
        issue(base - 1, 1)
        consume(base - 2, 0, False)
        consume(base - 1, 1, True)

    lp = lam_ref[...]
    lam = (jnp.exp(jnp.sum(lp[0:1] * lp[1:2], axis=-1, keepdims=True))
           - jnp.exp(jnp.sum(lp[2:3] * lp[3:4], axis=-1, keepdims=True)) + lambda_init)
    gain = sg_ref[...] * (1.0 - lambda_init)
    for hq in range(nq):
        a0 = acc_ref[2 * hq]
        a1 = acc_ref[2 * hq + 1]
        ot = a0[:V_DIM] / a0[V_DIM:V_DIM + 1] - lam * (a1[:V_DIM] / a1[V_DIM:V_DIM + 1])
        ot = ot * lax.rsqrt(jnp.mean(ot * ot, axis=0, keepdims=True) + LN_EPS)
        o_ref[0, :, hq * blk:(hq + 1) * blk] = (ot * gain).astype(BF16)


def _rel_bucket(rel):
    n = jnp.maximum(rel, 0)
    nf = jnp.maximum(n, 1).astype(F32)
    large = MAX_EXACT + (jnp.log(nf / MAX_EXACT) / math.log(MAX_DISTANCE / MAX_EXACT)
                         * (NUM_BUCKETS - MAX_EXACT)).astype(jnp.int32)
    return jnp.where(n < MAX_EXACT, n, jnp.minimum(large, NUM_BUCKETS - 1))


def _near_bias(rel_table):
    blk = ATT_BLOCK
    key = jnp.arange(blk)[:, None]
    query = jnp.arange(blk)[None, :]
    rel = jnp.stack([query - key + blk, query - key])
    bucket = _rel_bucket(rel)
    table = rel_table.astype(F32)
    table = (table - table[NUM_BUCKETS - 1]) * LOG2_E
    bias = jnp.zeros((N_HEADS,) + rel.shape, F32)
    for b in range(NUM_BUCKETS):
        bias = jnp.where(bucket[None] == b, table[b][:, None, None, None], bias)
    return jnp.where(rel[None] >= 0, bias, MASK_VALUE)


def _attention(q, k, vt, bias, lam_p, subln_g, lambda_init):
    B, S, _ = q.shape
    tq = ATT_TILE
    blk = ATT_BLOCK
    nq = tq // blk
    nchain = 2 * nq
    nnear = nq * (nq + 1)
    H = N_HEADS
    rows = V_DIM + SUM_ROWS
    vmem = (4 * S * V_DIM * 2 + 2 * 2 * blk * blk * 4 + (3 * nchain + nnear) * blk * blk * 4
            + 2 * nchain * rows * blk * 4 + 4 * tq * V_DIM * 2)
    return pl.pallas_call(
        functools.partial(_attn_kernel, lambda_init),
        grid=(B, H, S // tq),
        in_specs=[
            pl.BlockSpec((1, tq, V_DIM), lambda b, h, i: (b, i, h)),
            pl.BlockSpec((1, S, V_DIM), lambda b, h, i: (b, 0, h)),
            pl.BlockSpec((1, V_DIM, S), lambda b, h, i: (b, h, 0)),
            pl.BlockSpec((1, 2, blk, blk), lambda b, h, i: (h, 0, 0, 0)),
            pl.BlockSpec((4, HEAD_DIM), lambda b, h, i: (0, 0)),
            pl.BlockSpec((V_DIM, blk), lambda b, h, i: (0, 0)),
        ],
        out_specs=pl.BlockSpec((1, V_DIM, tq), lambda b, h, i: (b, h, i)),
        out_shape=jax.ShapeDtypeStruct((B, D_ATTN, S), BF16),
        scratch_shapes=[
            pltpu.VMEM((nchain, blk, V_DIM), BF16),
            pltpu.VMEM((nchain, 1, blk), F32),
            pltpu.VMEM((nchain, rows, blk), F32),
            pltpu.VMEM((2, nchain, blk, blk), F32),
            pltpu.VMEM((nnear, blk, blk), F32),
        ],
        compiler_params=_params(("parallel", "parallel", "arbitrary"), vmem),
        name="diff_attn",
    )(q, k, vt, bias, lam_p, subln_g)


def _layer1_kernel(x_ref, a_ref, wo_ref, g0_ref, b0_ref, wg_ref, wu_ref, wd_ref, g1_ref, b1_ref,
                   o_ref, y_ref, xb_ref, acc_ref, h_ref):
    h = lax.dot_general(a_ref[0], wo_ref[...], (((0,), (0,)), ((), ())), preferred_element_type=F32)
    y_ref[...] = _layernorm(ALPHA * x_ref[0] + h, g0_ref[...], b0_ref[...])
    o_ref[0] = _ffn_ln(y_ref, wg_ref, wu_ref, wd_ref, g1_ref, b1_ref, xb_ref, acc_ref, h_ref)


def _layer1(x, at, wo, g0, b0, wg, wu, wd, g1, b1):
    B, S, D = x.shape
    tm = ROW_TILE
    vmem = _ffn_vmem(tm) + D * D_ATTN * 2 + 2 * tm * D_ATTN * 2 + 6 * tm * D * 4
    return pl.pallas_call(
        _layer1_kernel,
        grid=(B, S // tm),
        in_specs=[
            pl.BlockSpec((1, tm, D), lambda b, i: (b, i, 0)),
            pl.BlockSpec((1, D_ATTN, tm), lambda b, i: (b, 0, i)),
            _resident((D_ATTN, D)), _resident((1, D)), _resident((1, D)),
        ] + _ffn_specs(),
        out_specs=pl.BlockSpec((1, tm, D), lambda b, i: (b, i, 0)),
        out_shape=jax.ShapeDtypeStruct((B, S, D), F32),
        scratch_shapes=_ffn_scratch(tm),
        compiler_params=_params(("parallel", "parallel"), vmem),
        name="layer1_oproj_ffn",
    )(x, at, wo, g0, b0, wg, wu, wd, g1, b1)


def kernel(x, pool_w, pool_scale, w_qkv, w_o, lam_p, subln_g, rel_table, w_gate, w_up, w_down,
           ln_mix_g, ln_mix_b, ln_ffn_g, ln_ffn_b):
    B, S, D = x.shape
    M = B * S
    row = lambda a: a.reshape(1, -1).astype(F32)
    ffn_w = lambda i: (w_gate[i].astype(BF16), w_up[i].astype(BF16), w_down[i].astype(BF16),
                       row(ln_ffn_g[i]), row(ln_ffn_b[i]))

    x2 = _layer0(x, pool_w[0].astype(BF16), row(pool_scale[0]), row(ln_mix_g[0]), row(ln_mix_b[0]), *ffn_w(0))

    lambda_init = 0.8 - 0.6 * math.exp(-0.3 * 1)
    wqkv = w_qkv[0].astype(BF16)
    q, k, vt = _qkv(x2, wqkv[:, :D_ATTN], wqkv[:, D_ATTN:2 * D_ATTN], wqkv[:, 2 * D_ATTN:].T)
    gain = jnp.broadcast_to(subln_g[0].astype(F32)[:, None], (V_DIM, ATT_BLOCK))
    at = _attention(q, k, vt, _near_bias(rel_table), lam_p[0].astype(F32), gain, lambda_init)
    return _layer1(x2, at, w_o[0].astype(BF16), row(ln_mix_g[1]), row(ln_mix_b[1]), *ffn_w(1))
```

```python
import functools
import math

import jax
import jax.numpy as jnp
from jax import lax
from jax.experimental import pallas as pl
from jax.experimental.pallas import tpu as pltpu

D_MODEL = 1024
DEPTH = 2
ALPHA = (2.0 * DEPTH) ** 0.25
LN_EPS = 1e-5
POOL_WINDOWS = (2, 4, 8, 16)
N_GROUPS = len(POOL_WINDOWS)
GROUP_W = D_MODEL // N_GROUPS
POOL_HALO = 16
HEAD_DIM = 64
N_HEADS = D_MODEL // (2 * HEAD_DIM)
V_DIM = 2 * HEAD_DIM
D_ATTN = N_HEADS * V_DIM
NUM_BUCKETS = 32
MAX_EXACT = NUM_BUCKETS // 2
MAX_DISTANCE = 128
D_FF = 2816
SM_SCALE = HEAD_DIM ** -0.5
LOG2_E = math.log2(math.e)

FF_CHUNK = 256
ROW_TILE = 512
ATT_BLOCK = 256
ATT_TILE = 1024
NEAR_LOOKAHEAD = 4
SUM_ROWS = 16
MASK_VALUE = -1e30
V7X_VMEM_BYTES = 64 * 1024 * 1024

F32 = jnp.float32
BF16 = jnp.bfloat16
NT_DIMS = (((1,), (1,)), ((), ()))


def _params(semantics, vmem_bytes):
    limit = min(int(vmem_bytes * 1.25) + (4 << 20), V7X_VMEM_BYTES - (6 << 20))
    return pltpu.CompilerParams(dimension_semantics=semantics, vmem_limit_bytes=limit)


def _resident(shape):
    return pl.BlockSpec(shape, lambda *_: (0,) * len(shape), pipeline_mode=pl.Buffered(1))


def _layernorm(z, g, b):
    mu = jnp.mean(z, axis=-1, keepdims=True)
    zc = z - mu
    var = jnp.mean(zc * zc, axis=-1, keepdims=True)
    return zc * lax.rsqrt(var + LN_EPS) * g + b


def _ffn_ln(y_ref, wg_ref, wu_ref, wd_ref, g_ref, b_ref, xb_ref, acc_ref, h_ref):
    xb_ref[...] = y_ref[...].astype(BF16)
    nc = wg_ref.shape[1] // FF_CHUNK

    def hidden(c):
        cols = slice(c * FF_CHUNK, (c + 1) * FF_CHUNK)
        xb = xb_ref[...]
        gate = jnp.dot(xb, wg_ref[:, cols], preferred_element_type=F32)
        up = jnp.dot(xb, wu_ref[:, cols], preferred_element_type=F32)
        h_ref[c % 2] = (gate * jax.nn.sigmoid(gate) * up).astype(BF16)

    def down(c):
        rows = slice(c * FF_CHUNK, (c + 1) * FF_CHUNK)
        y = jnp.dot(h_ref[c % 2], wd_ref[rows, :], preferred_element_type=F32)
        if c == 0:
            acc_ref[...] = y
        else:
            acc_ref[...] += y

    hidden(0)
    for c in range(nc):
        if c + 1 < nc:
            hidden(c + 1)
        down(c)
    return _layernorm(ALPHA * y_ref[...] + acc_ref[...], g_ref[...], b_ref[...])


def _ffn_specs():
    return [_resident((D_MODEL, D_FF)), _resident((D_MODEL, D_FF)), _resident((D_FF, D_MODEL)),
            _resident((1, D_MODEL)), _resident((1, D_MODEL))]


def _ffn_scratch(tm):
    return [pltpu.VMEM((tm, D_MODEL), F32), pltpu.VMEM((tm, D_MODEL), BF16), pltpu.VMEM((tm, D_MODEL), F32),
            pltpu.VMEM((2, tm, FF_CHUNK), BF16)]


def _ffn_vmem(tm):
    return 3 * D_MODEL * D_FF * 2 + 4 * tm * D_MODEL * 4 + tm * D_MODEL * 10 + 8 * tm * FF_CHUNK * 4


def _layer0_kernel(x_ref, halo_ref, pw_ref, ps_ref, g0_ref, b0_ref, wg_ref, wu_ref, wd_ref, g1_ref, b1_ref,
                   o_ref, y_ref, xb_ref, acc_ref, h_ref):
    s = pl.program_id(1)
    ts = x_ref.shape[1]
    x = x_ref[0]
    halo = jnp.where(s > 0, halo_ref[0], 0.0)
    xh = jnp.concatenate([halo, x], axis=0)
    t = s * ts + lax.broadcasted_iota(jnp.int32, (ts, 1), 0)
    mixed = []
    for g, win in enumerate(POOL_WINDOWS):
        a = xh[:, g * GROUP_W:(g + 1) * GROUP_W]
        shift = 1
        while shift < win:
            a = a + pltpu.roll(a, shift, 0)
            shift *= 2
        cnt = jnp.minimum(t + 1, win).astype(F32)
        d = a[POOL_HALO:] / cnt - x[:, g * GROUP_W:(g + 1) * GROUP_W]
        mixed.append(jnp.dot(d.astype(BF16), pw_ref[g], preferred_element_type=F32))
    h = jnp.concatenate(mixed, axis=-1) * ps_ref[...]
    y_ref[...] = _layernorm(ALPHA * x + h, g0_ref[...], b0_ref[...])
    o_ref[0] = _ffn_ln(y_ref, wg_ref, wu_ref, wd_ref, g1_ref, b1_ref, xb_ref, acc_ref, h_ref)


def _layer0(x, pw, ps, g0, b0, wg, wu, wd, g1, b1):
    B, S, D = x.shape
    ts = ROW_TILE
    halo_blocks = ts // POOL_HALO
    vmem = _ffn_vmem(ts) + 10 * ts * D * 4
    return pl.pallas_call(
        _layer0_kernel,
        grid=(B, S // ts),
        in_specs=[
            pl.BlockSpec((1, ts, D), lambda b_, s_: (b_, s_, 0)),
            pl.BlockSpec((1, POOL_HALO, D), lambda b_, s_: (b_, jnp.maximum(s_ * halo_blocks - 1, 0), 0)),
            _resident((N_GROUPS, GROUP_W, GROUP_W)), _resident((1, D)), _resident((1, D)), _resident((1, D)),
        ] + _ffn_specs(),
        out_specs=pl.BlockSpec((1, ts, D), lambda b_, s_: (b_, s_, 0)),
        out_shape=jax.ShapeDtypeStruct((B, S, D), F32),
        scratch_shapes=_ffn_scratch(ts),
        compiler_params=_params(("parallel", "parallel"), vmem),
        name="layer0_pool_ffn",
    )(x, x, pw, ps, g0, b0, wg, wu, wd, g1, b1)


def _qkv_kernel(x_ref, wq_ref, wk_ref, wvt_ref, q_ref, k_ref, vt_ref):
    xb = x_ref[0].astype(BF16)
    q = jnp.dot(xb, wq_ref[...], preferred_element_type=F32) * (SM_SCALE * LOG2_E)
    q_ref[0] = q.astype(BF16)
    k_ref[0] = jnp.dot(xb, wk_ref[...], preferred_element_type=F32).astype(BF16)
    vt = lax.dot_general(wvt_ref[...], xb, NT_DIMS, preferred_element_type=F32)
    vt_ref[0] = vt.astype(BF16)


def _qkv(x, wq, wk, wvt):
    B, S, D = x.shape
    tm = ROW_TILE
    vmem = 3 * D * D_ATTN * 2 + 2 * tm * D * 4 + 6 * tm * D_ATTN * 2 + tm * D * 2 + 3 * tm * D_ATTN * 4
    return pl.pallas_call(
        _qkv_kernel,
        grid=(B, S // tm),
        in_specs=[
            pl.BlockSpec((1, tm, D), lambda b, i: (b, i, 0)),
            _resident((D, D_ATTN)), _resident((D, D_ATTN)), _resident((D_ATTN, D)),
        ],
        out_specs=[
            pl.BlockSpec((1, tm, D_ATTN), lambda b, i: (b, i, 0)),
            pl.BlockSpec((1, tm, D_ATTN), lambda b, i: (b, i, 0)),
            pl.BlockSpec((1, D_ATTN, tm), lambda b, i: (b, 0, i)),
        ],
        out_shape=[
            jax.ShapeDtypeStruct((B, S, D_ATTN), BF16),
            jax.ShapeDtypeStruct((B, S, D_ATTN), BF16),
            jax.ShapeDtypeStruct((B, D_ATTN, S), BF16),
        ],
        compiler_params=_params(("parallel", "parallel"), vmem),
        name="qkv_proj",
    )(x, wq, wk, wvt)


def _attn_kernel(lambda_init, q_ref, k_ref, vt_ref, bias_ref, lam_ref, sg_ref, o_ref,
                 qs_ref, m_ref, acc_ref, s0_ref, s1_ref, ns_ref):
    i = pl.program_id(2)
    blk = ATT_BLOCK
    nq = q_ref.shape[1] // blk
    assert nq % 2 == 0, "the key-block pipeline below pairs blocks and needs an even count per tile"
    diag_bias = bias_ref[0, 1]
    prev_bias = bias_ref[0, 0]

    lane = lax.broadcasted_iota(jnp.int32, (blk, V_DIM), 1)
    for hq in range(nq):
        q = q_ref[0, hq * blk:(hq + 1) * blk, :]
        zero = jnp.zeros_like(q)
        qs_ref[2 * hq] = jnp.where(lane < HEAD_DIM, q, zero)
        qs_ref[2 * hq + 1] = jnp.where(lane >= HEAD_DIM, q, zero)

    def scores(kc, cb):
        start = pl.multiple_of(kc * blk, blk)
        k = k_ref[0, pl.ds(start, blk), :]
        return lax.dot_general(k, qs_ref[cb], NT_DIMS, preferred_element_type=F32)

    def values(kc):
        start = pl.multiple_of(kc * blk, blk)
        vt = vt_ref[0, :, pl.ds(start, blk)]
        return jnp.concatenate([vt, jnp.ones((SUM_ROWS, blk), BF16)], axis=0)

    def update(cb, s, vt, bias, first):
        if bias is not None:
            s = s + bias
        m_new = jnp.max(s, axis=0, keepdims=True)
        if not first:
            m_prev = m_ref[cb]
            m_new = jnp.maximum(m_prev, m_new)
        p = jnp.exp2(s - m_new)
        pv = jnp.dot(vt, p.astype(BF16), preferred_element_type=F32)
        if first:
            acc_ref[cb] = pv
        else:
            acc_ref[cb] = acc_ref[cb] * jnp.exp2(m_prev - m_new) + pv
        m_ref[cb] = m_new

    base = nq * i
    near = [(d, hq, c) for d in range(nq - 1, -1, -1) for hq in range(d, nq) for c in range(2)]

    s_refs = (s0_ref, s1_ref)

    def near_phase(extra_scores):
        def score_job(j, d, hq, c):
            def run():
                ns_ref[j] = scores(base + d, 2 * hq + c)
            return run

        jobs = [score_job(j, d, hq, c) for j, (d, hq, c) in enumerate(near)] + extra_scores
        ahead = max(NEAR_LOOKAHEAD, len(jobs) - len(near))
        for job in jobs[:ahead]:
            job()
        vts = {}
        for j, (d, hq, c) in enumerate(near):
            if j + ahead < len(jobs):
                jobs[j + ahead]()
            if d not in vts:
                vts[d] = values(base + d)
            bias = diag_bias if hq == d else prev_bias if hq == d + 1 else None
            update(2 * hq + c, ns_ref[j], vts[d], bias, hq == d)

    def score_into(kc, slot, cb):
        def run():
            s_refs[slot][cb] = scores(kc, cb)
        return run

    def issue_consume(kc, slot, issue_next=True, last=False):
        vt = values(kc)
        for cb in range(2 * nq):
            if issue_next:
                score_into(kc + 1, 1 - slot, cb)()
            bias = prev_bias if (last and cb < 2) else None
            update(cb, s_refs[slot][cb], vt, bias, False)

    @pl.when(i == 0)
    def _():
        near_phase([])

    @pl.when(i > 0)
    def _():
        near_phase([score_into(0, 0, cb) for cb in range(2 * nq)])

        def pair(t, carry):
            issue_consume(2 * t, 0)
            issue_consume(2 * t + 1, 1)
            return carry

        lax.fori_loop(0, (nq // 2) * i - 1, pair, 0)
        issue_consume(base - 2, 0)
        issue_consume(base - 1, 1, issue_next=False, last=True)

    lp = lam_ref[...]
    lam = (jnp.exp(jnp.sum(lp[0:1] * lp[1:2], axis=-1, keepdims=True))
           - jnp.exp(jnp.sum(lp[2:3] * lp[3:4], axis=-1, keepdims=True)) + lambda_init)
    gain = sg_ref[...] * (1.0 - lambda_init)
    for hq in range(nq):
        a0 = acc_ref[2 * hq]
        a1 = acc_ref[2 * hq + 1]
        ot = a0[:V_DIM] / a0[V_DIM:V_DIM + 1] - lam * (a1[:V_DIM] / a1[V_DIM:V_DIM + 1])
        ot = ot * lax.rsqrt(jnp.mean(ot * ot, axis=0, keepdims=True) + LN_EPS)
        o_ref[0, :, hq * blk:(hq + 1) * blk] = (ot * gain).astype(BF16)


def _rel_bucket(rel):
    n = jnp.maximum(rel, 0)
    nf = jnp.maximum(n, 1).astype(F32)
    large = MAX_EXACT + (jnp.log(nf / MAX_EXACT) / math.log(MAX_DISTANCE / MAX_EXACT)
                         * (NUM_BUCKETS - MAX_EXACT)).astype(jnp.int32)
    return jnp.where(n < MAX_EXACT, n, jnp.minimum(large, NUM_BUCKETS - 1))


def _near_bias(rel_table):
    blk = ATT_BLOCK
    key = jnp.arange(blk)[:, None]
    query = jnp.arange(blk)[None, :]
    rel = jnp.stack([query - key + blk, query - key])
    bucket = _rel_bucket(rel)
    table = rel_table.astype(F32)
    table = (table - table[NUM_BUCKETS - 1]) * LOG2_E
    bias = jnp.zeros((N_HEADS,) + rel.shape, F32)
    for b in range(NUM_BUCKETS):
        bias = jnp.where(bucket[None] == b, table[b][:, None, None, None], bias)
    return jnp.where(rel[None] >= 0, bias, MASK_VALUE)


def _attention(q, k, vt, bias, lam_p, subln_g, lambda_init):
    B, S, _ = q.shape
    tq = ATT_TILE
    blk = ATT_BLOCK
    nq = tq // blk
    nchain = 2 * nq
    nnear = nq * (nq + 1)
    H = N_HEADS
    rows = V_DIM + SUM_ROWS
    vmem = (4 * S * V_DIM * 2 + 2 * 2 * blk * blk * 4 + (3 * nchain + nnear) * blk * blk * 4
            + 2 * nchain * rows * blk * 4 + 4 * tq * V_DIM * 2)
    return pl.pallas_call(
        functools.partial(_attn_kernel, lambda_init),
        grid=(B, H, S // tq),
        in_specs=[
            pl.BlockSpec((1, tq, V_DIM), lambda b, h, i: (b, i, h)),
            pl.BlockSpec((1, S, V_DIM), lambda b, h, i: (b, 0, h)),
            pl.BlockSpec((1, V_DIM, S), lambda b, h, i: (b, h, 0)),
            pl.BlockSpec((1, 2, blk, blk), lambda b, h, i: (h, 0, 0, 0)),
            pl.BlockSpec((4, HEAD_DIM), lambda b, h, i: (0, 0)),
            pl.BlockSpec((V_DIM, blk), lambda b, h, i: (0, 0)),
        ],
        out_specs=pl.BlockSpec((1, V_DIM, tq), lambda b, h, i: (b, h, i)),
        out_shape=jax.ShapeDtypeStruct((B, D_ATTN, S), BF16),
        scratch_shapes=[
            pltpu.VMEM((nchain, blk, V_DIM), BF16),
            pltpu.VMEM((nchain, 1, blk), F32),
            pltpu.VMEM((nchain, rows, blk), F32),
            pltpu.VMEM((nchain, blk, blk), F32),
            pltpu.VMEM((nchain, blk, blk), F32),
            pltpu.VMEM((nnear, blk, blk), F32),
        ],
        compiler_params=_params(("parallel", "parallel", "arbitrary"), vmem),
        name="diff_attn",
    )(q, k, vt, bias, lam_p, subln_g)


def _layer1_kernel(x_ref, a_ref, wo_ref, g0_ref, b0_ref, wg_ref, wu_ref, wd_ref, g1_ref, b1_ref,
                   o_ref, y_ref, xb_ref, acc_ref, h_ref):
    h = lax.dot_general(a_ref[0], wo_ref[...], (((0,), (0,)), ((), ())), preferred_element_type=F32)
    y_ref[...] = _layernorm(ALPHA * x_ref[0] + h, g0_ref[...], b0_ref[...])
    o_ref[0] = _ffn_ln(y_ref, wg_ref, wu_ref, wd_ref, g1_ref, b1_ref, xb_ref, acc_ref, h_ref)


def _layer1(x, at, wo, g0, b0, wg, wu, wd, g1, b1):
    B, S, D = x.shape
    tm = ROW_TILE
    vmem = _ffn_vmem(tm) + D * D_ATTN * 2 + 2 * tm * D_ATTN * 2 + 6 * tm * D * 4
    return pl.pallas_call(
        _layer1_kernel,
        grid=(B, S // tm),
        in_specs=[
            pl.BlockSpec((1, tm, D), lambda b, i: (b, i, 0)),
            pl.BlockSpec((1, D_ATTN, tm), lambda b, i: (b, 0, i)),
            _resident((D_ATTN, D)), _resident((1, D)), _resident((1, D)),
        ] + _ffn_specs(),
        out_specs=pl.BlockSpec((1, tm, D), lambda b, i: (b, i, 0)),
        out_shape=jax.ShapeDtypeStruct((B, S, D), F32),
        scratch_shapes=_ffn_scratch(tm),
        compiler_params=_params(("parallel", "parallel"), vmem),
        name="layer1_oproj_ffn",
    )(x, at, wo, g0, b0, wg, wu, wd, g1, b1)


def kernel(x, pool_w, pool_scale, w_qkv, w_o, lam_p, subln_g, rel_table, w_gate, w_up, w_down,
           ln_mix_g, ln_mix_b, ln_ffn_g, ln_ffn_b):
    B, S, D = x.shape
    M = B * S
    row = lambda a: a.reshape(1, -1).astype(F32)
    ffn_w = lambda i: (w_gate[i].astype(BF16), w_up[i].astype(BF16), w_down[i].astype(BF16),
                       row(ln_ffn_g[i]), row(ln_ffn_b[i]))

    x2 = _layer0(x, pool_w[0].astype(BF16), row(pool_scale[0]), row(ln_mix_g[0]), row(ln_mix_b[0]), *ffn_w(0))

    lambda_init = 0.8 - 0.6 * math.exp(-0.3 * 1)
    wqkv = w_qkv[0].astype(BF16)
    q, k, vt = _qkv(x2, wqkv[:, :D_ATTN], wqkv[:, D_ATTN:2 * D_ATTN], wqkv[:, 2 * D_ATTN:].T)
    gain = jnp.broadcast_to(subln_g[0].astype(F32)[:, None], (V_DIM, ATT_BLOCK))
    at = _attention(q, k, vt, _near_bias(rel_table), lam_p[0].astype(F32), gain, lambda_init)
    return _layer1(x2, at, w_o[0].astype(BF16), row(ln_mix_g[1]), row(ln_mix_b[1]), *ffn_w(1))
```

```python
import functools
import math

import jax
import jax.numpy as jnp
from jax import lax
from jax.experimental import pallas as pl
from jax.experimental.pallas import tpu as pltpu

D_MODEL = 1024
DEPTH = 2
ALPHA = (2.0 * DEPTH) ** 0.25
LN_EPS = 1e-5
POOL_WINDOWS = (2, 4, 8, 16)
N_GROUPS = len(POOL_WINDOWS)
GROUP_W = D_MODEL // N_GROUPS
POOL_HALO = 16
HEAD_DIM = 64
N_HEADS = D_MODEL // (2 * HEAD_DIM)
V_DIM = 2 * HEAD_DIM
D_ATTN = N_HEADS * V_DIM
NUM_BUCKETS = 32
MAX_EXACT = NUM_BUCKETS // 2
MAX_DISTANCE = 128
D_FF = 2816
SM_SCALE = HEAD_DIM ** -0.5
LOG2_E = math.log2(math.e)

FF_CHUNK = 256
ROW_TILE = 512
HALVES = 2
OVERLAP_AFTER_CHUNK = 1
ATT_BLOCK = 256
ATT_TILE = 1024
NEAR_LOOKAHEAD = 4
SUM_ROWS = 16
MASK_VALUE = -1e30
V7X_VMEM_BYTES = 64 * 1024 * 1024

F32 = jnp.float32
BF16 = jnp.bfloat16
NT_DIMS = (((1,), (1,)), ((), ()))


def _params(semantics, vmem_bytes):
    limit = min(int(vmem_bytes * 1.25) + (4 << 20), V7X_VMEM_BYTES - (6 << 20))
    return pltpu.CompilerParams(dimension_semantics=semantics, vmem_limit_bytes=limit)


def _resident(shape):
    return pl.BlockSpec(shape, lambda *_: (0,) * len(shape), pipeline_mode=pl.Buffered(1))


def _layernorm(z, g, b):
    mu = jnp.mean(z, axis=-1, keepdims=True)
    zc = z - mu
    var = jnp.mean(zc * zc, axis=-1, keepdims=True)
    return zc * lax.rsqrt(var + LN_EPS) * g + b


def _ffn_ln(y_ref, wg_ref, wu_ref, wd_ref, g_ref, b_ref, xb_ref, acc_ref, h_ref, overlap=None):
    xb_ref[...] = y_ref[...].astype(BF16)
    nc = wg_ref.shape[1] // FF_CHUNK

    def hidden(c):
        cols = slice(c * FF_CHUNK, (c + 1) * FF_CHUNK)
        xb = xb_ref[...]
        gate = jnp.dot(xb, wg_ref[:, cols], preferred_element_type=F32)
        up = jnp.dot(xb, wu_ref[:, cols], preferred_element_type=F32)
        h_ref[c % 2] = (gate * jax.nn.sigmoid(gate) * up).astype(BF16)

    def down(c):
        rows = slice(c * FF_CHUNK, (c + 1) * FF_CHUNK)
        y = jnp.dot(h_ref[c % 2], wd_ref[rows, :], preferred_element_type=F32)
        if c == 0:
            acc_ref[...] = y
        else:
            acc_ref[...] += y

    hidden(0)
    for c in range(nc):
        if c + 1 < nc:
            hidden(c + 1)
        down(c)
        if overlap is not None and c == OVERLAP_AFTER_CHUNK:
            overlap()
    return _layernorm(ALPHA * y_ref[...] + acc_ref[...], g_ref[...], b_ref[...])


def _two_halves(mixer, o_ref, ffn_refs, scratch):
    y_ref, xb_ref, acc_ref, h_ref = scratch
    tm = y_ref.shape[1]
    mixer(0, y_ref.at[0])
    for h in range(HALVES):
        nxt = functools.partial(mixer, h + 1, y_ref.at[h + 1]) if h + 1 < HALVES else None
        o_ref[0, h * tm:(h + 1) * tm, :] = _ffn_ln(y_ref.at[h], *ffn_refs, xb_ref.at[h], acc_ref.at[h],
                                                   h_ref.at[h], overlap=nxt)


def _ffn_specs():
    return [_resident((D_MODEL, D_FF)), _resident((D_MODEL, D_FF)), _resident((D_FF, D_MODEL)),
            _resident((1, D_MODEL)), _resident((1, D_MODEL))]


def _ffn_scratch(tm):
    return [pltpu.VMEM((HALVES, tm, D_MODEL), F32), pltpu.VMEM((HALVES, tm, D_MODEL), BF16),
            pltpu.VMEM((HALVES, tm, D_MODEL), F32), pltpu.VMEM((HALVES, 2, tm, FF_CHUNK), BF16)]


def _ffn_vmem(tm):
    rows = HALVES * tm
    return 3 * D_MODEL * D_FF * 2 + 4 * rows * D_MODEL * 4 + rows * D_MODEL * 10 + 8 * tm * FF_CHUNK * 4


def _layer0_kernel(x_ref, halo_ref, pw_ref, ps_ref, g0_ref, b0_ref, wg_ref, wu_ref, wd_ref, g1_ref, b1_ref,
                   o_ref, y_ref, xb_ref, acc_ref, h_ref):
    s = pl.program_id(1)
    ts = x_ref.shape[1] // HALVES

    def mixer(hf, y_out):
        x = x_ref[0, hf * ts:(hf + 1) * ts, :]
        if hf == 0:
            halo = jnp.where(s > 0, halo_ref[0], 0.0)
        else:
            halo = x_ref[0, hf * ts - POOL_HALO:hf * ts, :]
        xh = jnp.concatenate([halo, x], axis=0)
        t = (s * HALVES + hf) * ts + lax.broadcasted_iota(jnp.int32, (ts, 1), 0)
        mixed = []
        for g, win in enumerate(POOL_WINDOWS):
            a = xh[:, g * GROUP_W:(g + 1) * GROUP_W]
            shift = 1
            while shift < win:
                a = a + pltpu.roll(a, shift, 0)
                shift *= 2
            inv_cnt = 1.0 / jnp.minimum(t + 1, win).astype(F32)
            d = a[POOL_HALO:] * inv_cnt - x[:, g * GROUP_W:(g + 1) * GROUP_W]
            mixed.append(jnp.dot(d.astype(BF16), pw_ref[g], preferred_element_type=F32))
        h = jnp.concatenate(mixed, axis=-1) * ps_ref[...]
        y_out[...] = _layernorm(ALPHA * x + h, g0_ref[...], b0_ref[...])

    _two_halves(mixer, o_ref, (wg_ref, wu_ref, wd_ref, g1_ref, b1_ref), (y_ref, xb_ref, acc_ref, h_ref))


def _layer0(x, pw, ps, g0, b0, wg, wu, wd, g1, b1):
    B, S, D = x.shape
    ts = HALVES * ROW_TILE
    halo_blocks = ts // POOL_HALO
    vmem = _ffn_vmem(ROW_TILE) + 10 * ROW_TILE * D * 4
    return pl.pallas_call(
        _layer0_kernel,
        grid=(B, S // ts),
        in_specs=[
            pl.BlockSpec((1, ts, D), lambda b_, s_: (b_, s_, 0)),
            pl.BlockSpec((1, POOL_HALO, D), lambda b_, s_: (b_, jnp.maximum(s_ * halo_blocks - 1, 0), 0)),
            _resident((N_GROUPS, GROUP_W, GROUP_W)), _resident((1, D)), _resident((1, D)), _resident((1, D)),
        ] + _ffn_specs(),
        out_specs=pl.BlockSpec((1, ts, D), lambda b_, s_: (b_, s_, 0)),
        out_shape=jax.ShapeDtypeStruct((B, S, D), F32),
        scratch_shapes=_ffn_scratch(ROW_TILE),
        compiler_params=_params(("parallel", "parallel"), vmem),
        name="layer0_pool_ffn",
    )(x, x, pw, ps, g0, b0, wg, wu, wd, g1, b1)


def _qkv_kernel(x_ref, wqk_ref, wvt_ref, q_ref, k_ref, vt_ref):
    xb = x_ref[0].astype(BF16)
    q = jnp.dot(xb, wqk_ref[:, :D_ATTN], preferred_element_type=F32) * (SM_SCALE * LOG2_E)
    q_ref[0] = q.astype(BF16)
    k_ref[0] = jnp.dot(xb, wqk_ref[:, D_ATTN:], preferred_element_type=F32).astype(BF16)
    vt = lax.dot_general(wvt_ref[...], xb, NT_DIMS, preferred_element_type=F32)
    vt_ref[0] = vt.astype(BF16)


def _qkv(x, wqk, wvt):
    B, S, D = x.shape
    tm = ROW_TILE
    vmem = 3 * D * D_ATTN * 2 + 2 * tm * D * 4 + 6 * tm * D_ATTN * 2 + tm * D * 2 + 3 * tm * D_ATTN * 4
    return pl.pallas_call(
        _qkv_kernel,
        grid=(B, S // tm),
        in_specs=[
            pl.BlockSpec((1, tm, D), lambda b, i: (b, i, 0)),
            _resident((D, 2 * D_ATTN)), _resident((D_ATTN, D)),
        ],
        out_specs=[
            pl.BlockSpec((1, tm, D_ATTN), lambda b, i: (b, i, 0)),
            pl.BlockSpec((1, tm, D_ATTN), lambda b, i: (b, i, 0)),
            pl.BlockSpec((1, D_ATTN, tm), lambda b, i: (b, 0, i)),
        ],
        out_shape=[
            jax.ShapeDtypeStruct((B, S, D_ATTN), BF16),
            jax.ShapeDtypeStruct((B, S, D_ATTN), BF16),
            jax.ShapeDtypeStruct((B, D_ATTN, S), BF16),
        ],
        compiler_params=_params(("parallel", "parallel"), vmem),
        name="qkv_proj",
    )(x, wqk, wvt)


def _attn_kernel(lambda_init, q_ref, k_ref, vt_ref, bias_ref, lam_ref, sg_ref, o_ref,
                 qs_ref, m_ref, acc_ref, s0_ref, s1_ref, ns_ref):
    i = pl.program_id(2)
    blk = ATT_BLOCK
    nq = q_ref.shape[1] // blk
    assert nq % 2 == 0, "the key-block pipeline below pairs blocks and needs an even count per tile"
    diag_bias = bias_ref[0, 1]
    prev_bias = bias_ref[0, 0]

    lane = lax.broadcasted_iota(jnp.int32, (blk, V_DIM), 1)
    for hq in range(nq):
        q = q_ref[0, hq * blk:(hq + 1) * blk, :]
        zero = jnp.zeros_like(q)
        qs_ref[2 * hq] = jnp.where(lane < HEAD_DIM, q, zero)
        qs_ref[2 * hq + 1] = jnp.where(lane >= HEAD_DIM, q, zero)

    def scores(kc, cb):
        start = pl.multiple_of(kc * blk, blk)
        k = k_ref[0, pl.ds(start, blk), :]
        return lax.dot_general(k, qs_ref[cb], NT_DIMS, preferred_element_type=F32)

    def values(kc):
        start = pl.multiple_of(kc * blk, blk)
        vt = vt_ref[0, :, pl.ds(start, blk)]
        return jnp.concatenate([vt, jnp.ones((SUM_ROWS, blk), BF16)], axis=0)

    def update(cb, s, vt, bias, first):
        if bias is not None:
            s = s + bias
        m_new = jnp.max(s, axis=0, keepdims=True)
        if not first:
            m_prev = m_ref[cb]
            m_new = jnp.maximum(m_prev, m_new)
        p = jnp.exp2(s - m_new)
        pv = jnp.dot(vt, p.astype(BF16), preferred_element_type=F32)
        if first:
            acc_ref[cb] = pv
        else:
            acc_ref[cb] = acc_ref[cb] * jnp.exp2(m_prev - m_new) + pv
        m_ref[cb] = m_new

    base = nq * i
    near = [(d, hq, c) for d in range(nq - 1, -1, -1) for hq in range(d, nq) for c in range(2)]

    s_refs = (s0_ref, s1_ref)

    def near_phase(extra_scores):
        def score_job(j, d, hq, c):
            def run():
                ns_ref[j] = scores(base + d, 2 * hq + c)
            return run

        jobs = [score_job(j, d, hq, c) for j, (d, hq, c) in enumerate(near)] + extra_scores
        ahead = max(NEAR_LOOKAHEAD, len(jobs) - len(near))
        for job in jobs[:ahead]:
            job()
        vts = {}
        for j, (d, hq, c) in enumerate(near):
            if j + ahead < len(jobs):
                jobs[j + ahead]()
            if d not in vts:
                vts[d] = values(base + d)
            bias = diag_bias if hq == d else prev_bias if hq == d + 1 else None
            update(2 * hq + c, ns_ref[j], vts[d], bias, hq == d)

    def score_into(kc, slot, cb):
        def run():
            s_refs[slot][cb] = scores(kc, cb)
        return run

    def issue_consume(kc, slot, issue_next=True, last=False):
        vt = values(kc)
        for cb in range(2 * nq):
            if issue_next:
                score_into(kc + 1, 1 - slot, cb)()
            bias = prev_bias if (last and cb < 2) else None
            update(cb, s_refs[slot][cb], vt, bias, False)

    @pl.when(i == 0)
    def _():
        near_phase([])

    @pl.when(i > 0)
    def _():
        near_phase([score_into(0, 0, cb) for cb in range(2 * nq)])

        def pair(t, carry):
            issue_consume(2 * t, 0)
            issue_consume(2 * t + 1, 1)
            return carry

        lax.fori_loop(0, (nq // 2) * i - 1, pair, 0)
        issue_consume(base - 2, 0)
        issue_consume(base - 1, 1, issue_next=False, last=True)

    lp = lam_ref[...]
    lam = (jnp.exp(jnp.sum(lp[0:1] * lp[1:2], axis=-1, keepdims=True))
           - jnp.exp(jnp.sum(lp[2:3] * lp[3:4], axis=-1, keepdims=True)) + lambda_init)
    gain = sg_ref[...] * (1.0 - lambda_init)
    for hq in range(nq):
        a0 = acc_ref[2 * hq]
        a1 = acc_ref[2 * hq + 1]
        ot = a0[:V_DIM] / a0[V_DIM:V_DIM + 1] - lam * (a1[:V_DIM] / a1[V_DIM:V_DIM + 1])
        ot = ot * lax.rsqrt(jnp.mean(ot * ot, axis=0, keepdims=True) + LN_EPS)
        o_ref[0, :, hq * blk:(hq + 1) * blk] = (ot * gain).astype(BF16)


def _rel_bucket(rel):
    n = jnp.maximum(rel, 0)
    nf = jnp.maximum(n, 1).astype(F32)
    large = MAX_EXACT + jnp.floor(jnp.log(nf / MAX_EXACT) / math.log(MAX_DISTANCE / MAX_EXACT)
                                  * (NUM_BUCKETS - MAX_EXACT)).astype(jnp.int32)
    return jnp.where(n < MAX_EXACT, n, jnp.minimum(large, NUM_BUCKETS - 1))


def _near_bias(rel_table):
    blk = ATT_BLOCK
    key = jnp.arange(blk)[:, None]
    query = jnp.arange(blk)[None, :]
    rel = jnp.stack([query - key + blk, query - key])
    bucket = _rel_bucket(rel)
    table = rel_table.astype(F32)
    table = (table - table[NUM_BUCKETS - 1]) * LOG2_E
    bias = jnp.zeros((N_HEADS,) + rel.shape, F32)
    for b in range(NUM_BUCKETS):
        bias = jnp.where(bucket[None] == b, table[b][:, None, None, None], bias)
    return jnp.where(rel[None] >= 0, bias, MASK_VALUE)


def _attention(q, k, vt, bias, lam_p, subln_g, lambda_init):
    B, S, _ = q.shape
    tq = ATT_TILE
    blk = ATT_BLOCK
    nq = tq // blk
    nchain = 2 * nq
    nnear = nq * (nq + 1)
    H = N_HEADS
    rows = V_DIM + SUM_ROWS
    vmem = (4 * S * V_DIM * 2 + 2 * 2 * blk * blk * 4 + (3 * nchain + nnear) * blk * blk * 4
            + 2 * nchain * rows * blk * 4 + 4 * tq * V_DIM * 2)
    return pl.pallas_call(
        functools.partial(_attn_kernel, lambda_init),
        grid=(B, H, S // tq),
        in_specs=[
            pl.BlockSpec((1, tq, V_DIM), lambda b, h, i: (b, i, h)),
            pl.BlockSpec((1, S, V_DIM), lambda b, h, i: (b, 0, h)),
            pl.BlockSpec((1, V_DIM, S), lambda b, h, i: (b, h, 0)),
            pl.BlockSpec((1, 2, blk, blk), lambda b, h, i: (h, 0, 0, 0)),
            pl.BlockSpec((4, HEAD_DIM), lambda b, h, i: (0, 0)),
            pl.BlockSpec((V_DIM, blk), lambda b, h, i: (0, 0)),
        ],
        out_specs=pl.BlockSpec((1, V_DIM, tq), lambda b, h, i: (b, h, i)),
        out_shape=jax.ShapeDtypeStruct((B, D_ATTN, S), BF16),
        scratch_shapes=[
            pltpu.VMEM((nchain, blk, V_DIM), BF16),
            pltpu.VMEM((nchain, 1, blk), F32),
            pltpu.VMEM((nchain, rows, blk), F32),
            pltpu.VMEM((nchain, blk, blk), F32),
            pltpu.VMEM((nchain, blk, blk), F32),
            pltpu.VMEM((nnear, blk, blk), F32),
        ],
        compiler_params=_params(("parallel", "parallel", "arbitrary"), vmem),
        name="diff_attn",
    )(q, k, vt, bias, lam_p, subln_g)


def _layer1_kernel(x_ref, a_ref, wo_ref, g0_ref, b0_ref, wg_ref, wu_ref, wd_ref, g1_ref, b1_ref,
                   o_ref, y_ref, xb_ref, acc_ref, h_ref):
    tm = x_ref.shape[1] // HALVES

    def mixer(hf, y_out):
        a_t = a_ref[0, :, hf * tm:(hf + 1) * tm]
        h = lax.dot_general(a_t, wo_ref[...], (((0,), (0,)), ((), ())), preferred_element_type=F32)
        y_out[...] = _layernorm(ALPHA * x_ref[0, hf * tm:(hf + 1) * tm, :] + h, g0_ref[...], b0_ref[...])

    _two_halves(mixer, o_ref, (wg_ref, wu_ref, wd_ref, g1_ref, b1_ref), (y_ref, xb_ref, acc_ref, h_ref))


def _layer1(x, at, wo, g0, b0, wg, wu, wd, g1, b1):
    B, S, D = x.shape
    tm = HALVES * ROW_TILE
    vmem = _ffn_vmem(ROW_TILE) + D * D_ATTN * 2 + 2 * tm * D_ATTN * 2 + 6 * ROW_TILE * D * 4
    return pl.pallas_call(
        _layer1_kernel,
        grid=(B, S // tm),
        in_specs=[
            pl.BlockSpec((1, tm, D), lambda b, i: (b, i, 0)),
            pl.BlockSpec((1, D_ATTN, tm), lambda b, i: (b, 0, i)),
            _resident((D_ATTN, D)), _resident((1, D)), _resident((1, D)),
        ] + _ffn_specs(),
        out_specs=pl.BlockSpec((1, tm, D), lambda b, i: (b, i, 0)),
        out_shape=jax.ShapeDtypeStruct((B, S, D), F32),
        scratch_shapes=_ffn_scratch(ROW_TILE),
        compiler_params=_params(("parallel", "parallel"), vmem),
        name="layer1_oproj_ffn",
    )(x, at, wo, g0, b0, wg, wu, wd, g1, b1)


def kernel(x, pool_w, pool_scale, w_qkv, w_o, lam_p, subln_g, rel_table, w_gate, w_up, w_down,
           ln_mix_g, ln_mix_b, ln_ffn_g, ln_ffn_b):
    B, S, D = x.shape
    M = B * S
    row = lambda a: a.reshape(1, -1).astype(F32)
    ffn_w = lambda i: (w_gate[i].astype(BF16), w_up[i].astype(BF16), w_down[i].astype(BF16),
                       row(ln_ffn_g[i]), row(ln_ffn_b[i]))

    x2 = _layer0(x, pool_w[0].astype(BF16), row(pool_scale[0]), row(ln_mix_g[0]), row(ln_mix_b[0]), *ffn_w(0))

    lambda_init = 0.8 - 0.6 * math.exp(-0.3 * 1)
    q, k, vt = _qkv(x2, w_qkv[0, :, :2 * D_ATTN].astype(BF16), w_qkv[0, :, 2 * D_ATTN:].T.astype(BF16))
    gain = jnp.broadcast_to(subln_g[0].astype(F32)[:, None], (V_DIM, ATT_BLOCK))
    at = _attention(q, k, vt, _near_bias(rel_table), lam_p[0].astype(F32), gain, lambda_init)
    return _layer1(x2, at, w_o[0].astype(BF16), row(ln_mix_g[1]), row(ln_mix_b[1]), *ffn_w(1))
```

```python
import functools
import math

import jax
import jax.numpy as jnp
from jax import lax
from jax.experimental import pallas as pl
from jax.experimental.pallas import tpu as pltpu

D_MODEL = 1024
DEPTH = 2
ALPHA = (2.0 * DEPTH) ** 0.25
LN_EPS = 1e-5
POOL_WINDOWS = (2, 4, 8, 16)
N_GROUPS = len(POOL_WINDOWS)
GROUP_W = D_MODEL // N_GROUPS
POOL_HALO = 16
HEAD_DIM = 64
N_HEADS = D_MODEL // (2 * HEAD_DIM)
V_DIM = 2 * HEAD_DIM
D_ATTN = N_HEADS * V_DIM
NUM_BUCKETS = 32
MAX_EXACT = NUM_BUCKETS // 2
MAX_DISTANCE = 128
D_FF = 2816
SM_SCALE = HEAD_DIM ** -0.5
LOG2_E = math.log2(math.e)

FF_CHUNK = 256
ROW_TILE = 512
HALVES = 2
OVERLAP_AFTER_CHUNK = 1
ATT_BLOCK = 256
ATT_TILE = 1024
NEAR_LOOKAHEAD = 4
SUM_ROWS = 16
MASK_VALUE = -1e30
V7X_VMEM_BYTES = 64 * 1024 * 1024
VMEM_RESERVED_BYTES = 6 * 1024 * 1024
VMEM_TEMP_FACTOR = 1.5

F32 = jnp.float32
BF16 = jnp.bfloat16
NT_DIMS = (((1,), (1,)), ((), ()))


def _params(semantics, vmem_bytes):
    limit = min(int(vmem_bytes * VMEM_TEMP_FACTOR), V7X_VMEM_BYTES - VMEM_RESERVED_BYTES)
    return pltpu.CompilerParams(dimension_semantics=semantics, vmem_limit_bytes=limit)


def _resident(shape):
    return pl.BlockSpec(shape, lambda *_: (0,) * len(shape), pipeline_mode=pl.Buffered(1))


def _layernorm(z, g, b):
    mu = jnp.mean(z, axis=-1, keepdims=True)
    zc = z - mu
    var = jnp.mean(zc * zc, axis=-1, keepdims=True)
    return zc * lax.rsqrt(var + LN_EPS) * g + b


def _ffn_ln(y_ref, wg_ref, wu_ref, wd_ref, g_ref, b_ref, xb_ref, acc_ref, h_ref, overlap=None):
    xb_ref[...] = y_ref[...].astype(BF16)
    nc = wg_ref.shape[1] // FF_CHUNK

    def hidden(c):
        cols = slice(c * FF_CHUNK, (c + 1) * FF_CHUNK)
        xb = xb_ref[...]
        gate = jnp.dot(xb, wg_ref[:, cols], preferred_element_type=F32)
        up = jnp.dot(xb, wu_ref[:, cols], preferred_element_type=F32)
        h_ref[c % 2] = (gate * jax.nn.sigmoid(gate) * up).astype(BF16)

    def down(c):
        rows = slice(c * FF_CHUNK, (c + 1) * FF_CHUNK)
        y = jnp.dot(h_ref[c % 2], wd_ref[rows, :], preferred_element_type=F32)
        if c == 0:
            acc_ref[...] = y
        else:
            acc_ref[...] += y

    hidden(0)
    for c in range(nc):
        if c + 1 < nc:
            hidden(c + 1)
        down(c)
        if overlap is not None and c == OVERLAP_AFTER_CHUNK:
            overlap()
    return _layernorm(ALPHA * y_ref[...] + acc_ref[...], g_ref[...], b_ref[...])


def _two_halves(mixer, o_ref, ffn_refs, scratch):
    y_ref, xb_ref, acc_ref, h_ref = scratch
    tm = y_ref.shape[1]
    mixer(0, y_ref.at[0])
    for h in range(HALVES):
        nxt = functools.partial(mixer, h + 1, y_ref.at[h + 1]) if h + 1 < HALVES else None
        o_ref[0, h * tm:(h + 1) * tm, :] = _ffn_ln(y_ref.at[h], *ffn_refs, xb_ref.at[h], acc_ref.at[h],
                                                   h_ref.at[h], overlap=nxt)


def _ffn_specs():
    return [_resident((D_MODEL, D_FF)), _resident((D_MODEL, D_FF)), _resident((D_FF, D_MODEL)),
            _resident((1, D_MODEL)), _resident((1, D_MODEL))]


def _ffn_scratch(tm):
    return [pltpu.VMEM((HALVES, tm, D_MODEL), F32), pltpu.VMEM((HALVES, tm, D_MODEL), BF16),
            pltpu.VMEM((HALVES, tm, D_MODEL), F32), pltpu.VMEM((HALVES, 2, tm, FF_CHUNK), BF16)]


def _ffn_vmem(tm):
    rows = HALVES * tm
    return 3 * D_MODEL * D_FF * 2 + 4 * rows * D_MODEL * 4 + rows * D_MODEL * 10 + 8 * tm * FF_CHUNK * 4


def _layer0_kernel(x_ref, halo_ref, pw_ref, ps_ref, g0_ref, b0_ref, wg_ref, wu_ref, wd_ref, g1_ref, b1_ref,
                   o_ref, y_ref, xb_ref, acc_ref, h_ref):
    s = pl.program_id(1)
    ts = x_ref.shape[1] // HALVES

    def mixer(hf, y_out):
        x = x_ref[0, hf * ts:(hf + 1) * ts, :]
        if hf == 0:
            halo = jnp.where(s > 0, halo_ref[0], 0.0)
        else:
            halo = x_ref[0, hf * ts - POOL_HALO:hf * ts, :]
        xh = jnp.concatenate([halo, x], axis=0)
        t = (s * HALVES + hf) * ts + lax.broadcasted_iota(jnp.int32, (ts, 1), 0)
        mixed = []
        for g, win in enumerate(POOL_WINDOWS):
            a = xh[:, g * GROUP_W:(g + 1) * GROUP_W]
            shift = 1
            while shift < win:
                a = a + pltpu.roll(a, shift, 0)
                shift *= 2
            inv_cnt = 1.0 / jnp.minimum(t + 1, win).astype(F32)
            d = a[POOL_HALO:] * inv_cnt - x[:, g * GROUP_W:(g + 1) * GROUP_W]
            mixed.append(jnp.dot(d.astype(BF16), pw_ref[g], preferred_element_type=F32))
        h = jnp.concatenate(mixed, axis=-1) * ps_ref[...]
        y_out[...] = _layernorm(ALPHA * x + h, g0_ref[...], b0_ref[...])

    _two_halves(mixer, o_ref, (wg_ref, wu_ref, wd_ref, g1_ref, b1_ref), (y_ref, xb_ref, acc_ref, h_ref))


def _layer0(x, pw, ps, g0, b0, wg, wu, wd, g1, b1):
    B, S, D = x.shape
    ts = HALVES * ROW_TILE
    halo_blocks = ts // POOL_HALO
    vmem = _ffn_vmem(ROW_TILE) + 10 * ROW_TILE * D * 4
    return pl.pallas_call(
        _layer0_kernel,
        grid=(B, S // ts),
        in_specs=[
            pl.BlockSpec((1, ts, D), lambda b_, s_: (b_, s_, 0)),
            pl.BlockSpec((1, POOL_HALO, D), lambda b_, s_: (b_, jnp.maximum(s_ * halo_blocks - 1, 0), 0)),
            _resident((N_GROUPS, GROUP_W, GROUP_W)), _resident((1, D)), _resident((1, D)), _resident((1, D)),
        ] + _ffn_specs(),
        out_specs=pl.BlockSpec((1, ts, D), lambda b_, s_: (b_, s_, 0)),
        out_shape=jax.ShapeDtypeStruct((B, S, D), F32),
        scratch_shapes=_ffn_scratch(ROW_TILE),
        compiler_params=_params(("parallel", "parallel"), vmem),
        name="layer0_pool_ffn",
    )(x, x, pw, ps, g0, b0, wg, wu, wd, g1, b1)


def _qkv_kernel(x_ref, wqk_ref, wvt_ref, q_ref, k_ref, vt_ref):
    xb = x_ref[0].astype(BF16)
    q = jnp.dot(xb, wqk_ref[:, :D_ATTN], preferred_element_type=F32) * (SM_SCALE * LOG2_E)
    q_ref[0] = q.astype(BF16)
    k_ref[0] = jnp.dot(xb, wqk_ref[:, D_ATTN:], preferred_element_type=F32).astype(BF16)
    vt = lax.dot_general(wvt_ref[...], xb, NT_DIMS, preferred_element_type=F32)
    vt_ref[0] = vt.astype(BF16)


def _qkv(x, wqk, wvt):
    B, S, D = x.shape
    tm = ROW_TILE
    vmem = 3 * D * D_ATTN * 2 + 2 * tm * D * 4 + 6 * tm * D_ATTN * 2 + tm * D * 2 + 3 * tm * D_ATTN * 4
    return pl.pallas_call(
        _qkv_kernel,
        grid=(B, S // tm),
        in_specs=[
            pl.BlockSpec((1, tm, D), lambda b, i: (b, i, 0)),
            _resident((D, 2 * D_ATTN)), _resident((D_ATTN, D)),
        ],
        out_specs=[
            pl.BlockSpec((1, tm, D_ATTN), lambda b, i: (b, i, 0)),
            pl.BlockSpec((1, tm, D_ATTN), lambda b, i: (b, i, 0)),
            pl.BlockSpec((1, D_ATTN, tm), lambda b, i: (b, 0, i)),
        ],
        out_shape=[
            jax.ShapeDtypeStruct((B, S, D_ATTN), BF16),
            jax.ShapeDtypeStruct((B, S, D_ATTN), BF16),
            jax.ShapeDtypeStruct((B, D_ATTN, S), BF16),
        ],
        compiler_params=_params(("parallel", "parallel"), vmem),
        name="qkv_proj",
    )(x, wqk, wvt)


def _attn_kernel(lambda_init, q_ref, k_ref, vt_ref, bias_ref, lam_ref, sg_ref, o_ref,
                 qs_ref, m_ref, acc_ref, s0_ref, s1_ref, ns_ref):
    i = pl.program_id(2)
    blk = ATT_BLOCK
    nq = q_ref.shape[1] // blk
    assert nq % 2 == 0, "the key-block pipeline below pairs blocks and needs an even count per tile"
    diag_bias = bias_ref[0, 1]
    prev_bias = bias_ref[0, 0]

    lane = lax.broadcasted_iota(jnp.int32, (blk, V_DIM), 1)
    for hq in range(nq):
        q = q_ref[0, hq * blk:(hq + 1) * blk, :]
        zero = jnp.zeros_like(q)
        qs_ref[2 * hq] = jnp.where(lane < HEAD_DIM, q, zero)
        qs_ref[2 * hq + 1] = jnp.where(lane >= HEAD_DIM, q, zero)

    def scores(kc, cb):
        start = pl.multiple_of(kc * blk, blk)
        k = k_ref[0, pl.ds(start, blk), :]
        return lax.dot_general(k, qs_ref[cb], NT_DIMS, preferred_element_type=F32)

    def values(kc):
        start = pl.multiple_of(kc * blk, blk)
        vt = vt_ref[0, :, pl.ds(start, blk)]
        return jnp.concatenate([vt, jnp.ones((SUM_ROWS, blk), BF16)], axis=0)

    def update(cb, s, vt, bias, first):
        if bias is not None:
            s = s + bias
        m_new = jnp.max(s, axis=0, keepdims=True)
        if not first:
            m_prev = m_ref[cb]
            m_new = jnp.maximum(m_prev, m_new)
        p = jnp.exp2(s - m_new)
        pv = jnp.dot(vt, p.astype(BF16), preferred_element_type=F32)
        if first:
            acc_ref[cb] = pv
        else:
            acc_ref[cb] = acc_ref[cb] * jnp.exp2(m_prev - m_new) + pv
        m_ref[cb] = m_new

    base = nq * i
    near = [(d, hq, c) for d in range(nq - 1, -1, -1) for hq in range(d, nq) for c in range(2)]

    s_refs = (s0_ref, s1_ref)

    lp = lam_ref[...]
    lam = (jnp.exp(jnp.sum(lp[0:1] * lp[1:2], axis=-1, keepdims=True))
           - jnp.exp(jnp.sum(lp[2:3] * lp[3:4], axis=-1, keepdims=True)) + lambda_init)

    def finalize(hq):
        a0 = acc_ref[2 * hq]
        a1 = acc_ref[2 * hq + 1]
        ot = a0[:V_DIM] / a0[V_DIM:V_DIM + 1] - lam * (a1[:V_DIM] / a1[V_DIM:V_DIM + 1])
        ot = ot * lax.rsqrt(jnp.mean(ot * ot, axis=0, keepdims=True) + LN_EPS)
        gain = sg_ref[...] * (1.0 - lambda_init)
        o_ref[0, :, hq * blk:(hq + 1) * blk] = (ot * gain).astype(BF16)

    def near_phase(extra_scores, final):
        def score_job(j, d, hq, c):
            def run():
                ns_ref[j] = scores(base + d, 2 * hq + c)
            return run

        jobs = [score_job(j, d, hq, c) for j, (d, hq, c) in enumerate(near)] + extra_scores
        ahead = max(NEAR_LOOKAHEAD, len(jobs) - len(near))
        for job in jobs[:ahead]:
            job()
        vts = {}
        for j, (d, hq, c) in enumerate(near):
            if j + ahead < len(jobs):
                jobs[j + ahead]()
            if d not in vts:
                vts[d] = values(base + d)
            bias = diag_bias if hq == d else prev_bias if hq == d + 1 else None
            update(2 * hq + c, ns_ref[j], vts[d], bias, hq == d)
            if final and d == 0 and c == 1:
                finalize(hq)

    def score_into(kc, slot, cb):
        def run():
            s_refs[slot][cb] = scores(kc, cb)
        return run

    def issue_consume(kc, slot, issue_next=True, last=False):
        vt = values(kc)
        for cb in range(2 * nq):
            if issue_next:
                score_into(kc + 1, 1 - slot, cb)()
            bias = prev_bias if (last and cb < 2) else None
            update(cb, s_refs[slot][cb], vt, bias, False)
            if last and cb % 2 == 1:
                finalize(cb // 2)

    @pl.when(i == 0)
    def _():
        near_phase([], final=True)

    @pl.when(i > 0)
    def _():
        near_phase([score_into(0, 0, cb) for cb in range(2 * nq)], final=False)

        def pair(t, carry):
            issue_consume(2 * t, 0)
            issue_consume(2 * t + 1, 1)
            return carry

        lax.fori_loop(0, (nq // 2) * i - 1, pair, 0)
        issue_consume(base - 2, 0)
        issue_consume(base - 1, 1, issue_next=False, last=True)


def _rel_bucket(rel):
    n = jnp.maximum(rel, 0)
    nf = jnp.maximum(n, 1).astype(F32)
    large = MAX_EXACT + jnp.floor(jnp.log(nf / MAX_EXACT) / math.log(MAX_DISTANCE / MAX_EXACT)
                                  * (NUM_BUCKETS - MAX_EXACT)).astype(jnp.int32)
    return jnp.where(n < MAX_EXACT, n, jnp.minimum(large, NUM_BUCKETS - 1))


def _near_bias(rel_table):
    blk = ATT_BLOCK
    key = jnp.arange(blk)[:, None]
    query = jnp.arange(blk)[None, :]
    rel = jnp.stack([query - key + blk, query - key])
    bucket = _rel_bucket(rel)
    table = rel_table.astype(F32)
    table = (table - table[NUM_BUCKETS - 1]) * LOG2_E
    bias = jnp.zeros((N_HEADS,) + rel.shape, F32)
    for b in range(NUM_BUCKETS):
        bias = jnp.where(bucket[None] == b, table[b][:, None, None, None], bias)
    return jnp.where(rel[None] >= 0, bias, MASK_VALUE)


def _attention(q, k, vt, bias, lam_p, subln_g, lambda_init):
    B, S, _ = q.shape
    tq = ATT_TILE
    blk = ATT_BLOCK
    nq = tq // blk
    nchain = 2 * nq
    nnear = nq * (nq + 1)
    H = N_HEADS
    rows = V_DIM + SUM_ROWS
    vmem = (4 * S * V_DIM * 2 + 2 * 2 * blk * blk * 4 + (3 * nchain + nnear) * blk * blk * 4
            + 2 * nchain * rows * blk * 4 + 4 * tq * V_DIM * 2)
    return pl.pallas_call(
        functools.partial(_attn_kernel, lambda_init),
        grid=(B, H, S // tq),
        in_specs=[
            pl.BlockSpec((1, tq, V_DIM), lambda b, h, i: (b, i, h)),
            pl.BlockSpec((1, S, V_DIM), lambda b, h, i: (b, 0, h)),
            pl.BlockSpec((1, V_DIM, S), lambda b, h, i: (b, h, 0)),
            pl.BlockSpec((1, 2, blk, blk), lambda b, h, i: (h, 0, 0, 0)),
            pl.BlockSpec((4, HEAD_DIM), lambda b, h, i: (0, 0)),
            pl.BlockSpec((V_DIM, blk), lambda b, h, i: (0, 0)),
        ],
        out_specs=pl.BlockSpec((1, V_DIM, tq), lambda b, h, i: (b, h, i)),
        out_shape=jax.ShapeDtypeStruct((B, D_ATTN, S), BF16),
        scratch_shapes=[
            pltpu.VMEM((nchain, blk, V_DIM), BF16),
            pltpu.VMEM((nchain, 1, blk), F32),
            pltpu.VMEM((nchain, rows, blk), F32),
            pltpu.VMEM((nchain, blk, blk), F32),
            pltpu.VMEM((nchain, blk, blk), F32),
            pltpu.VMEM((nnear, blk, blk), F32),
        ],
        compiler_params=_params(("parallel", "parallel", "arbitrary"), vmem),
        name="diff_attn",
    )(q, k, vt, bias, lam_p, subln_g)


def _layer1_kernel(x_ref, a_ref, wo_ref, g0_ref, b0_ref, wg_ref, wu_ref, wd_ref, g1_ref, b1_ref,
                   o_ref, y_ref, xb_ref, acc_ref, h_ref):
    tm = x_ref.shape[1] // HALVES

    def mixer(hf, y_out):
        a_t = a_ref[0, :, hf * tm:(hf + 1) * tm]
        h = lax.dot_general(a_t, wo_ref[...], (((0,), (0,)), ((), ())), preferred_element_type=F32)
        y_out[...] = _layernorm(ALPHA * x_ref[0, hf * tm:(hf + 1) * tm, :] + h, g0_ref[...], b0_ref[...])

    _two_halves(mixer, o_ref, (wg_ref, wu_ref, wd_ref, g1_ref, b1_ref), (y_ref, xb_ref, acc_ref, h_ref))


def _layer1(x, at, wo, g0, b0, wg, wu, wd, g1, b1):
    B, S, D = x.shape
    tm = HALVES * ROW_TILE
    vmem = _ffn_vmem(ROW_TILE) + D * D_ATTN * 2 + 2 * tm * D_ATTN * 2 + 6 * ROW_TILE * D * 4
    return pl.pallas_call(
        _layer1_kernel,
        grid=(B, S // tm),
        in_specs=[
            pl.BlockSpec((1, tm, D), lambda b, i: (b, i, 0)),
            pl.BlockSpec((1, D_ATTN, tm), lambda b, i: (b, 0, i)),
            _resident((D_ATTN, D)), _resident((1, D)), _resident((1, D)),
        ] + _ffn_specs(),
        out_specs=pl.BlockSpec((1, tm, D), lambda b, i: (b, i, 0)),
        out_shape=jax.ShapeDtypeStruct((B, S, D), F32),
        scratch_shapes=_ffn_scratch(ROW_TILE),
        compiler_params=_params(("parallel", "parallel"), vmem),
        name="layer1_oproj_ffn",
    )(x, at, wo, g0, b0, wg, wu, wd, g1, b1)


def kernel(x, pool_w, pool_scale, w_qkv, w_o, lam_p, subln_g, rel_table, w_gate, w_up, w_down,
           ln_mix_g, ln_mix_b, ln_ffn_g, ln_ffn_b):
    B, S, D = x.shape
    M = B * S
    row = lambda a: a.reshape(1, -1).astype(F32)
    ffn_w = lambda i: (w_gate[i].astype(BF16), w_up[i].astype(BF16), w_down[i].astype(BF16),
                       row(ln_ffn_g[i]), row(ln_ffn_b[i]))

    x2 = _layer0(x, pool_w[0].astype(BF16), row(pool_scale[0]), row(ln_mix_g[0]), row(ln_mix_b[0]), *ffn_w(0))

    lambda_init = 0.8 - 0.6 * math.exp(-0.3 * 1)
    q, k, vt = _qkv(x2, w_qkv[0, :, :2 * D_ATTN].astype(BF16), w_qkv[0, :, 2 * D_ATTN:].T.astype(BF16))
    gain = jnp.broadcast_to(subln_g[0].astype(F32)[:, None], (V_DIM, ATT_BLOCK))
    at = _attention(q, k, vt, _near_bias(rel_table), lam_p[0].astype(F32), gain, lambda_init)
    return _layer1(x2, at, w_o[0].astype(BF16), row(ln_mix_g[1]), row(ln_mix_b[1]), *ffn_w(1))
```

```python
import functools
import math

import jax
import jax.numpy as jnp
from jax import lax
from jax.experimental import pallas as pl
from jax.experimental.pallas import tpu as pltpu

D_MODEL = 1024
DEPTH = 2
ALPHA = (2.0 * DEPTH) ** 0.25
LN_EPS = 1e-5
POOL_WINDOWS = (2, 4, 8, 16)
N_GROUPS = len(POOL_WINDOWS)
GROUP_W = D_MODEL // N_GROUPS
POOL_HALO = 16
HEAD_DIM = 64
N_HEADS = D_MODEL // (2 * HEAD_DIM)
V_DIM = 2 * HEAD_DIM
D_ATTN = N_HEADS * V_DIM
NUM_BUCKETS = 32
MAX_EXACT = NUM_BUCKETS // 2
MAX_DISTANCE = 128
D_FF = 2816
SM_SCALE = HEAD_DIM ** -0.5
LOG2_E = math.log2(math.e)

FF_CHUNK = 256
ROW_TILE = 512
HALVES = 2
OVERLAP_AFTER_CHUNK = 1
WIDE_STAGE_ROWS = 128
TALL_STAGE_ROWS = 256
ATT_BLOCK = 256
ATT_TILE = 1024
NEAR_LOOKAHEAD = 4
SUM_ROWS = 16
MASK_VALUE = -1e30
V7X_VMEM_BYTES = 64 * 1024 * 1024
VMEM_RESERVED_BYTES = 6 * 1024 * 1024
VMEM_TEMP_FACTOR = 1.5

F32 = jnp.float32
BF16 = jnp.bfloat16
NT_DIMS = (((1,), (1,)), ((), ()))


def _params(semantics, vmem_bytes):
    limit = min(int(vmem_bytes * VMEM_TEMP_FACTOR), V7X_VMEM_BYTES - VMEM_RESERVED_BYTES)
    return pltpu.CompilerParams(dimension_semantics=semantics, vmem_limit_bytes=limit)


def _resident(shape):
    return pl.BlockSpec(shape, lambda *_: (0,) * len(shape), pipeline_mode=pl.Buffered(1))


def _layernorm(z, g, b):
    mu = jnp.mean(z, axis=-1, keepdims=True)
    zc = z - mu
    var = jnp.mean(zc * zc, axis=-1, keepdims=True)
    return zc * lax.rsqrt(var + LN_EPS) * g + b


def _ffn_ln(y_ref, wg_ref, wu_ref, wd_ref, g_ref, b_ref, xb_ref, acc_ref, h_ref, overlap=None):
    xb_ref[...] = y_ref[...].astype(BF16)
    nc = wg_ref.shape[1] // FF_CHUNK

    def hidden(c):
        cols = slice(c * FF_CHUNK, (c + 1) * FF_CHUNK)
        xb = xb_ref[...]
        gate = jnp.dot(xb, wg_ref[:, cols], preferred_element_type=F32)
        up = jnp.dot(xb, wu_ref[:, cols], preferred_element_type=F32)
        h_ref[c % 2] = (gate * jax.nn.sigmoid(gate) * up).astype(BF16)

    def down(c):
        rows = slice(c * FF_CHUNK, (c + 1) * FF_CHUNK)
        y = jnp.dot(h_ref[c % 2], wd_ref[rows, :], preferred_element_type=F32)
        if c == 0:
            acc_ref[...] = y
        else:
            acc_ref[...] += y

    hidden(0)
    for c in range(nc):
        if c + 1 < nc:
            hidden(c + 1)
        down(c)
        if overlap is not None and c == OVERLAP_AFTER_CHUNK:
            overlap()
    return _layernorm(ALPHA * y_ref[...] + acc_ref[...], g_ref[...], b_ref[...])


def _two_halves(mixer, o_ref, ffn_refs, scratch):
    y_ref, xb_ref, acc_ref, h_ref = scratch
    tm = y_ref.shape[1]
    mixer(0, y_ref.at[0])
    for h in range(HALVES):
        nxt = functools.partial(mixer, h + 1, y_ref.at[h + 1]) if h + 1 < HALVES else None
        o_ref[0, h * tm:(h + 1) * tm, :] = _ffn_ln(y_ref.at[h], *ffn_refs, xb_ref.at[h], acc_ref.at[h],
                                                   h_ref.at[h], overlap=nxt)


def _fetch_bf16(src, dst_ref, stage_ref, sem_ref):
    rows = stage_ref.shape[1]
    n = src.shape[0] // rows

    def chunk(c):
        return pltpu.make_async_copy(src.at[pl.ds(c * rows, rows), :], stage_ref.at[c % 2], sem_ref.at[c % 2])

    chunk(0).start()
    for c in range(n):
        if c + 1 < n:
            chunk(c + 1).start()
        chunk(c).wait()
        dst_ref[c * rows:(c + 1) * rows, :] = stage_ref[c % 2].astype(BF16)


def _load_ffn_weights(layer, hbm_refs, vmem_refs, stage_refs, sem_refs):
    wg_hbm, wu_hbm, wd_hbm = hbm_refs
    wg_ref, wu_ref, wd_ref = vmem_refs
    wide_stage, tall_stage = stage_refs
    wide_sem, tall_sem = sem_refs

    @pl.when(jnp.logical_and(pl.program_id(0) == 0, pl.program_id(1) == 0))
    def _():
        _fetch_bf16(wg_hbm.at[layer], wg_ref, wide_stage, wide_sem)
        _fetch_bf16(wu_hbm.at[layer], wu_ref, wide_stage, wide_sem)
        _fetch_bf16(wd_hbm.at[layer], wd_ref, tall_stage, tall_sem)


def _ffn_specs():
    hbm = pl.BlockSpec(memory_space=pl.ANY)
    return [hbm, hbm, hbm, _resident((1, D_MODEL)), _resident((1, D_MODEL))]


def _ffn_scratch(tm):
    return [pltpu.VMEM((HALVES, tm, D_MODEL), F32), pltpu.VMEM((HALVES, tm, D_MODEL), BF16),
            pltpu.VMEM((HALVES, tm, D_MODEL), F32), pltpu.VMEM((HALVES, 2, tm, FF_CHUNK), BF16),
            pltpu.VMEM((D_MODEL, D_FF), BF16), pltpu.VMEM((D_MODEL, D_FF), BF16), pltpu.VMEM((D_FF, D_MODEL), BF16),
            pltpu.VMEM((2, WIDE_STAGE_ROWS, D_FF), F32), pltpu.VMEM((2, TALL_STAGE_ROWS, D_MODEL), F32),
            pltpu.SemaphoreType.DMA((2,)), pltpu.SemaphoreType.DMA((2,))]


def _ffn_vmem(tm):
    rows = HALVES * tm
    stage = 2 * (WIDE_STAGE_ROWS * D_FF + TALL_STAGE_ROWS * D_MODEL) * 4
    return (3 * D_MODEL * D_FF * 2 + stage + 4 * rows * D_MODEL * 4 + rows * D_MODEL * 10
            + 8 * tm * FF_CHUNK * 4)


def _layer0_kernel(layer, x_ref, halo_ref, pw_ref, ps_ref, g0_ref, b0_ref, wg_hbm, wu_hbm, wd_hbm, g1_ref, b1_ref,
                   o_ref, y_ref, xb_ref, acc_ref, h_ref, wg_ref, wu_ref, wd_ref, wide_stage, tall_stage,
                   wide_sem, tall_sem):
    _load_ffn_weights(layer, (wg_hbm, wu_hbm, wd_hbm), (wg_ref, wu_ref, wd_ref), (wide_stage, tall_stage),
                      (wide_sem, tall_sem))
    s = pl.program_id(1)
    ts = x_ref.shape[1] // HALVES

    def mixer(hf, y_out):
        x = x_ref[0, hf * ts:(hf + 1) * ts, :]
        if hf == 0:
            halo = jnp.where(s > 0, halo_ref[0], 0.0)
        else:
            halo = x_ref[0, hf * ts - POOL_HALO:hf * ts, :]
        xh = jnp.concatenate([halo, x], axis=0)
        t = (s * HALVES + hf) * ts + lax.broadcasted_iota(jnp.int32, (ts, 1), 0)
        mixed = []
        for g, win in enumerate(POOL_WINDOWS):
            a = xh[:, g * GROUP_W:(g + 1) * GROUP_W]
            shift = 1
            while shift < win:
                a = a + pltpu.roll(a, shift, 0)
                shift *= 2
            inv_cnt = 1.0 / jnp.minimum(t + 1, win).astype(F32)
            d = a[POOL_HALO:] * inv_cnt - x[:, g * GROUP_W:(g + 1) * GROUP_W]
            mixed.append(jnp.dot(d.astype(BF16), pw_ref[g], preferred_element_type=F32))
        h = jnp.concatenate(mixed, axis=-1) * ps_ref[...]
        y_out[...] = _layernorm(ALPHA * x + h, g0_ref[...], b0_ref[...])

    _two_halves(mixer, o_ref, (wg_ref, wu_ref, wd_ref, g1_ref, b1_ref), (y_ref, xb_ref, acc_ref, h_ref))


def _layer0(layer, x, pw, ps, g0, b0, wg, wu, wd, g1, b1):
    B, S, D = x.shape
    ts = HALVES * ROW_TILE
    halo_blocks = ts // POOL_HALO
    vmem = _ffn_vmem(ROW_TILE) + 10 * ROW_TILE * D * 4
    return pl.pallas_call(
        functools.partial(_layer0_kernel, layer),
        grid=(B, S // ts),
        in_specs=[
            pl.BlockSpec((1, ts, D), lambda b_, s_: (b_, s_, 0)),
            pl.BlockSpec((1, POOL_HALO, D), lambda b_, s_: (b_, jnp.maximum(s_ * halo_blocks - 1, 0), 0)),
            _resident((N_GROUPS, GROUP_W, GROUP_W)), _resident((1, D)), _resident((1, D)), _resident((1, D)),
        ] + _ffn_specs(),
        out_specs=pl.BlockSpec((1, ts, D), lambda b_, s_: (b_, s_, 0)),
        out_shape=jax.ShapeDtypeStruct((B, S, D), F32),
        scratch_shapes=_ffn_scratch(ROW_TILE),
        compiler_params=_params(("arbitrary", "arbitrary"), vmem),
        name="layer0_pool_ffn",
    )(x, x, pw, ps, g0, b0, wg, wu, wd, g1, b1)


def _qkv_kernel(x_ref, wqk_ref, wvt_ref, q_ref, k_ref, vt_ref):
    xb = x_ref[0].astype(BF16)
    q = jnp.dot(xb, wqk_ref[:, :D_ATTN], preferred_element_type=F32) * (SM_SCALE * LOG2_E)
    q_ref[0] = q.astype(BF16)
    k_ref[0] = jnp.dot(xb, wqk_ref[:, D_ATTN:], preferred_element_type=F32).astype(BF16)
    vt = lax.dot_general(wvt_ref[...], xb, NT_DIMS, preferred_element_type=F32)
    vt_ref[0] = vt.astype(BF16)


def _qkv(x, wqk, wvt):
    B, S, D = x.shape
    tm = ROW_TILE
    vmem = 3 * D * D_ATTN * 2 + 2 * tm * D * 4 + 6 * tm * D_ATTN * 2 + tm * D * 2 + 3 * tm * D_ATTN * 4
    return pl.pallas_call(
        _qkv_kernel,
        grid=(B, S // tm),
        in_specs=[
            pl.BlockSpec((1, tm, D), lambda b, i: (b, i, 0)),
            _resident((D, 2 * D_ATTN)), _resident((D_ATTN, D)),
        ],
        out_specs=[
            pl.BlockSpec((1, tm, D_ATTN), lambda b, i: (b, i, 0)),
            pl.BlockSpec((1, tm, D_ATTN), lambda b, i: (b, i, 0)),
            pl.BlockSpec((1, D_ATTN, tm), lambda b, i: (b, 0, i)),
        ],
        out_shape=[
            jax.ShapeDtypeStruct((B, S, D_ATTN), BF16),
            jax.ShapeDtypeStruct((B, S, D_ATTN), BF16),
            jax.ShapeDtypeStruct((B, D_ATTN, S), BF16),
        ],
        compiler_params=_params(("parallel", "parallel"), vmem),
        name="qkv_proj",
    )(x, wqk, wvt)


def _attn_kernel(lambda_init, q_ref, k_ref, vt_ref, bias_ref, lam_ref, sg_ref, o_ref,
                 qs_ref, m_ref, acc_ref, s0_ref, s1_ref, ns_ref):
    i = pl.program_id(2)
    blk = ATT_BLOCK
    nq = q_ref.shape[1] // blk
    assert nq % 2 == 0, "the key-block pipeline below pairs blocks and needs an even count per tile"
    diag_bias = bias_ref[0, 1]
    prev_bias = bias_ref[0, 0]

    lane = lax.broadcasted_iota(jnp.int32, (blk, V_DIM), 1)
    for hq in range(nq):
        q = q_ref[0, hq * blk:(hq + 1) * blk, :]
        zero = jnp.zeros_like(q)
        qs_ref[2 * hq] = jnp.where(lane < HEAD_DIM, q, zero)
        qs_ref[2 * hq + 1] = jnp.where(lane >= HEAD_DIM, q, zero)

    def scores(kc, cb):
        start = pl.multiple_of(kc * blk, blk)
        k = k_ref[0, pl.ds(start, blk), :]
        return lax.dot_general(k, qs_ref[cb], NT_DIMS, preferred_element_type=F32)

    def values(kc):
        start = pl.multiple_of(kc * blk, blk)
        vt = vt_ref[0, :, pl.ds(start, blk)]
        return jnp.concatenate([vt, jnp.ones((SUM_ROWS, blk), BF16)], axis=0)

    def update(cb, s, vt, bias, first):
        if bias is not None:
            s = s + bias
        m_new = jnp.max(s, axis=0, keepdims=True)
        if not first:
            m_prev = m_ref[cb]
            m_new = jnp.maximum(m_prev, m_new)
        p = jnp.exp2(s - m_new)
        pv = jnp.dot(vt, p.astype(BF16), preferred_element_type=F32)
        if first:
            acc_ref[cb] = pv
        else:
            acc_ref[cb] = acc_ref[cb] * jnp.exp2(m_prev - m_new) + pv
        m_ref[cb] = m_new

    base = nq * i
    near = [(d, hq, c) for d in range(nq - 1, -1, -1) for hq in range(d, nq) for c in range(2)]

    s_refs = (s0_ref, s1_ref)

    lp = lam_ref[...]
    lam = (jnp.exp(jnp.sum(lp[0:1] * lp[1:2], axis=-1, keepdims=True))
           - jnp.exp(jnp.sum(lp[2:3] * lp[3:4], axis=-1, keepdims=True)) + lambda_init)

    def finalize(hq):
        a0 = acc_ref[2 * hq]
        a1 = acc_ref[2 * hq + 1]
        ot = a0[:V_DIM] / a0[V_DIM:V_DIM + 1] - lam * (a1[:V_DIM] / a1[V_DIM:V_DIM + 1])
        ot = ot * lax.rsqrt(jnp.mean(ot * ot, axis=0, keepdims=True) + LN_EPS)
        gain = sg_ref[...] * (1.0 - lambda_init)
        o_ref[0, :, hq * blk:(hq + 1) * blk] = (ot * gain).astype(BF16)

    def near_phase(extra_scores, final):
        def score_job(j, d, hq, c):
            def run():
                ns_ref[j] = scores(base + d, 2 * hq + c)
            return run

        jobs = [score_job(j, d, hq, c) for j, (d, hq, c) in enumerate(near)] + extra_scores
        ahead = max(NEAR_LOOKAHEAD, len(jobs) - len(near))
        for job in jobs[:ahead]:
            job()
        vts = {}
        for j, (d, hq, c) in enumerate(near):
            if j + ahead < len(jobs):
                jobs[j + ahead]()
            if d not in vts:
                vts[d] = values(base + d)
            bias = diag_bias if hq == d else prev_bias if hq == d + 1 else None
            update(2 * hq + c, ns_ref[j], vts[d], bias, hq == d)
            if final and d == 0 and c == 1:
                finalize(hq)

    def score_into(kc, slot, cb):
        def run():
            s_refs[slot][cb] = scores(kc, cb)
        return run

    def issue_consume(kc, slot, issue_next=True, last=False):
        vt = values(kc)
        for cb in range(2 * nq):
            if issue_next:
                score_into(kc + 1, 1 - slot, cb)()
            bias = prev_bias if (last and cb < 2) else None
            update(cb, s_refs[slot][cb], vt, bias, False)
            if last and cb % 2 == 1:
                finalize(cb // 2)

    @pl.when(i == 0)
    def _():
        near_phase([], final=True)

    @pl.when(i > 0)
    def _():
        near_phase([score_into(0, 0, cb) for cb in range(2 * nq)], final=False)

        def pair(t, carry):
            issue_consume(2 * t, 0)
            issue_consume(2 * t + 1, 1)
            return carry

        lax.fori_loop(0, (nq // 2) * i - 1, pair, 0)
        issue_consume(base - 2, 0)
        issue_consume(base - 1, 1, issue_next=False, last=True)


def _rel_bucket(rel):
    n = jnp.maximum(rel, 0)
    nf = jnp.maximum(n, 1).astype(F32)
    large = MAX_EXACT + jnp.floor(jnp.log(nf / MAX_EXACT) / math.log(MAX_DISTANCE / MAX_EXACT)
                                  * (NUM_BUCKETS - MAX_EXACT)).astype(jnp.int32)
    return jnp.where(n < MAX_EXACT, n, jnp.minimum(large, NUM_BUCKETS - 1))


def _near_bias(rel_table):
    blk = ATT_BLOCK
    key = jnp.arange(blk)[:, None]
    query = jnp.arange(blk)[None, :]
    rel = jnp.stack([query - key + blk, query - key])
    bucket = _rel_bucket(rel)
    table = rel_table.astype(F32)
    table = (table - table[NUM_BUCKETS - 1]) * LOG2_E
    bias = jnp.zeros((N_HEADS,) + rel.shape, F32)
    for b in range(NUM_BUCKETS):
        bias = jnp.where(bucket[None] == b, table[b][:, None, None, None], bias)
    return jnp.where(rel[None] >= 0, bias, MASK_VALUE)


def _attention(q, k, vt, bias, lam_p, subln_g, lambda_init):
    B, S, _ = q.shape
    tq = ATT_TILE
    blk = ATT_BLOCK
    nq = tq // blk
    nchain = 2 * nq
    nnear = nq * (nq + 1)
    H = N_HEADS
    rows = V_DIM + SUM_ROWS
    vmem = (4 * S * V_DIM * 2 + 2 * 2 * blk * blk * 4 + (3 * nchain + nnear) * blk * blk * 4
            + 2 * nchain * rows * blk * 4 + 4 * tq * V_DIM * 2)
    return pl.pallas_call(
        functools.partial(_attn_kernel, lambda_init),
        grid=(B, H, S // tq),
        in_specs=[
            pl.BlockSpec((1, tq, V_DIM), lambda b, h, i: (b, i, h)),
            pl.BlockSpec((1, S, V_DIM), lambda b, h, i: (b, 0, h)),
            pl.BlockSpec((1, V_DIM, S), lambda b, h, i: (b, h, 0)),
            pl.BlockSpec((1, 2, blk, blk), lambda b, h, i: (h, 0, 0, 0)),
            pl.BlockSpec((4, HEAD_DIM), lambda b, h, i: (0, 0)),
            pl.BlockSpec((V_DIM, blk), lambda b, h, i: (0, 0)),
        ],
        out_specs=pl.BlockSpec((1, V_DIM, tq), lambda b, h, i: (b, h, i)),
        out_shape=jax.ShapeDtypeStruct((B, D_ATTN, S), BF16),
        scratch_shapes=[
            pltpu.VMEM((nchain, blk, V_DIM), BF16),
            pltpu.VMEM((nchain, 1, blk), F32),
            pltpu.VMEM((nchain, rows, blk), F32),
            pltpu.VMEM((nchain, blk, blk), F32),
            pltpu.VMEM((nchain, blk, blk), F32),
            pltpu.VMEM((nnear, blk, blk), F32),
        ],
        compiler_params=_params(("parallel", "parallel", "arbitrary"), vmem),
        name="diff_attn",
    )(q, k, vt, bias, lam_p, subln_g)


def _layer1_kernel(layer, x_ref, a_ref, wo_ref, g0_ref, b0_ref, wg_hbm, wu_hbm, wd_hbm, g1_ref, b1_ref,
                   o_ref, y_ref, xb_ref, acc_ref, h_ref, wg_ref, wu_ref, wd_ref, wide_stage, tall_stage,
                   wide_sem, tall_sem):
    _load_ffn_weights(layer, (wg_hbm, wu_hbm, wd_hbm), (wg_ref, wu_ref, wd_ref), (wide_stage, tall_stage),
                      (wide_sem, tall_sem))
    tm = x_ref.shape[1] // HALVES

    def mixer(hf, y_out):
        a_t = a_ref[0, :, hf * tm:(hf + 1) * tm]
        h = lax.dot_general(a_t, wo_ref[...], (((0,), (0,)), ((), ())), preferred_element_type=F32)
        y_out[...] = _layernorm(ALPHA * x_ref[0, hf * tm:(hf + 1) * tm, :] + h, g0_ref[...], b0_ref[...])

    _two_halves(mixer, o_ref, (wg_ref, wu_ref, wd_ref, g1_ref, b1_ref), (y_ref, xb_ref, acc_ref, h_ref))


def _layer1(layer, x, at, wo, g0, b0, wg, wu, wd, g1, b1):
    B, S, D = x.shape
    tm = HALVES * ROW_TILE
    vmem = _ffn_vmem(ROW_TILE) + D * D_ATTN * 2 + 2 * tm * D_ATTN * 2 + 6 * ROW_TILE * D * 4
    return pl.pallas_call(
        functools.partial(_layer1_kernel, layer),
        grid=(B, S // tm),
        in_specs=[
            pl.BlockSpec((1, tm, D), lambda b, i: (b, i, 0)),
            pl.BlockSpec((1, D_ATTN, tm), lambda b, i: (b, 0, i)),
            _resident((D_ATTN, D)), _resident((1, D)), _resident((1, D)),
        ] + _ffn_specs(),
        out_specs=pl.BlockSpec((1, tm, D), lambda b, i: (b, i, 0)),
        out_shape=jax.ShapeDtypeStruct((B, S, D), F32),
        scratch_shapes=_ffn_scratch(ROW_TILE),
        compiler_params=_params(("arbitrary", "arbitrary"), vmem),
        name="layer1_oproj_ffn",
    )(x, at, wo, g0, b0, wg, wu, wd, g1, b1)


def kernel(x, pool_w, pool_scale, w_qkv, w_o, lam_p, subln_g, rel_table, w_gate, w_up, w_down,
           ln_mix_g, ln_mix_b, ln_ffn_g, ln_ffn_b):
    row = lambda a: a.reshape(1, -1).astype(F32)
    ffn_w = lambda i: (w_gate.astype(F32), w_up.astype(F32), w_down.astype(F32),
                       row(ln_ffn_g[i]), row(ln_ffn_b[i]))

    x2 = _layer0(0, x, pool_w[0].astype(BF16), row(pool_scale[0]), row(ln_mix_g[0]), row(ln_mix_b[0]), *ffn_w(0))

    lambda_init = 0.8 - 0.6 * math.exp(-0.3 * 1)
    q, k, vt = _qkv(x2, w_qkv[0, :, :2 * D_ATTN].astype(BF16), w_qkv[0, :, 2 * D_ATTN:].T.astype(BF16))
    gain = jnp.broadcast_to(subln_g[0].astype(F32)[:, None], (V_DIM, ATT_BLOCK))
    at = _attention(q, k, vt, _near_bias(rel_table), lam_p[0].astype(F32), gain, lambda_init)
    return _layer1(1, x2, at, w_o[0].astype(BF16), row(ln_mix_g[1]), row(ln_mix_b[1]), *ffn_w(1))
```

```python
import functools
import math

import jax
import jax.numpy as jnp
from jax import lax
from jax.experimental import pallas as pl
from jax.experimental.pallas import tpu as pltpu

D_MODEL = 1024
DEPTH = 2
ALPHA = (2.0 * DEPTH) ** 0.25
LN_EPS = 1e-5
POOL_WINDOWS = (2, 4, 8, 16)
N_GROUPS = len(POOL_WINDOWS)
GROUP_W = D_MODEL // N_GROUPS
POOL_HALO = 16
HEAD_DIM = 64
N_HEADS = D_MODEL // (2 * HEAD_DIM)
V_DIM = 2 * HEAD_DIM
D_ATTN = N_HEADS * V_DIM
NUM_BUCKETS = 32
MAX_EXACT = NUM_BUCKETS // 2
MAX_DISTANCE = 128
D_FF = 2816
SM_SCALE = HEAD_DIM ** -0.5
LOG2_E = math.log2(math.e)

FF_CHUNK = 256
ROW_TILE = 512
HALVES = 2
OVERLAP_AFTER_CHUNK = 1
STAGE_SLOTS = 6
WIDE_STAGE_ROWS = 64
TALL_STAGE_ROWS = 128
ATT_BLOCK = 256
ATT_TILE = 1024
NEAR_LOOKAHEAD = 4
SUM_ROWS = 16
MASK_VALUE = -1e30
V7X_VMEM_BYTES = 64 * 1024 * 1024
VMEM_RESERVED_BYTES = 6 * 1024 * 1024
VMEM_TEMP_FACTOR = 1.5

F32 = jnp.float32
BF16 = jnp.bfloat16
NT_DIMS = (((1,), (1,)), ((), ()))


def _params(semantics, vmem_bytes):
    limit = min(int(vmem_bytes * VMEM_TEMP_FACTOR), V7X_VMEM_BYTES - VMEM_RESERVED_BYTES)
    return pltpu.CompilerParams(dimension_semantics=semantics, vmem_limit_bytes=limit)


def _resident(shape):
    return pl.BlockSpec(shape, lambda *_: (0,) * len(shape), pipeline_mode=pl.Buffered(1))


def _layernorm(z, g, b):
    mu = jnp.mean(z, axis=-1, keepdims=True)
    zc = z - mu
    var = jnp.mean(zc * zc, axis=-1, keepdims=True)
    return zc * lax.rsqrt(var + LN_EPS) * g + b


def _ffn_ln(y_ref, wg_ref, wu_ref, wd_ref, g_ref, b_ref, xb_ref, acc_ref, h_ref, overlap=None):
    xb_ref[...] = y_ref[...].astype(BF16)
    nc = wg_ref.shape[1] // FF_CHUNK

    def hidden(c):
        cols = slice(c * FF_CHUNK, (c + 1) * FF_CHUNK)
        xb = xb_ref[...]
        gate = jnp.dot(xb, wg_ref[:, cols], preferred_element_type=F32)
        up = jnp.dot(xb, wu_ref[:, cols], preferred_element_type=F32)
        h_ref[c % 2] = (gate * jax.nn.sigmoid(gate) * up).astype(BF16)

    def down(c):
        rows = slice(c * FF_CHUNK, (c + 1) * FF_CHUNK)
        y = jnp.dot(h_ref[c % 2], wd_ref[rows, :], preferred_element_type=F32)
        if c == 0:
            acc_ref[...] = y
        else:
            acc_ref[...] += y

    hidden(0)
    for c in range(nc):
        if c + 1 < nc:
            hidden(c + 1)
        down(c)
        if overlap is not None and c == OVERLAP_AFTER_CHUNK:
            overlap()
    return _layernorm(ALPHA * y_ref[...] + acc_ref[...], g_ref[...], b_ref[...])


def _two_halves(mixer, o_ref, ffn_refs, scratch):
    y_ref, xb_ref, acc_ref, h_ref = scratch
    tm = y_ref.shape[1]
    mixer(0, y_ref.at[0])
    for h in range(HALVES):
        nxt = functools.partial(mixer, h + 1, y_ref.at[h + 1]) if h + 1 < HALVES else None
        o_ref[0, h * tm:(h + 1) * tm, :] = _ffn_ln(y_ref.at[h], *ffn_refs, xb_ref.at[h], acc_ref.at[h],
                                                   h_ref.at[h], overlap=nxt)


def _fetch_bf16(src, dst_ref, stage_ref, sem_ref):
    nslot, rows = stage_ref.shape[0], stage_ref.shape[1]
    assert src.shape[0] % rows == 0
    n = src.shape[0] // rows

    def chunk(c):
        slot = c % nslot
        return pltpu.make_async_copy(src.at[pl.ds(c * rows, rows), :], stage_ref.at[slot], sem_ref.at[slot])

    for c in range(min(nslot, n)):
        chunk(c).start()
    for c in range(n):
        chunk(c).wait()
        dst_ref[c * rows:(c + 1) * rows, :] = stage_ref[c % nslot].astype(BF16)
        if c + nslot < n:
            chunk(c + nslot).start()


def _load_ffn_weights(layer, hbm_refs, vmem_refs, stage_refs, sem_refs):
    wg_hbm, wu_hbm, wd_hbm = hbm_refs
    wg_ref, wu_ref, wd_ref = vmem_refs
    wide_stage, tall_stage = stage_refs
    wide_sem, tall_sem = sem_refs

    @pl.when(jnp.logical_and(pl.program_id(0) == 0, pl.program_id(1) == 0))
    def _():
        _fetch_bf16(wg_hbm.at[layer], wg_ref, wide_stage, wide_sem)
        _fetch_bf16(wu_hbm.at[layer], wu_ref, wide_stage, wide_sem)
        _fetch_bf16(wd_hbm.at[layer], wd_ref, tall_stage, tall_sem)


def _ffn_specs():
    hbm = pl.BlockSpec(memory_space=pl.ANY)
    return [hbm, hbm, hbm, _resident((1, D_MODEL)), _resident((1, D_MODEL))]


def _ffn_scratch(tm):
    return [pltpu.VMEM((HALVES, tm, D_MODEL), F32), pltpu.VMEM((HALVES, tm, D_MODEL), BF16),
            pltpu.VMEM((HALVES, tm, D_MODEL), F32), pltpu.VMEM((HALVES, 2, tm, FF_CHUNK), BF16),
            pltpu.VMEM((D_MODEL, D_FF), BF16), pltpu.VMEM((D_MODEL, D_FF), BF16), pltpu.VMEM((D_FF, D_MODEL), BF16),
            pltpu.VMEM((STAGE_SLOTS, WIDE_STAGE_ROWS, D_FF), F32),
            pltpu.VMEM((STAGE_SLOTS, TALL_STAGE_ROWS, D_MODEL), F32),
            pltpu.SemaphoreType.DMA((STAGE_SLOTS,)), pltpu.SemaphoreType.DMA((STAGE_SLOTS,))]


def _ffn_vmem(tm):
    rows = HALVES * tm
    stage = STAGE_SLOTS * (WIDE_STAGE_ROWS * D_FF + TALL_STAGE_ROWS * D_MODEL) * 4
    return (3 * D_MODEL * D_FF * 2 + stage + 4 * rows * D_MODEL * 4 + rows * D_MODEL * 10
            + 8 * tm * FF_CHUNK * 4)


def _layer0_kernel(layer, x_ref, halo_ref, pw_ref, ps_ref, g0_ref, b0_ref, wg_hbm, wu_hbm, wd_hbm, g1_ref, b1_ref,
                   o_ref, y_ref, xb_ref, acc_ref, h_ref, wg_ref, wu_ref, wd_ref, wide_stage, tall_stage,
                   wide_sem, tall_sem):
    _load_ffn_weights(layer, (wg_hbm, wu_hbm, wd_hbm), (wg_ref, wu_ref, wd_ref), (wide_stage, tall_stage),
                      (wide_sem, tall_sem))
    s = pl.program_id(1)
    ts = x_ref.shape[1] // HALVES

    def mixer(hf, y_out):
        x = x_ref[0, hf * ts:(hf + 1) * ts, :]
        if hf == 0:
            halo = jnp.where(s > 0, halo_ref[0], 0.0)
        else:
            halo = x_ref[0, hf * ts - POOL_HALO:hf * ts, :]
        xh = jnp.concatenate([halo, x], axis=0)
        t = (s * HALVES + hf) * ts + lax.broadcasted_iota(jnp.int32, (ts, 1), 0)
        mixed = []
        for g, win in enumerate(POOL_WINDOWS):
            a = xh[:, g * GROUP_W:(g + 1) * GROUP_W]
            shift = 1
            while shift < win:
                a = a + pltpu.roll(a, shift, 0)
                shift *= 2
            inv_cnt = 1.0 / jnp.minimum(t + 1, win).astype(F32)
            d = a[POOL_HALO:] * inv_cnt - x[:, g * GROUP_W:(g + 1) * GROUP_W]
            mixed.append(jnp.dot(d.astype(BF16), pw_ref[g], preferred_element_type=F32))
        h = jnp.concatenate(mixed, axis=-1) * ps_ref[...]
        y_out[...] = _layernorm(ALPHA * x + h, g0_ref[...], b0_ref[...])

    _two_halves(mixer, o_ref, (wg_ref, wu_ref, wd_ref, g1_ref, b1_ref), (y_ref, xb_ref, acc_ref, h_ref))


def _layer0(layer, x, pw, ps, g0, b0, wg, wu, wd, g1, b1):
    B, S, D = x.shape
    ts = HALVES * ROW_TILE
    halo_blocks = ts // POOL_HALO
    vmem = _ffn_vmem(ROW_TILE) + 10 * ROW_TILE * D * 4
    return pl.pallas_call(
        functools.partial(_layer0_kernel, layer),
        grid=(B, S // ts),
        in_specs=[
            pl.BlockSpec((1, ts, D), lambda b_, s_: (b_, s_, 0)),
            pl.BlockSpec((1, POOL_HALO, D), lambda b_, s_: (b_, jnp.maximum(s_ * halo_blocks - 1, 0), 0)),
            _resident((N_GROUPS, GROUP_W, GROUP_W)), _resident((1, D)), _resident((1, D)), _resident((1, D)),
        ] + _ffn_specs(),
        out_specs=pl.BlockSpec((1, ts, D), lambda b_, s_: (b_, s_, 0)),
        out_shape=jax.ShapeDtypeStruct((B, S, D), F32),
        scratch_shapes=_ffn_scratch(ROW_TILE),
        compiler_params=_params(("arbitrary", "arbitrary"), vmem),
        name="layer0_pool_ffn",
    )(x, x, pw, ps, g0, b0, wg, wu, wd, g1, b1)


def _qkv_kernel(x_ref, wqk_ref, wvt_ref, q_ref, k_ref, vt_ref):
    xb = x_ref[0].astype(BF16)
    q = jnp.dot(xb, wqk_ref[:, :D_ATTN], preferred_element_type=F32) * (SM_SCALE * LOG2_E)
    q_ref[0] = q.astype(BF16)
    k_ref[0] = jnp.dot(xb, wqk_ref[:, D_ATTN:], preferred_element_type=F32).astype(BF16)
    vt = lax.dot_general(wvt_ref[...], xb, NT_DIMS, preferred_element_type=F32)
    vt_ref[0] = vt.astype(BF16)


def _qkv(x, wqk, wvt):
    B, S, D = x.shape
    tm = ROW_TILE
    vmem = 3 * D * D_ATTN * 2 + 2 * tm * D * 4 + 6 * tm * D_ATTN * 2 + tm * D * 2 + 3 * tm * D_ATTN * 4
    return pl.pallas_call(
        _qkv_kernel,
        grid=(B, S // tm),
        in_specs=[
            pl.BlockSpec((1, tm, D), lambda b, i: (b, i, 0)),
            _resident((D, 2 * D_ATTN)), _resident((D_ATTN, D)),
        ],
        out_specs=[
            pl.BlockSpec((1, tm, D_ATTN), lambda b, i: (b, i, 0)),
            pl.BlockSpec((1, tm, D_ATTN), lambda b, i: (b, i, 0)),
            pl.BlockSpec((1, D_ATTN, tm), lambda b, i: (b, 0, i)),
        ],
        out_shape=[
            jax.ShapeDtypeStruct((B, S, D_ATTN), BF16),
            jax.ShapeDtypeStruct((B, S, D_ATTN), BF16),
            jax.ShapeDtypeStruct((B, D_ATTN, S), BF16),
        ],
        compiler_params=_params(("parallel", "parallel"), vmem),
        name="qkv_proj",
    )(x, wqk, wvt)


def _attn_kernel(lambda_init, q_ref, k_ref, vt_ref, bias_ref, lam_ref, sg_ref, o_ref,
                 qs_ref, m_ref, acc_ref, s0_ref, s1_ref, ns_ref):
    i = pl.program_id(2)
    blk = ATT_BLOCK
    nq = q_ref.shape[1] // blk
    assert nq % 2 == 0, "the key-block pipeline below pairs blocks and needs an even count per tile"
    diag_bias = bias_ref[0, 1]
    prev_bias = bias_ref[0, 0]

    lane = lax.broadcasted_iota(jnp.int32, (blk, V_DIM), 1)
    for hq in range(nq):
        q = q_ref[0, hq * blk:(hq + 1) * blk, :]
        zero = jnp.zeros_like(q)
        qs_ref[2 * hq] = jnp.where(lane < HEAD_DIM, q, zero)
        qs_ref[2 * hq + 1] = jnp.where(lane >= HEAD_DIM, q, zero)

    def scores(kc, cb):
        start = pl.multiple_of(kc * blk, blk)
        k = k_ref[0, pl.ds(start, blk), :]
        return lax.dot_general(k, qs_ref[cb], NT_DIMS, preferred_element_type=F32)

    def values(kc):
        start = pl.multiple_of(kc * blk, blk)
        vt = vt_ref[0, :, pl.ds(start, blk)]
        return jnp.concatenate([vt, jnp.ones((SUM_ROWS, blk), BF16)], axis=0)

    def update(cb, s, vt, bias, first):
        if bias is not None:
            s = s + bias
        m_new = jnp.max(s, axis=0, keepdims=True)
        if not first:
            m_prev = m_ref[cb]
            m_new = jnp.maximum(m_prev, m_new)
        p = jnp.exp2(s - m_new)
        pv = jnp.dot(vt, p.astype(BF16), preferred_element_type=F32)
        if first:
            acc_ref[cb] = pv
        else:
            acc_ref[cb] = acc_ref[cb] * jnp.exp2(m_prev - m_new) + pv
        m_ref[cb] = m_new

    base = nq * i
    near = [(d, hq, c) for d in range(nq - 1, -1, -1) for hq in range(d, nq) for c in range(2)]

    s_refs = (s0_ref, s1_ref)

    lp = lam_ref[...]
    lam = (jnp.exp(jnp.sum(lp[0:1] * lp[1:2], axis=-1, keepdims=True))
           - jnp.exp(jnp.sum(lp[2:3] * lp[3:4], axis=-1, keepdims=True)) + lambda_init)

    def finalize(hq):
        a0 = acc_ref[2 * hq]
        a1 = acc_ref[2 * hq + 1]
        ot = a0[:V_DIM] / a0[V_DIM:V_DIM + 1] - lam * (a1[:V_DIM] / a1[V_DIM:V_DIM + 1])
        ot = ot * lax.rsqrt(jnp.mean(ot * ot, axis=0, keepdims=True) + LN_EPS)
        gain = sg_ref[...] * (1.0 - lambda_init)
        o_ref[0, :, hq * blk:(hq + 1) * blk] = (ot * gain).astype(BF16)

    def near_phase(extra_scores, final):
        def score_job(j, d, hq, c):
            def run():
                ns_ref[j] = scores(base + d, 2 * hq + c)
            return run

        jobs = [score_job(j, d, hq, c) for j, (d, hq, c) in enumerate(near)] + extra_scores
        ahead = max(NEAR_LOOKAHEAD, len(jobs) - len(near))
        for job in jobs[:ahead]:
            job()
        vts = {}
        for j, (d, hq, c) in enumerate(near):
            if j + ahead < len(jobs):
                jobs[j + ahead]()
            if d not in vts:
                vts[d] = values(base + d)
            bias = diag_bias if hq == d else prev_bias if hq == d + 1 else None
            update(2 * hq + c, ns_ref[j], vts[d], bias, hq == d)
            if final and d == 0 and c == 1:
                finalize(hq)

    def score_into(kc, slot, cb):
        def run():
            s_refs[slot][cb] = scores(kc, cb)
        return run

    def issue_consume(kc, slot, issue_next=True, last=False):
        vt = values(kc)
        for cb in range(2 * nq):
            if issue_next:
                score_into(kc + 1, 1 - slot, cb)()
            bias = prev_bias if (last and cb < 2) else None
            update(cb, s_refs[slot][cb], vt, bias, False)
            if last and cb % 2 == 1:
                finalize(cb // 2)

    @pl.when(i == 0)
    def _():
        near_phase([], final=True)

    @pl.when(i > 0)
    def _():
        near_phase([score_into(0, 0, cb) for cb in range(2 * nq)], final=False)

        def pair(t, carry):
            issue_consume(2 * t, 0)
            issue_consume(2 * t + 1, 1)
            return carry

        lax.fori_loop(0, (nq // 2) * i - 1, pair, 0)
        issue_consume(base - 2, 0)
        issue_consume(base - 1, 1, issue_next=False, last=True)


def _rel_bucket(rel):
    n = jnp.maximum(rel, 0)
    nf = jnp.maximum(n, 1).astype(F32)
    large = MAX_EXACT + jnp.floor(jnp.log(nf / MAX_EXACT) / math.log(MAX_DISTANCE / MAX_EXACT)
                                  * (NUM_BUCKETS - MAX_EXACT)).astype(jnp.int32)
    return jnp.where(n < MAX_EXACT, n, jnp.minimum(large, NUM_BUCKETS - 1))


def _near_bias(rel_table):
    blk = ATT_BLOCK
    key = jnp.arange(blk)[:, None]
    query = jnp.arange(blk)[None, :]
    rel = jnp.stack([query - key + blk, query - key])
    bucket = _rel_bucket(rel)
    table = rel_table.astype(F32)
    table = (table - table[NUM_BUCKETS - 1]) * LOG2_E
    bias = jnp.zeros((N_HEADS,) + rel.shape, F32)
    for b in range(NUM_BUCKETS):
        bias = jnp.where(bucket[None] == b, table[b][:, None, None, None], bias)
    return jnp.where(rel[None] >= 0, bias, MASK_VALUE)


def _attention(q, k, vt, bias, lam_p, subln_g, lambda_init):
    B, S, _ = q.shape
    tq = ATT_TILE
    blk = ATT_BLOCK
    nq = tq // blk
    nchain = 2 * nq
    nnear = nq * (nq + 1)
    H = N_HEADS
    rows = V_DIM + SUM_ROWS
    vmem = (4 * S * V_DIM * 2 + 2 * 2 * blk * blk * 4 + (3 * nchain + nnear) * blk * blk * 4
            + 2 * nchain * rows * blk * 4 + 4 * tq * V_DIM * 2)
    return pl.pallas_call(
        functools.partial(_attn_kernel, lambda_init),
        grid=(B, H, S // tq),
        in_specs=[
            pl.BlockSpec((1, tq, V_DIM), lambda b, h, i: (b, i, h)),
            pl.BlockSpec((1, S, V_DIM), lambda b, h, i: (b, 0, h)),
            pl.BlockSpec((1, V_DIM, S), lambda b, h, i: (b, h, 0)),
            pl.BlockSpec((1, 2, blk, blk), lambda b, h, i: (h, 0, 0, 0)),
            pl.BlockSpec((4, HEAD_DIM), lambda b, h, i: (0, 0)),
            pl.BlockSpec((V_DIM, blk), lambda b, h, i: (0, 0)),
        ],
        out_specs=pl.BlockSpec((1, V_DIM, tq), lambda b, h, i: (b, h, i)),
        out_shape=jax.ShapeDtypeStruct((B, D_ATTN, S), BF16),
        scratch_shapes=[
            pltpu.VMEM((nchain, blk, V_DIM), BF16),
            pltpu.VMEM((nchain, 1, blk), F32),
            pltpu.VMEM((nchain, rows, blk), F32),
            pltpu.VMEM((nchain, blk, blk), F32),
            pltpu.VMEM((nchain, blk, blk), F32),
            pltpu.VMEM((nnear, blk, blk), F32),
        ],
        compiler_params=_params(("parallel", "parallel", "arbitrary"), vmem),
        name="diff_attn",
    )(q, k, vt, bias, lam_p, subln_g)


def _layer1_kernel(layer, x_ref, a_ref, wo_ref, g0_ref, b0_ref, wg_hbm, wu_hbm, wd_hbm, g1_ref, b1_ref,
                   o_ref, y_ref, xb_ref, acc_ref, h_ref, wg_ref, wu_ref, wd_ref, wide_stage, tall_stage,
                   wide_sem, tall_sem):
    _load_ffn_weights(layer, (wg_hbm, wu_hbm, wd_hbm), (wg_ref, wu_ref, wd_ref), (wide_stage, tall_stage),
                      (wide_sem, tall_sem))
    tm = x_ref.shape[1] // HALVES

    def mixer(hf, y_out):
        a_t = a_ref[0, :, hf * tm:(hf + 1) * tm]
        h = lax.dot_general(a_t, wo_ref[...], (((0,), (0,)), ((), ())), preferred_element_type=F32)
        y_out[...] = _layernorm(ALPHA * x_ref[0, hf * tm:(hf + 1) * tm, :] + h, g0_ref[...], b0_ref[...])

    _two_halves(mixer, o_ref, (wg_ref, wu_ref, wd_ref, g1_ref, b1_ref), (y_ref, xb_ref, acc_ref, h_ref))


def _layer1(layer, x, at, wo, g0, b0, wg, wu, wd, g1, b1):
    B, S, D = x.shape
    tm = HALVES * ROW_TILE
    vmem = _ffn_vmem(ROW_TILE) + D * D_ATTN * 2 + 2 * tm * D_ATTN * 2 + 6 * ROW_TILE * D * 4
    return pl.pallas_call(
        functools.partial(_layer1_kernel, layer),
        grid=(B, S // tm),
        in_specs=[
            pl.BlockSpec((1, tm, D), lambda b, i: (b, i, 0)),
            pl.BlockSpec((1, D_ATTN, tm), lambda b, i: (b, 0, i)),
            _resident((D_ATTN, D)), _resident((1, D)), _resident((1, D)),
        ] + _ffn_specs(),
        out_specs=pl.BlockSpec((1, tm, D), lambda b, i: (b, i, 0)),
        out_shape=jax.ShapeDtypeStruct((B, S, D), F32),
        scratch_shapes=_ffn_scratch(ROW_TILE),
        compiler_params=_params(("arbitrary", "arbitrary"), vmem),
        name="layer1_oproj_ffn",
    )(x, at, wo, g0, b0, wg, wu, wd, g1, b1)


def kernel(x, pool_w, pool_scale, w_qkv, w_o, lam_p, subln_g, rel_table, w_gate, w_up, w_down,
           ln_mix_g, ln_mix_b, ln_ffn_g, ln_ffn_b):
    row = lambda a: a.reshape(1, -1).astype(F32)
    ffn_w = lambda i: (w_gate.astype(F32), w_up.astype(F32), w_down.astype(F32),
                       row(ln_ffn_g[i]), row(ln_ffn_b[i]))

    x2 = _layer0(0, x, pool_w[0].astype(BF16), row(pool_scale[0]), row(ln_mix_g[0]), row(ln_mix_b[0]), *ffn_w(0))

    lambda_init = 0.8 - 0.6 * math.exp(-0.3 * 1)
    q, k, vt = _qkv(x2, w_qkv[0, :, :2 * D_ATTN].astype(BF16), w_qkv[0, :, 2 * D_ATTN:].T.astype(BF16))
    gain = jnp.broadcast_to(subln_g[0].astype(F32)[:, None], (V_DIM, ATT_BLOCK))
    at = _attention(q, k, vt, _near_bias(rel_table), lam_p[0].astype(F32), gain, lambda_init)
    return _layer1(1, x2, at, w_o[0].astype(BF16), row(ln_mix_g[1]), row(ln_mix_b[1]), *ffn_w(1))
```

```python
import functools
import math

import jax
import jax.numpy as jnp
from jax import lax
from jax.experimental import pallas as pl
from jax.experimental.pallas import tpu as pltpu

D_MODEL = 1024
DEPTH = 2
ALPHA = (2.0 * DEPTH) ** 0.25
LN_EPS = 1e-5
POOL_WINDOWS = (2, 4, 8, 16)
N_GROUPS = len(POOL_WINDOWS)
GROUP_W = D_MODEL // N_GROUPS
POOL_HALO = 16
HEAD_DIM = 64
N_HEADS = D_MODEL // (2 * HEAD_DIM)
V_DIM = 2 * HEAD_DIM
D_ATTN = N_HEADS * V_DIM
NUM_BUCKETS = 32
MAX_EXACT = NUM_BUCKETS // 2
MAX_DISTANCE = 128
D_FF = 2816
SM_SCALE = HEAD_DIM ** -0.5
LOG2_E = math.log2(math.e)

FF_CHUNK = 256
ROW_TILE = 512
HALVES = 2
OVERLAP_AFTER_CHUNK = 1
STAGE_SLOTS = 6
WIDE_STAGE_ROWS = 64
TALL_STAGE_ROWS = 128
ATT_BLOCK = 256
ATT_TILE = 1024
NEAR_LOOKAHEAD = 8
FAR_BLOCKS = 2
SUM_ROWS = 16
MASK_VALUE = -1e30
V7X_VMEM_BYTES = 64 * 1024 * 1024
VMEM_RESERVED_BYTES = 6 * 1024 * 1024
VMEM_TEMP_FACTOR = 1.5

F32 = jnp.float32
BF16 = jnp.bfloat16
NT_DIMS = (((1,), (1,)), ((), ()))


def _params(semantics, vmem_bytes):
    limit = min(int(vmem_bytes * VMEM_TEMP_FACTOR), V7X_VMEM_BYTES - VMEM_RESERVED_BYTES)
    return pltpu.CompilerParams(dimension_semantics=semantics, vmem_limit_bytes=limit)


def _resident(shape):
    return pl.BlockSpec(shape, lambda *_: (0,) * len(shape), pipeline_mode=pl.Buffered(1))


def _layernorm(z, g, b):
    mu = jnp.mean(z, axis=-1, keepdims=True)
    zc = z - mu
    var = jnp.mean(zc * zc, axis=-1, keepdims=True)
    return zc * lax.rsqrt(var + LN_EPS) * g + b


def _ffn_ln(y_ref, wg_ref, wu_ref, wd_ref, g_ref, b_ref, xb_ref, acc_ref, h_ref, overlap=None):
    xb_ref[...] = y_ref[...].astype(BF16)
    nc = wg_ref.shape[1] // FF_CHUNK

    def hidden(c):
        cols = slice(c * FF_CHUNK, (c + 1) * FF_CHUNK)
        xb = xb_ref[...]
        gate = jnp.dot(xb, wg_ref[:, cols], preferred_element_type=F32)
        up = jnp.dot(xb, wu_ref[:, cols], preferred_element_type=F32)
        h_ref[c % 2] = (gate * jax.nn.sigmoid(gate) * up).astype(BF16)

    def down(c):
        rows = slice(c * FF_CHUNK, (c + 1) * FF_CHUNK)
        y = jnp.dot(h_ref[c % 2], wd_ref[rows, :], preferred_element_type=F32)
        if c == 0:
            acc_ref[...] = y
        else:
            acc_ref[...] += y

    hidden(0)
    for c in range(nc):
        if c + 1 < nc:
            hidden(c + 1)
        down(c)
        if overlap is not None and c == OVERLAP_AFTER_CHUNK:
            overlap()
    return _layernorm(ALPHA * y_ref[...] + acc_ref[...], g_ref[...], b_ref[...])


def _two_halves(mixer, o_ref, ffn_refs, scratch):
    y_ref, xb_ref, acc_ref, h_ref = scratch
    tm = y_ref.shape[1]
    mixer(0, y_ref.at[0])
    for h in range(HALVES):
        nxt = functools.partial(mixer, h + 1, y_ref.at[h + 1]) if h + 1 < HALVES else None
        o_ref[0, h * tm:(h + 1) * tm, :] = _ffn_ln(y_ref.at[h], *ffn_refs, xb_ref.at[h], acc_ref.at[h],
                                                   h_ref.at[h], overlap=nxt)


def _fetch_bf16(src, dst_ref, stage_ref, sem_ref):
    nslot, rows = stage_ref.shape[0], stage_ref.shape[1]
    assert src.shape[0] % rows == 0
    n = src.shape[0] // rows

    def chunk(c):
        slot = c % nslot
        return pltpu.make_async_copy(src.at[pl.ds(c * rows, rows), :], stage_ref.at[slot], sem_ref.at[slot])

    for c in range(min(nslot, n)):
        chunk(c).start()
    for c in range(n):
        chunk(c).wait()
        dst_ref[c * rows:(c + 1) * rows, :] = stage_ref[c % nslot].astype(BF16)
        if c + nslot < n:
            chunk(c + nslot).start()


def _load_ffn_weights(layer, hbm_refs, vmem_refs, stage_refs, sem_refs):
    wg_hbm, wu_hbm, wd_hbm = hbm_refs
    wg_ref, wu_ref, wd_ref = vmem_refs
    wide_stage, tall_stage = stage_refs
    wide_sem, tall_sem = sem_refs

    @pl.when(jnp.logical_and(pl.program_id(0) == 0, pl.program_id(1) == 0))
    def _():
        _fetch_bf16(wg_hbm.at[layer], wg_ref, wide_stage, wide_sem)
        _fetch_bf16(wu_hbm.at[layer], wu_ref, wide_stage, wide_sem)
        _fetch_bf16(wd_hbm.at[layer], wd_ref, tall_stage, tall_sem)


def _ffn_specs():
    hbm = pl.BlockSpec(memory_space=pl.ANY)
    return [hbm, hbm, hbm, _resident((1, D_MODEL)), _resident((1, D_MODEL))]


def _ffn_scratch(tm):
    return [pltpu.VMEM((HALVES, tm, D_MODEL), F32), pltpu.VMEM((HALVES, tm, D_MODEL), BF16),
            pltpu.VMEM((HALVES, tm, D_MODEL), F32), pltpu.VMEM((HALVES, 2, tm, FF_CHUNK), BF16),
            pltpu.VMEM((D_MODEL, D_FF), BF16), pltpu.VMEM((D_MODEL, D_FF), BF16), pltpu.VMEM((D_FF, D_MODEL), BF16),
            pltpu.VMEM((STAGE_SLOTS, WIDE_STAGE_ROWS, D_FF), F32),
            pltpu.VMEM((STAGE_SLOTS, TALL_STAGE_ROWS, D_MODEL), F32),
            pltpu.SemaphoreType.DMA((STAGE_SLOTS,)), pltpu.SemaphoreType.DMA((STAGE_SLOTS,))]


def _ffn_vmem(tm):
    rows = HALVES * tm
    stage = STAGE_SLOTS * (WIDE_STAGE_ROWS * D_FF + TALL_STAGE_ROWS * D_MODEL) * 4
    return (3 * D_MODEL * D_FF * 2 + stage + 4 * rows * D_MODEL * 4 + rows * D_MODEL * 10
            + 8 * tm * FF_CHUNK * 4)


def _layer0_kernel(layer, x_ref, halo_ref, pw_ref, ps_ref, g0_ref, b0_ref, wg_hbm, wu_hbm, wd_hbm, g1_ref, b1_ref,
                   o_ref, y_ref, xb_ref, acc_ref, h_ref, wg_ref, wu_ref, wd_ref, wide_stage, tall_stage,
                   wide_sem, tall_sem):
    _load_ffn_weights(layer, (wg_hbm, wu_hbm, wd_hbm), (wg_ref, wu_ref, wd_ref), (wide_stage, tall_stage),
                      (wide_sem, tall_sem))
    s = pl.program_id(1)
    ts = x_ref.shape[1] // HALVES

    def mixer(hf, y_out):
        x = x_ref[0, hf * ts:(hf + 1) * ts, :]
        if hf == 0:
            halo = jnp.where(s > 0, halo_ref[0], 0.0)
        else:
            halo = x_ref[0, hf * ts - POOL_HALO:hf * ts, :]
        xh = jnp.concatenate([halo, x], axis=0)
        t = (s * HALVES + hf) * ts + lax.broadcasted_iota(jnp.int32, (ts, 1), 0)
        mixed = []
        for g, win in enumerate(POOL_WINDOWS):
            a = xh[:, g * GROUP_W:(g + 1) * GROUP_W]
            shift = 1
            while shift < win:
                a = a + pltpu.roll(a, shift, 0)
                shift *= 2
            inv_cnt = 1.0 / jnp.minimum(t + 1, win).astype(F32)
            d = a[POOL_HALO:] * inv_cnt - x[:, g * GROUP_W:(g + 1) * GROUP_W]
            mixed.append(jnp.dot(d.astype(BF16), pw_ref[g], preferred_element_type=F32))
        h = jnp.concatenate(mixed, axis=-1) * ps_ref[...]
        y_out[...] = _layernorm(ALPHA * x + h, g0_ref[...], b0_ref[...])

    _two_halves(mixer, o_ref, (wg_ref, wu_ref, wd_ref, g1_ref, b1_ref), (y_ref, xb_ref, acc_ref, h_ref))


def _layer0(layer, x, pw, ps, g0, b0, wg, wu, wd, g1, b1):
    B, S, D = x.shape
    ts = HALVES * ROW_TILE
    halo_blocks = ts // POOL_HALO
    vmem = _ffn_vmem(ROW_TILE) + 10 * ROW_TILE * D * 4
    return pl.pallas_call(
        functools.partial(_layer0_kernel, layer),
        grid=(B, S // ts),
        in_specs=[
            pl.BlockSpec((1, ts, D), lambda b_, s_: (b_, s_, 0)),
            pl.BlockSpec((1, POOL_HALO, D), lambda b_, s_: (b_, jnp.maximum(s_ * halo_blocks - 1, 0), 0)),
            _resident((N_GROUPS, GROUP_W, GROUP_W)), _resident((1, D)), _resident((1, D)), _resident((1, D)),
        ] + _ffn_specs(),
        out_specs=pl.BlockSpec((1, ts, D), lambda b_, s_: (b_, s_, 0)),
        out_shape=jax.ShapeDtypeStruct((B, S, D), F32),
        scratch_shapes=_ffn_scratch(ROW_TILE),
        compiler_params=_params(("arbitrary", "arbitrary"), vmem),
        name="layer0_pool_ffn",
    )(x, x, pw, ps, g0, b0, wg, wu, wd, g1, b1)


def _qkv_kernel(x_ref, wqk_ref, wvt_ref, q_ref, k_ref, vt_ref):
    xb = x_ref[0].astype(BF16)
    q = jnp.dot(xb, wqk_ref[:, :D_ATTN], preferred_element_type=F32) * (SM_SCALE * LOG2_E)
    q_ref[0] = q.astype(BF16)
    k_ref[0] = jnp.dot(xb, wqk_ref[:, D_ATTN:], preferred_element_type=F32).astype(BF16)
    vt = lax.dot_general(wvt_ref[...], xb, NT_DIMS, preferred_element_type=F32)
    vt_ref[0] = vt.astype(BF16)


def _qkv(x, wqk, wvt):
    B, S, D = x.shape
    tm = ROW_TILE
    vmem = 3 * D * D_ATTN * 2 + 2 * tm * D * 4 + 6 * tm * D_ATTN * 2 + tm * D * 2 + 3 * tm * D_ATTN * 4
    return pl.pallas_call(
        _qkv_kernel,
        grid=(B, S // tm),
        in_specs=[
            pl.BlockSpec((1, tm, D), lambda b, i: (b, i, 0)),
            _resident((D, 2 * D_ATTN)), _resident((D_ATTN, D)),
        ],
        out_specs=[
            pl.BlockSpec((1, tm, D_ATTN), lambda b, i: (b, i, 0)),
            pl.BlockSpec((1, tm, D_ATTN), lambda b, i: (b, i, 0)),
            pl.BlockSpec((1, D_ATTN, tm), lambda b, i: (b, 0, i)),
        ],
        out_shape=[
            jax.ShapeDtypeStruct((B, S, D_ATTN), BF16),
            jax.ShapeDtypeStruct((B, S, D_ATTN), BF16),
            jax.ShapeDtypeStruct((B, D_ATTN, S), BF16),
        ],
        compiler_params=_params(("parallel", "parallel"), vmem),
        name="qkv_proj",
    )(x, wqk, wvt)


def _attn_kernel(lambda_init, q_ref, k_ref, vt_ref, bias_ref, lam_ref, sg_ref, o_ref,
                 qs_ref, m_ref, acc_ref, s0_ref, s1_ref, ns_ref):
    i = pl.program_id(2)
    blk = ATT_BLOCK
    nq = q_ref.shape[1] // blk
    half = blk - MAX_DISTANCE
    diag_bias = bias_ref[0, 1]
    prev_corner = bias_ref[0, 0, blk - half:, :half]

    lane = lax.broadcasted_iota(jnp.int32, (blk, V_DIM), 1)
    for hq in range(nq):
        q = q_ref[0, hq * blk:(hq + 1) * blk, :]
        zero = jnp.zeros_like(q)
        qs_ref[2 * hq] = jnp.where(lane < HEAD_DIM, q, zero)
        qs_ref[2 * hq + 1] = jnp.where(lane >= HEAD_DIM, q, zero)

    def scores(kc, cb, nblk=1):
        rows = nblk * blk
        start = pl.multiple_of(kc * rows, rows)
        k = k_ref[0, pl.ds(start, rows), :]
        return lax.dot_general(k, qs_ref[cb], NT_DIMS, preferred_element_type=F32)

    def values(kc, nblk=1):
        rows = nblk * blk
        start = pl.multiple_of(kc * rows, rows)
        vt = vt_ref[0, :, pl.ds(start, rows)]
        return jnp.concatenate([vt, jnp.ones((SUM_ROWS, rows), BF16)], axis=0)

    def update(cb, s, vt, bias, first):
        if bias == "diag":
            s = s + diag_bias
        elif bias == "prev":
            top = s.shape[0] - half
            bottom = jnp.concatenate([s[top:, :half] + prev_corner, s[top:, half:]], axis=1)
            s = jnp.concatenate([s[:top], bottom], axis=0)
        m_new = jnp.max(s, axis=0, keepdims=True)
        if not first:
            m_prev = m_ref[cb]
            m_new = jnp.maximum(m_prev, m_new)
        p = jnp.exp2(s - m_new)
        pv = jnp.dot(vt, p.astype(BF16), preferred_element_type=F32)
        if first:
            acc_ref[cb] = pv
        else:
            acc_ref[cb] = acc_ref[cb] * jnp.exp2(m_prev - m_new) + pv
        m_ref[cb] = m_new

    base = nq * i
    near = [(d, hq, c) for d in range(nq - 1, -1, -1) for hq in range(d, nq) for c in range(2)]

    s_refs = (s0_ref, s1_ref)

    lp = lam_ref[...]
    lam = (jnp.exp(jnp.sum(lp[0:1] * lp[1:2], axis=-1, keepdims=True))
           - jnp.exp(jnp.sum(lp[2:3] * lp[3:4], axis=-1, keepdims=True)) + lambda_init)

    def finalize(hq):
        a0 = acc_ref[2 * hq]
        a1 = acc_ref[2 * hq + 1]
        ot = a0[:V_DIM] / a0[V_DIM:V_DIM + 1] - lam * (a1[:V_DIM] / a1[V_DIM:V_DIM + 1])
        ot = ot * lax.rsqrt(jnp.mean(ot * ot, axis=0, keepdims=True) + LN_EPS)
        gain = sg_ref[...] * (1.0 - lambda_init)
        o_ref[0, :, hq * blk:(hq + 1) * blk] = (ot * gain).astype(BF16)

    def near_phase(extra_scores, final):
        def score_job(j, d, hq, c):
            def run():
                ns_ref[j] = scores(base + d, 2 * hq + c)
            return run

        jobs = [score_job(j, d, hq, c) for j, (d, hq, c) in enumerate(near)] + extra_scores
        ahead = max(NEAR_LOOKAHEAD, len(jobs) - len(near))
        for job in jobs[:ahead]:
            job()
        vts = {}
        for j, (d, hq, c) in enumerate(near):
            if j + ahead < len(jobs):
                jobs[j + ahead]()
            if d not in vts:
                vts[d] = values(base + d)
            bias = "diag" if hq == d else "prev" if hq == d + 1 else None
            update(2 * hq + c, ns_ref[j], vts[d], bias, hq == d)
            if final and d == 0 and c == 1:
                finalize(hq)

    def score_into(kc, slot, cb):
        def run():
            s_refs[slot][cb] = scores(kc, cb, FAR_BLOCKS)
        return run

    def issue_consume(kc, slot, issue_next=True, last=False):
        vt = values(kc, FAR_BLOCKS)
        for cb in range(2 * nq):
            if issue_next:
                score_into(kc + 1, 1 - slot, cb)()
            bias = "prev" if (last and cb < 2) else None
            update(cb, s_refs[slot][cb], vt, bias, False)
            if last and cb % 2 == 1:
                finalize(cb // 2)

    @pl.when(i == 0)
    def _():
        near_phase([], final=True)

    assert nq % (2 * FAR_BLOCKS) == 0, "the pipeline pairs chunks and needs an even count per tile"
    pairs = nq // (2 * FAR_BLOCKS) * i

    @pl.when(i > 0)
    def _():
        near_phase([score_into(0, 0, cb) for cb in range(2 * nq)], final=False)

        def pair(t, carry):
            issue_consume(2 * t, 0)
            issue_consume(2 * t + 1, 1)
            return carry

        lax.fori_loop(0, pairs - 1, pair, 0)
        issue_consume(2 * pairs - 2, 0)
        issue_consume(2 * pairs - 1, 1, issue_next=False, last=True)


def _rel_bucket(rel):
    n = jnp.maximum(rel, 0)
    nf = jnp.maximum(n, 1).astype(F32)
    large = MAX_EXACT + jnp.floor(jnp.log(nf / MAX_EXACT) / math.log(MAX_DISTANCE / MAX_EXACT)
                                  * (NUM_BUCKETS - MAX_EXACT)).astype(jnp.int32)
    return jnp.where(n < MAX_EXACT, n, jnp.minimum(large, NUM_BUCKETS - 1))


def _near_bias(rel_table):
    blk = ATT_BLOCK
    key = jnp.arange(blk)[:, None]
    query = jnp.arange(blk)[None, :]
    rel = jnp.stack([query - key + blk, query - key])
    bucket = _rel_bucket(rel)
    table = rel_table.astype(F32)
    table = (table - table[NUM_BUCKETS - 1]) * LOG2_E
    bias = jnp.zeros((N_HEADS,) + rel.shape, F32)
    for b in range(NUM_BUCKETS):
        bias = jnp.where(bucket[None] == b, table[b][:, None, None, None], bias)
    return jnp.where(rel[None] >= 0, bias, MASK_VALUE)


def _attention(q, k, vt, bias, lam_p, subln_g, lambda_init):
    B, S, _ = q.shape
    tq = ATT_TILE
    blk = ATT_BLOCK
    nq = tq // blk
    nchain = 2 * nq
    nnear = nq * (nq + 1)
    H = N_HEADS
    rows = V_DIM + SUM_ROWS
    vmem = (4 * S * V_DIM * 2 + 2 * 2 * blk * blk * 4 + ((2 * FAR_BLOCKS + 1) * nchain + nnear) * blk * blk * 4
            + 2 * nchain * rows * blk * 4 + 4 * tq * V_DIM * 2)
    return pl.pallas_call(
        functools.partial(_attn_kernel, lambda_init),
        grid=(B, H, S // tq),
        in_specs=[
            pl.BlockSpec((1, tq, V_DIM), lambda b, h, i: (b, i, h)),
            pl.BlockSpec((1, S, V_DIM), lambda b, h, i: (b, 0, h)),
            pl.BlockSpec((1, V_DIM, S), lambda b, h, i: (b, h, 0)),
            pl.BlockSpec((1, 2, blk, blk), lambda b, h, i: (h, 0, 0, 0)),
            pl.BlockSpec((4, HEAD_DIM), lambda b, h, i: (0, 0)),
            pl.BlockSpec((V_DIM, blk), lambda b, h, i: (0, 0)),
        ],
        out_specs=pl.BlockSpec((1, V_DIM, tq), lambda b, h, i: (b, h, i)),
        out_shape=jax.ShapeDtypeStruct((B, D_ATTN, S), BF16),
        scratch_shapes=[
            pltpu.VMEM((nchain, blk, V_DIM), BF16),
            pltpu.VMEM((nchain, 1, blk), F32),
            pltpu.VMEM((nchain, rows, blk), F32),
            pltpu.VMEM((nchain, FAR_BLOCKS * blk, blk), F32),
            pltpu.VMEM((nchain, FAR_BLOCKS * blk, blk), F32),
            pltpu.VMEM((nnear, blk, blk), F32),
        ],
        compiler_params=_params(("parallel", "parallel", "arbitrary"), vmem),
        name="diff_attn",
    )(q, k, vt, bias, lam_p, subln_g)


def _layer1_kernel(layer, x_ref, a_ref, wo_ref, g0_ref, b0_ref, wg_hbm, wu_hbm, wd_hbm, g1_ref, b1_ref,
                   o_ref, y_ref, xb_ref, acc_ref, h_ref, wg_ref, wu_ref, wd_ref, wide_stage, tall_stage,
                   wide_sem, tall_sem):
    _load_ffn_weights(layer, (wg_hbm, wu_hbm, wd_hbm), (wg_ref, wu_ref, wd_ref), (wide_stage, tall_stage),
                      (wide_sem, tall_sem))
    tm = x_ref.shape[1] // HALVES

    def mixer(hf, y_out):
        a_t = a_ref[0, :, hf * tm:(hf + 1) * tm]
        h = lax.dot_general(a_t, wo_ref[...], (((0,), (0,)), ((), ())), preferred_element_type=F32)
        y_out[...] = _layernorm(ALPHA * x_ref[0, hf * tm:(hf + 1) * tm, :] + h, g0_ref[...], b0_ref[...])

    _two_halves(mixer, o_ref, (wg_ref, wu_ref, wd_ref, g1_ref, b1_ref), (y_ref, xb_ref, acc_ref, h_ref))


def _layer1(layer, x, at, wo, g0, b0, wg, wu, wd, g1, b1):
    B, S, D = x.shape
    tm = HALVES * ROW_TILE
    vmem = _ffn_vmem(ROW_TILE) + D * D_ATTN * 2 + 2 * tm * D_ATTN * 2 + 6 * ROW_TILE * D * 4
    return pl.pallas_call(
        functools.partial(_layer1_kernel, layer),
        grid=(B, S // tm),
        in_specs=[
            pl.BlockSpec((1, tm, D), lambda b, i: (b, i, 0)),
            pl.BlockSpec((1, D_ATTN, tm), lambda b, i: (b, 0, i)),
            _resident((D_ATTN, D)), _resident((1, D)), _resident((1, D)),
        ] + _ffn_specs(),
        out_specs=pl.BlockSpec((1, tm, D), lambda b, i: (b, i, 0)),
        out_shape=jax.ShapeDtypeStruct((B, S, D), F32),
        scratch_shapes=_ffn_scratch(ROW_TILE),
        compiler_params=_params(("arbitrary", "arbitrary"), vmem),
        name="layer1_oproj_ffn",
    )(x, at, wo, g0, b0, wg, wu, wd, g1, b1)


def kernel(x, pool_w, pool_scale, w_qkv, w_o, lam_p, subln_g, rel_table, w_gate, w_up, w_down,
           ln_mix_g, ln_mix_b, ln_ffn_g, ln_ffn_b):
    row = lambda a: a.reshape(1, -1).astype(F32)
    ffn_w = lambda i: (w_gate.astype(F32), w_up.astype(F32), w_down.astype(F32),
                       row(ln_ffn_g[i]), row(ln_ffn_b[i]))

    x2 = _layer0(0, x, pool_w[0].astype(BF16), row(pool_scale[0]), row(ln_mix_g[0]), row(ln_mix_b[0]), *ffn_w(0))

    lambda_init = 0.8 - 0.6 * math.exp(-0.3 * 1)
    q, k, vt = _qkv(x2, w_qkv[0, :, :2 * D_ATTN].astype(BF16), w_qkv[0, :, 2 * D_ATTN:].T.astype(BF16))
    gain = jnp.broadcast_to(subln_g[0].astype(F32)[:, None], (V_DIM, ATT_BLOCK))
    at = _attention(q, k, vt, _near_bias(rel_table), lam_p[0].astype(F32), gain, lambda_init)
    return _layer1(1, x2, at, w_o[0].astype(BF16), row(ln_mix_g[1]), row(ln_mix_b[1]), *ffn_w(1))
```

```python
import functools
import math

import jax
import jax.numpy as jnp
from jax import lax
from jax.experimental import pallas as pl
from jax.experimental.pallas import tpu as pltpu

D_MODEL = 1024
DEPTH = 2
ALPHA = (2.0 * DEPTH) ** 0.25
LN_EPS = 1e-5
POOL_WINDOWS = (2, 4, 8, 16)
N_GROUPS = len(POOL_WINDOWS)
GROUP_W = D_MODEL // N_GROUPS
POOL_HALO = 16
HEAD_DIM = 64
N_HEADS = D_MODEL // (2 * HEAD_DIM)
V_DIM = 2 * HEAD_DIM
D_ATTN = N_HEADS * V_DIM
NUM_BUCKETS = 32
MAX_EXACT = NUM_BUCKETS // 2
MAX_DISTANCE = 128
D_FF = 2816
SM_SCALE = HEAD_DIM ** -0.5
LOG2_E = math.log2(math.e)

FF_CHUNK = 256
ROW_TILE = 512
HALVES = 2
OVERLAP_AFTER_CHUNK = 1
STAGE_SLOTS = 6
WIDE_STAGE_ROWS = 64
TALL_STAGE_ROWS = 128
ATT_BLOCK = 256
ATT_TILE = 1024
NEAR_LOOKAHEAD = 8
FAR_BLOCKS = 2
SUM_ROWS = 16
MASK_VALUE = -1e30
V7X_VMEM_BYTES = 64 * 1024 * 1024
VMEM_RESERVED_BYTES = 6 * 1024 * 1024
VMEM_TEMP_FACTOR = 1.5

F32 = jnp.float32
BF16 = jnp.bfloat16
NT_DIMS = (((1,), (1,)), ((), ()))


def _params(semantics, vmem_bytes):
    limit = min(int(vmem_bytes * VMEM_TEMP_FACTOR), V7X_VMEM_BYTES - VMEM_RESERVED_BYTES)
    return pltpu.CompilerParams(dimension_semantics=semantics, vmem_limit_bytes=limit)


def _resident(shape):
    return pl.BlockSpec(shape, lambda *_: (0,) * len(shape), pipeline_mode=pl.Buffered(1))


def _layernorm(z, g, b):
    mu = jnp.mean(z, axis=-1, keepdims=True)
    zc = z - mu
    var = jnp.mean(zc * zc, axis=-1, keepdims=True)
    return zc * lax.rsqrt(var + LN_EPS) * g + b


def _ffn_ln(y_ref, wg_ref, wu_ref, wd_ref, g_ref, b_ref, xb_ref, acc_ref, h_ref, overlap=None):
    xb_ref[...] = y_ref[...].astype(BF16)
    nc = wg_ref.shape[1] // FF_CHUNK

    def hidden(c):
        cols = slice(c * FF_CHUNK, (c + 1) * FF_CHUNK)
        xb = xb_ref[...]
        gate = jnp.dot(xb, wg_ref[:, cols], preferred_element_type=F32)
        up = jnp.dot(xb, wu_ref[:, cols], preferred_element_type=F32)
        h_ref[c % 2] = (gate * jax.nn.sigmoid(gate) * up).astype(BF16)

    def down(c):
        rows = slice(c * FF_CHUNK, (c + 1) * FF_CHUNK)
        y = jnp.dot(h_ref[c % 2], wd_ref[rows, :], preferred_element_type=F32)
        if c == 0:
            acc_ref[...] = y
        else:
            acc_ref[...] += y

    hidden(0)
    for c in range(nc):
        if c + 1 < nc:
            hidden(c + 1)
        down(c)
        if overlap is not None and c == OVERLAP_AFTER_CHUNK:
            overlap()
    return _layernorm(ALPHA * y_ref[...] + acc_ref[...], g_ref[...], b_ref[...])


def _two_halves(mixer, o_ref, ffn_refs, scratch):
    y_ref, xb_ref, acc_ref, h_ref = scratch
    tm = y_ref.shape[1]
    mixer(0, y_ref.at[0])
    for h in range(HALVES):
        nxt = functools.partial(mixer, h + 1, y_ref.at[h + 1]) if h + 1 < HALVES else None
        o_ref[0, h * tm:(h + 1) * tm, :] = _ffn_ln(y_ref.at[h], *ffn_refs, xb_ref.at[h], acc_ref.at[h],
                                                   h_ref.at[h], overlap=nxt)


def _fetch_bf16(src, dst_ref, stage_ref, sem_ref):
    nslot, rows = stage_ref.shape[0], stage_ref.shape[1]
    assert src.shape[0] % rows == 0
    n = src.shape[0] // rows

    def chunk(c):
        slot = c % nslot
        return pltpu.make_async_copy(src.at[pl.ds(c * rows, rows), :], stage_ref.at[slot], sem_ref.at[slot])

    for c in range(min(nslot, n)):
        chunk(c).start()
    for c in range(n):
        chunk(c).wait()
        dst_ref[c * rows:(c + 1) * rows, :] = stage_ref[c % nslot].astype(BF16)
        if c + nslot < n:
            chunk(c + nslot).start()


def _load_ffn_weights(layer, hbm_refs, vmem_refs, stage_refs, sem_refs):
    wg_hbm, wu_hbm, wd_hbm = hbm_refs
    wg_ref, wu_ref, wd_ref = vmem_refs
    wide_stage, tall_stage = stage_refs
    wide_sem, tall_sem = sem_refs

    @pl.when(jnp.logical_and(pl.program_id(0) == 0, pl.program_id(1) == 0))
    def _():
        _fetch_bf16(wg_hbm.at[layer], wg_ref, wide_stage, wide_sem)
        _fetch_bf16(wu_hbm.at[layer], wu_ref, wide_stage, wide_sem)
        _fetch_bf16(wd_hbm.at[layer], wd_ref, tall_stage, tall_sem)


def _ffn_specs():
    hbm = pl.BlockSpec(memory_space=pl.ANY)
    return [hbm, hbm, hbm, _resident((1, D_MODEL)), _resident((1, D_MODEL))]


def _ffn_scratch(tm):
    return [pltpu.VMEM((HALVES, tm, D_MODEL), F32), pltpu.VMEM((HALVES, tm, D_MODEL), BF16),
            pltpu.VMEM((HALVES, tm, D_MODEL), F32), pltpu.VMEM((HALVES, 2, tm, FF_CHUNK), BF16),
            pltpu.VMEM((D_MODEL, D_FF), BF16), pltpu.VMEM((D_MODEL, D_FF), BF16), pltpu.VMEM((D_FF, D_MODEL), BF16),
            pltpu.VMEM((STAGE_SLOTS, WIDE_STAGE_ROWS, D_FF), F32),
            pltpu.VMEM((STAGE_SLOTS, TALL_STAGE_ROWS, D_MODEL), F32),
            pltpu.SemaphoreType.DMA((STAGE_SLOTS,)), pltpu.SemaphoreType.DMA((STAGE_SLOTS,))]


def _ffn_vmem(tm):
    rows = HALVES * tm
    stage = STAGE_SLOTS * (WIDE_STAGE_ROWS * D_FF + TALL_STAGE_ROWS * D_MODEL) * 4
    return (3 * D_MODEL * D_FF * 2 + stage + 4 * rows * D_MODEL * 4 + rows * D_MODEL * 10
            + 8 * tm * FF_CHUNK * 4)


def _layer0_kernel(layer, x_ref, halo_ref, pw_ref, ps_ref, g0_ref, b0_ref, wg_hbm, wu_hbm, wd_hbm, g1_ref, b1_ref,
                   o_ref, y_ref, xb_ref, acc_ref, h_ref, wg_ref, wu_ref, wd_ref, wide_stage, tall_stage,
                   wide_sem, tall_sem):
    _load_ffn_weights(layer, (wg_hbm, wu_hbm, wd_hbm), (wg_ref, wu_ref, wd_ref), (wide_stage, tall_stage),
                      (wide_sem, tall_sem))
    s = pl.program_id(1)
    ts = x_ref.shape[1] // HALVES

    def mixer(hf, y_out):
        x = x_ref[0, hf * ts:(hf + 1) * ts, :]
        if hf == 0:
            halo = jnp.where(s > 0, halo_ref[0], 0.0)
        else:
            halo = x_ref[0, hf * ts - POOL_HALO:hf * ts, :]
        xh = jnp.concatenate([halo, x], axis=0)
        t = (s * HALVES + hf) * ts + lax.broadcasted_iota(jnp.int32, (ts, 1), 0)
        mixed = []
        for g, win in enumerate(POOL_WINDOWS):
            a = xh[:, g * GROUP_W:(g + 1) * GROUP_W]
            shift = 1
            while shift < win:
                a = a + pltpu.roll(a, shift, 0)
                shift *= 2
            inv_cnt = 1.0 / jnp.minimum(t + 1, win).astype(F32)
            d = a[POOL_HALO:] * inv_cnt - x[:, g * GROUP_W:(g + 1) * GROUP_W]
            mixed.append(jnp.dot(d.astype(BF16), pw_ref[g], preferred_element_type=F32))
        h = jnp.concatenate(mixed, axis=-1) * ps_ref[...]
        y_out[...] = _layernorm(ALPHA * x + h, g0_ref[...], b0_ref[...])

    _two_halves(mixer, o_ref, (wg_ref, wu_ref, wd_ref, g1_ref, b1_ref), (y_ref, xb_ref, acc_ref, h_ref))


def _layer0(layer, x, pw, ps, g0, b0, wg, wu, wd, g1, b1):
    B, S, D = x.shape
    ts = HALVES * ROW_TILE
    halo_blocks = ts // POOL_HALO
    vmem = _ffn_vmem(ROW_TILE) + 10 * ROW_TILE * D * 4
    return pl.pallas_call(
        functools.partial(_layer0_kernel, layer),
        grid=(B, S // ts),
        in_specs=[
            pl.BlockSpec((1, ts, D), lambda b_, s_: (b_, s_, 0)),
            pl.BlockSpec((1, POOL_HALO, D), lambda b_, s_: (b_, jnp.maximum(s_ * halo_blocks - 1, 0), 0)),
            _resident((N_GROUPS, GROUP_W, GROUP_W)), _resident((1, D)), _resident((1, D)), _resident((1, D)),
        ] + _ffn_specs(),
        out_specs=pl.BlockSpec((1, ts, D), lambda b_, s_: (b_, s_, 0)),
        out_shape=jax.ShapeDtypeStruct((B, S, D), F32),
        scratch_shapes=_ffn_scratch(ROW_TILE),
        compiler_params=_params(("arbitrary", "arbitrary"), vmem),
        name="layer0_pool_ffn",
    )(x, x, pw, ps, g0, b0, wg, wu, wd, g1, b1)


def _qkv_kernel(layer, x_ref, w_hbm, q_ref, k_ref, vt_ref, w_ref, stage_ref, sem_ref):
    @pl.when(jnp.logical_and(pl.program_id(0) == 0, pl.program_id(1) == 0))
    def _():
        _fetch_bf16(w_hbm.at[layer], w_ref, stage_ref, sem_ref)

    xb = x_ref[0].astype(BF16)
    q = jnp.dot(xb, w_ref[:, :D_ATTN], preferred_element_type=F32) * (SM_SCALE * LOG2_E)
    q_ref[0] = q.astype(BF16)
    k_ref[0] = jnp.dot(xb, w_ref[:, D_ATTN:2 * D_ATTN], preferred_element_type=F32).astype(BF16)
    vt = lax.dot_general(w_ref[:, 2 * D_ATTN:], xb, (((0,), (1,)), ((), ())), preferred_element_type=F32)
    vt_ref[0] = vt.astype(BF16)


def _qkv(layer, x, w_qkv):
    B, S, D = x.shape
    tm = ROW_TILE
    stage = STAGE_SLOTS * WIDE_STAGE_ROWS * 3 * D_ATTN * 4
    vmem = (3 * D * D_ATTN * 2 + stage + 2 * tm * D * 4 + 6 * tm * D_ATTN * 2 + tm * D * 2
            + 3 * tm * D_ATTN * 4)
    return pl.pallas_call(
        functools.partial(_qkv_kernel, layer),
        grid=(B, S // tm),
        in_specs=[
            pl.BlockSpec((1, tm, D), lambda b, i: (b, i, 0)),
            pl.BlockSpec(memory_space=pl.ANY),
        ],
        out_specs=[
            pl.BlockSpec((1, tm, D_ATTN), lambda b, i: (b, i, 0)),
            pl.BlockSpec((1, tm, D_ATTN), lambda b, i: (b, i, 0)),
            pl.BlockSpec((1, D_ATTN, tm), lambda b, i: (b, 0, i)),
        ],
        out_shape=[
            jax.ShapeDtypeStruct((B, S, D_ATTN), BF16),
            jax.ShapeDtypeStruct((B, S, D_ATTN), BF16),
            jax.ShapeDtypeStruct((B, D_ATTN, S), BF16),
        ],
        scratch_shapes=[pltpu.VMEM((D, 3 * D_ATTN), BF16),
                        pltpu.VMEM((STAGE_SLOTS, WIDE_STAGE_ROWS, 3 * D_ATTN), F32),
                        pltpu.SemaphoreType.DMA((STAGE_SLOTS,))],
        compiler_params=_params(("arbitrary", "arbitrary"), vmem),
        name="qkv_proj",
    )(x, w_qkv)


def _attn_kernel(lambda_init, q_ref, k_ref, vt_ref, bias_ref, lam_ref, sg_ref, o_ref,
                 qs_ref, m_ref, acc_ref, s0_ref, s1_ref, ns_ref):
    i = pl.program_id(2)
    blk = ATT_BLOCK
    nq = q_ref.shape[1] // blk
    half = blk - MAX_DISTANCE
    diag_bias = bias_ref[0, 1]
    prev_corner = bias_ref[0, 0, blk - half:, :half]

    lane = lax.broadcasted_iota(jnp.int32, (blk, V_DIM), 1)
    for hq in range(nq):
        q = q_ref[0, hq * blk:(hq + 1) * blk, :]
        zero = jnp.zeros_like(q)
        qs_ref[2 * hq] = jnp.where(lane < HEAD_DIM, q, zero)
        qs_ref[2 * hq + 1] = jnp.where(lane >= HEAD_DIM, q, zero)

    def scores(kc, cb, nblk=1):
        rows = nblk * blk
        start = pl.multiple_of(kc * rows, rows)
        k = k_ref[0, pl.ds(start, rows), :]
        return lax.dot_general(k, qs_ref[cb], NT_DIMS, preferred_element_type=F32)

    def values(kc, nblk=1):
        rows = nblk * blk
        start = pl.multiple_of(kc * rows, rows)
        vt = vt_ref[0, :, pl.ds(start, rows)]
        return jnp.concatenate([vt, jnp.ones((SUM_ROWS, rows), BF16)], axis=0)

    def update(cb, s, vt, bias, first):
        if bias == "diag":
            s = s + diag_bias
        elif bias == "prev":
            top = s.shape[0] - half
            bottom = jnp.concatenate([s[top:, :half] + prev_corner, s[top:, half:]], axis=1)
            s = jnp.concatenate([s[:top], bottom], axis=0)
        m_new = jnp.max(s, axis=0, keepdims=True)
        if not first:
            m_prev = m_ref[cb]
            m_new = jnp.maximum(m_prev, m_new)
        p = jnp.exp2(s - m_new)
        pv = jnp.dot(vt, p.astype(BF16), preferred_element_type=F32)
        if first:
            acc_ref[cb] = pv
        else:
            acc_ref[cb] = acc_ref[cb] * jnp.exp2(m_prev - m_new) + pv
        m_ref[cb] = m_new

    base = nq * i
    near = [(d, hq, c) for d in range(nq - 1, -1, -1) for hq in range(d, nq) for c in range(2)]

    s_refs = (s0_ref, s1_ref)

    lp = lam_ref[...]
    lam = (jnp.exp(jnp.sum(lp[0:1] * lp[1:2], axis=-1, keepdims=True))
           - jnp.exp(jnp.sum(lp[2:3] * lp[3:4], axis=-1, keepdims=True)) + lambda_init)

    def finalize(hq):
        a0 = acc_ref[2 * hq]
        a1 = acc_ref[2 * hq + 1]
        ot = a0[:V_DIM] / a0[V_DIM:V_DIM + 1] - lam * (a1[:V_DIM] / a1[V_DIM:V_DIM + 1])
        ot = ot * lax.rsqrt(jnp.mean(ot * ot, axis=0, keepdims=True) + LN_EPS)
        gain = sg_ref[...] * (1.0 - lambda_init)
        o_ref[0, :, hq * blk:(hq + 1) * blk] = (ot * gain).astype(BF16)

    def near_phase(extra_scores, final):
        def score_job(j, d, hq, c):
            def run():
                ns_ref[j] = scores(base + d, 2 * hq + c)
            return run

        jobs = [score_job(j, d, hq, c) for j, (d, hq, c) in enumerate(near)] + extra_scores
        ahead = max(NEAR_LOOKAHEAD, len(jobs) - len(near))
        for job in jobs[:ahead]:
            job()
        vts = {}
        for j, (d, hq, c) in enumerate(near):
            if j + ahead < len(jobs):
                jobs[j + ahead]()
            if d not in vts:
                vts[d] = values(base + d)
            bias = "diag" if hq == d else "prev" if hq == d + 1 else None
            update(2 * hq + c, ns_ref[j], vts[d], bias, hq == d)
            if final and d == 0 and c == 1:
                finalize(hq)

    def score_into(kc, slot, cb):
        def run():
            s_refs[slot][cb] = scores(kc, cb, FAR_BLOCKS)
        return run

    def issue_consume(kc, slot, issue_next=True, last=False):
        vt = values(kc, FAR_BLOCKS)
        for cb in range(2 * nq):
            if issue_next:
                score_into(kc + 1, 1 - slot, cb)()
            bias = "prev" if (last and cb < 2) else None
            update(cb, s_refs[slot][cb], vt, bias, False)
            if last and cb % 2 == 1:
                finalize(cb // 2)

    @pl.when(i == 0)
    def _():
        near_phase([], final=True)

    assert nq % (2 * FAR_BLOCKS) == 0, "the pipeline pairs chunks and needs an even count per tile"
    pairs = nq // (2 * FAR_BLOCKS) * i

    @pl.when(i > 0)
    def _():
        near_phase([score_into(0, 0, cb) for cb in range(2 * nq)], final=False)

        def pair(t, carry):
            issue_consume(2 * t, 0)
            issue_consume(2 * t + 1, 1)
            return carry

        lax.fori_loop(0, pairs - 1, pair, 0)
        issue_consume(2 * pairs - 2, 0)
        issue_consume(2 * pairs - 1, 1, issue_next=False, last=True)


def _rel_bucket(rel):
    n = jnp.maximum(rel, 0)
    nf = jnp.maximum(n, 1).astype(F32)
    large = MAX_EXACT + jnp.floor(jnp.log(nf / MAX_EXACT) / math.log(MAX_DISTANCE / MAX_EXACT)
                                  * (NUM_BUCKETS - MAX_EXACT)).astype(jnp.int32)
    return jnp.where(n < MAX_EXACT, n, jnp.minimum(large, NUM_BUCKETS - 1))


def _near_bias(rel_table):
    blk = ATT_BLOCK
    key = jnp.arange(blk)[:, None]
    query = jnp.arange(blk)[None, :]
    rel = jnp.stack([query - key + blk, query - key])
    bucket = _rel_bucket(rel)
    table = rel_table.astype(F32)
    table = (table - table[NUM_BUCKETS - 1]) * LOG2_E
    bias = jnp.zeros((N_HEADS,) + rel.shape, F32)
    for b in range(NUM_BUCKETS):
        bias = jnp.where(bucket[None] == b, table[b][:, None, None, None], bias)
    return jnp.where(rel[None] >= 0, bias, MASK_VALUE)


def _attention(q, k, vt, bias, lam_p, subln_g, lambda_init):
    B, S, _ = q.shape
    tq = ATT_TILE
    blk = ATT_BLOCK
    nq = tq // blk
    nchain = 2 * nq
    nnear = nq * (nq + 1)
    H = N_HEADS
    rows = V_DIM + SUM_ROWS
    vmem = (4 * S * V_DIM * 2 + 2 * 2 * blk * blk * 4 + ((2 * FAR_BLOCKS + 1) * nchain + nnear) * blk * blk * 4
            + 2 * nchain * rows * blk * 4 + 4 * tq * V_DIM * 2)
    return pl.pallas_call(
        functools.partial(_attn_kernel, lambda_init),
        grid=(B, H, S // tq),
        in_specs=[
            pl.BlockSpec((1, tq, V_DIM), lambda b, h, i: (b, i, h)),
            pl.BlockSpec((1, S, V_DIM), lambda b, h, i: (b, 0, h)),
            pl.BlockSpec((1, V_DIM, S), lambda b, h, i: (b, h, 0)),
            pl.BlockSpec((1, 2, blk, blk), lambda b, h, i: (h, 0, 0, 0)),
            pl.BlockSpec((4, HEAD_DIM), lambda b, h, i: (0, 0)),
            pl.BlockSpec((V_DIM, blk), lambda b, h, i: (0, 0)),
        ],
        out_specs=pl.BlockSpec((1, V_DIM, tq), lambda b, h, i: (b, h, i)),
        out_shape=jax.ShapeDtypeStruct((B, D_ATTN, S), BF16),
        scratch_shapes=[
            pltpu.VMEM((nchain, blk, V_DIM), BF16),
            pltpu.VMEM((nchain, 1, blk), F32),
            pltpu.VMEM((nchain, rows, blk), F32),
            pltpu.VMEM((nchain, FAR_BLOCKS * blk, blk), F32),
            pltpu.VMEM((nchain, FAR_BLOCKS * blk, blk), F32),
            pltpu.VMEM((nnear, blk, blk), F32),
        ],
        compiler_params=_params(("parallel", "parallel", "arbitrary"), vmem),
        name="diff_attn",
    )(q, k, vt, bias, lam_p, subln_g)


def _layer1_kernel(layer, x_ref, a_ref, wo_ref, g0_ref, b0_ref, wg_hbm, wu_hbm, wd_hbm, g1_ref, b1_ref,
                   o_ref, y_ref, xb_ref, acc_ref, h_ref, wg_ref, wu_ref, wd_ref, wide_stage, tall_stage,
                   wide_sem, tall_sem):
    _load_ffn_weights(layer, (wg_hbm, wu_hbm, wd_hbm), (wg_ref, wu_ref, wd_ref), (wide_stage, tall_stage),
                      (wide_sem, tall_sem))
    tm = x_ref.shape[1] // HALVES

    def mixer(hf, y_out):
        a_t = a_ref[0, :, hf * tm:(hf + 1) * tm]
        h = lax.dot_general(a_t, wo_ref[...], (((0,), (0,)), ((), ())), preferred_element_type=F32)
        y_out[...] = _layernorm(ALPHA * x_ref[0, hf * tm:(hf + 1) * tm, :] + h, g0_ref[...], b0_ref[...])

    _two_halves(mixer, o_ref, (wg_ref, wu_ref, wd_ref, g1_ref, b1_ref), (y_ref, xb_ref, acc_ref, h_ref))


def _layer1(layer, x, at, wo, g0, b0, wg, wu, wd, g1, b1):
    B, S, D = x.shape
    tm = HALVES * ROW_TILE
    vmem = _ffn_vmem(ROW_TILE) + D * D_ATTN * 2 + 2 * tm * D_ATTN * 2 + 6 * ROW_TILE * D * 4
    return pl.pallas_call(
        functools.partial(_layer1_kernel, layer),
        grid=(B, S // tm),
        in_specs=[
            pl.BlockSpec((1, tm, D), lambda b, i: (b, i, 0)),
            pl.BlockSpec((1, D_ATTN, tm), lambda b, i: (b, 0, i)),
            _resident((D_ATTN, D)), _resident((1, D)), _resident((1, D)),
        ] + _ffn_specs(),
        out_specs=pl.BlockSpec((1, tm, D), lambda b, i: (b, i, 0)),
        out_shape=jax.ShapeDtypeStruct((B, S, D), F32),
        scratch_shapes=_ffn_scratch(ROW_TILE),
        compiler_params=_params(("arbitrary", "arbitrary"), vmem),
        name="layer1_oproj_ffn",
    )(x, at, wo, g0, b0, wg, wu, wd, g1, b1)


def kernel(x, pool_w, pool_scale, w_qkv, w_o, lam_p, subln_g, rel_table, w_gate, w_up, w_down,
           ln_mix_g, ln_mix_b, ln_ffn_g, ln_ffn_b):
    row = lambda a: a.reshape(1, -1).astype(F32)
    ffn_w = lambda i: (w_gate.astype(F32), w_up.astype(F32), w_down.astype(F32),
                       row(ln_ffn_g[i]), row(ln_ffn_b[i]))

    x2 = _layer0(0, x, pool_w[0].astype(BF16), row(pool_scale[0]), row(ln_mix_g[0]), row(ln_mix_b[0]), *ffn_w(0))

    lambda_init = 0.8 - 0.6 * math.exp(-0.3 * 1)
    q, k, vt = _qkv(0, x2, w_qkv.astype(F32))
    gain = jnp.broadcast_to(subln_g[0].astype(F32)[:, None], (V_DIM, ATT_BLOCK))
    at = _attention(q, k, vt, _near_bias(rel_table), lam_p[0].astype(F32), gain, lambda_init)
    return _layer1(1, x2, at, w_o[0].astype(BF16), row(ln_mix_g[1]), row(ln_mix_b[1]), *ffn_w(1))
```

```python
import functools
import math

import jax
import jax.numpy as jnp
from jax import lax
from jax.experimental import pallas as pl
from jax.experimental.pallas import tpu as pltpu

D_MODEL = 1024
DEPTH = 2
ALPHA = (2.0 * DEPTH) ** 0.25
LN_EPS = 1e-5
POOL_WINDOWS = (2, 4, 8, 16)
N_GROUPS = len(POOL_WINDOWS)
GROUP_W = D_MODEL // N_GROUPS
POOL_HALO = 16
HEAD_DIM = 64
N_HEADS = D_MODEL // (2 * HEAD_DIM)
V_DIM = 2 * HEAD_DIM
D_ATTN = N_HEADS * V_DIM
NUM_BUCKETS = 32
MAX_EXACT = NUM_BUCKETS // 2
MAX_DISTANCE = 128
D_FF = 2816
SM_SCALE = HEAD_DIM ** -0.5
LOG2_E = math.log2(math.e)

FF_CHUNK = 256
ROW_TILE = 512
HALVES = 2
OVERLAP_AFTER_CHUNK = 1
STAGE_SLOTS = 6
WIDE_STAGE_ROWS = 64
TALL_STAGE_ROWS = 128
ATT_BLOCK = 256
ATT_TILE = 1024
NEAR_LOOKAHEAD = 8
FAR_BLOCKS = 2
SUM_ROWS = 16
MASK_VALUE = -1e30
V7X_VMEM_BYTES = 64 * 1024 * 1024
VMEM_RESERVED_BYTES = 6 * 1024 * 1024
VMEM_TEMP_FACTOR = 1.5

F32 = jnp.float32
BF16 = jnp.bfloat16
NT_DIMS = (((1,), (1,)), ((), ()))


def _params(semantics, vmem_bytes):
    limit = min(int(vmem_bytes * VMEM_TEMP_FACTOR), V7X_VMEM_BYTES - VMEM_RESERVED_BYTES)
    return pltpu.CompilerParams(dimension_semantics=semantics, vmem_limit_bytes=limit)


def _resident(shape):
    return pl.BlockSpec(shape, lambda *_: (0,) * len(shape), pipeline_mode=pl.Buffered(1))


def _layernorm(z, g, b):
    mu = jnp.mean(z, axis=-1, keepdims=True)
    zc = z - mu
    var = jnp.mean(zc * zc, axis=-1, keepdims=True)
    return zc * lax.rsqrt(var + LN_EPS) * g + b


def _ffn_ln(y_ref, wg_ref, wu_ref, wd_ref, g_ref, b_ref, xb_ref, acc_ref, h_ref, overlap=None):
    xb_ref[...] = y_ref[...].astype(BF16)
    nc = wg_ref.shape[1] // FF_CHUNK

    def hidden(c):
        cols = slice(c * FF_CHUNK, (c + 1) * FF_CHUNK)
        xb = xb_ref[...]
        gate = jnp.dot(xb, wg_ref[:, cols], preferred_element_type=F32)
        up = jnp.dot(xb, wu_ref[:, cols], preferred_element_type=F32)
        h_ref[c % 2] = (gate * jax.nn.sigmoid(gate) * up).astype(BF16)

    def down(c):
        rows = slice(c * FF_CHUNK, (c + 1) * FF_CHUNK)
        y = jnp.dot(h_ref[c % 2], wd_ref[rows, :], preferred_element_type=F32)
        if c == 0:
            acc_ref[...] = y
        else:
            acc_ref[...] += y

    hidden(0)
    for c in range(nc):
        if c + 1 < nc:
            hidden(c + 1)
        down(c)
        if overlap is not None and c == OVERLAP_AFTER_CHUNK:
            overlap()
    return _layernorm(ALPHA * y_ref[...] + acc_ref[...], g_ref[...], b_ref[...])


def _two_halves(mixer, o_ref, ffn_refs, scratch):
    y_ref, xb_ref, acc_ref, h_ref = scratch
    tm = y_ref.shape[1]
    mixer(0, y_ref.at[0])
    for h in range(HALVES):
        nxt = functools.partial(mixer, h + 1, y_ref.at[h + 1]) if h + 1 < HALVES else None
        o_ref[0, h * tm:(h + 1) * tm, :] = _ffn_ln(y_ref.at[h], *ffn_refs, xb_ref.at[h], acc_ref.at[h],
                                                   h_ref.at[h], overlap=nxt)


def _fetch_bf16(src, dst_ref, stage_ref, sem_ref):
    nslot, rows = stage_ref.shape[0], stage_ref.shape[1]
    assert src.shape[0] % rows == 0
    n = src.shape[0] // rows

    def chunk(c):
        slot = c % nslot
        return pltpu.make_async_copy(src.at[pl.ds(c * rows, rows), :], stage_ref.at[slot], sem_ref.at[slot])

    for c in range(min(nslot, n)):
        chunk(c).start()
    for c in range(n):
        chunk(c).wait()
        dst_ref[c * rows:(c + 1) * rows, :] = stage_ref[c % nslot].astype(BF16)
        if c + nslot < n:
            chunk(c + nslot).start()


def _load_ffn_weights(layer, hbm_refs, vmem_refs, stage_refs, sem_refs):
    wg_hbm, wu_hbm, wd_hbm = hbm_refs
    wg_ref, wu_ref, wd_ref = vmem_refs
    wide_stage, tall_stage = stage_refs
    wide_sem, tall_sem = sem_refs

    @pl.when(jnp.logical_and(pl.program_id(0) == 0, pl.program_id(1) == 0))
    def _():
        _fetch_bf16(wg_hbm.at[layer], wg_ref, wide_stage, wide_sem)
        _fetch_bf16(wu_hbm.at[layer], wu_ref, wide_stage, wide_sem)
        _fetch_bf16(wd_hbm.at[layer], wd_ref, tall_stage, tall_sem)


def _ffn_specs():
    hbm = pl.BlockSpec(memory_space=pl.ANY)
    return [hbm, hbm, hbm, _resident((1, D_MODEL)), _resident((1, D_MODEL))]


def _ffn_scratch(tm):
    return [pltpu.VMEM((HALVES, tm, D_MODEL), F32), pltpu.VMEM((HALVES, tm, D_MODEL), BF16),
            pltpu.VMEM((HALVES, tm, D_MODEL), F32), pltpu.VMEM((HALVES, 2, tm, FF_CHUNK), BF16),
            pltpu.VMEM((D_MODEL, D_FF), BF16), pltpu.VMEM((D_MODEL, D_FF), BF16), pltpu.VMEM((D_FF, D_MODEL), BF16),
            pltpu.VMEM((STAGE_SLOTS, WIDE_STAGE_ROWS, D_FF), F32),
            pltpu.VMEM((STAGE_SLOTS, TALL_STAGE_ROWS, D_MODEL), F32),
            pltpu.SemaphoreType.DMA((STAGE_SLOTS,)), pltpu.SemaphoreType.DMA((STAGE_SLOTS,))]


def _ffn_vmem(tm):
    rows = HALVES * tm
    stage = STAGE_SLOTS * (WIDE_STAGE_ROWS * D_FF + TALL_STAGE_ROWS * D_MODEL) * 4
    return (3 * D_MODEL * D_FF * 2 + stage + 4 * rows * D_MODEL * 4 + rows * D_MODEL * 10
            + 8 * tm * FF_CHUNK * 4)


def _layer0_kernel(layer, x_ref, halo_ref, pw_ref, ps_ref, g0_ref, b0_ref, wg_hbm, wu_hbm, wd_hbm, g1_ref, b1_ref,
                   o_ref, y_ref, xb_ref, acc_ref, h_ref, wg_ref, wu_ref, wd_ref, wide_stage, tall_stage,
                   wide_sem, tall_sem):
    _load_ffn_weights(layer, (wg_hbm, wu_hbm, wd_hbm), (wg_ref, wu_ref, wd_ref), (wide_stage, tall_stage),
                      (wide_sem, tall_sem))
    s = pl.program_id(1)
    ts = x_ref.shape[1] // HALVES

    def mixer(hf, y_out):
        x = x_ref[0, hf * ts:(hf + 1) * ts, :]
        if hf == 0:
            halo = jnp.where(s > 0, halo_ref[0], 0.0)
        else:
            halo = x_ref[0, hf * ts - POOL_HALO:hf * ts, :]
        xh = jnp.concatenate([halo, x], axis=0)
        t = (s * HALVES + hf) * ts + lax.broadcasted_iota(jnp.int32, (ts, 1), 0)
        mixed = []
        for g, win in enumerate(POOL_WINDOWS):
            a = xh[:, g * GROUP_W:(g + 1) * GROUP_W]
            shift = 1
            while shift < win:
                a = a + pltpu.roll(a, shift, 0)
                shift *= 2
            inv_cnt = 1.0 / jnp.minimum(t + 1, win).astype(F32)
            d = a[POOL_HALO:] * inv_cnt - x[:, g * GROUP_W:(g + 1) * GROUP_W]
            mixed.append(jnp.dot(d.astype(BF16), pw_ref[g], preferred_element_type=F32))
        h = jnp.concatenate(mixed, axis=-1) * ps_ref[...]
        y_out[...] = _layernorm(ALPHA * x + h, g0_ref[...], b0_ref[...])

    _two_halves(mixer, o_ref, (wg_ref, wu_ref, wd_ref, g1_ref, b1_ref), (y_ref, xb_ref, acc_ref, h_ref))


def _layer0(layer, x, pw, ps, g0, b0, wg, wu, wd, g1, b1):
    B, S, D = x.shape
    ts = HALVES * ROW_TILE
    halo_blocks = ts // POOL_HALO
    vmem = _ffn_vmem(ROW_TILE) + 10 * ROW_TILE * D * 4
    return pl.pallas_call(
        functools.partial(_layer0_kernel, layer),
        grid=(B, S // ts),
        in_specs=[
            pl.BlockSpec((1, ts, D), lambda b_, s_: (b_, s_, 0)),
            pl.BlockSpec((1, POOL_HALO, D), lambda b_, s_: (b_, jnp.maximum(s_ * halo_blocks - 1, 0), 0)),
            _resident((N_GROUPS, GROUP_W, GROUP_W)), _resident((1, D)), _resident((1, D)), _resident((1, D)),
        ] + _ffn_specs(),
        out_specs=pl.BlockSpec((1, ts, D), lambda b_, s_: (b_, s_, 0)),
        out_shape=jax.ShapeDtypeStruct((B, S, D), F32),
        scratch_shapes=_ffn_scratch(ROW_TILE),
        compiler_params=_params(("arbitrary", "arbitrary"), vmem),
        name="layer0_pool_ffn",
    )(x, x, pw, ps, g0, b0, wg, wu, wd, g1, b1)


def _qkv_kernel(layer, x_ref, w_hbm, q_ref, k_ref, vt_ref, w_ref, stage_ref, sem_ref):
    @pl.when(jnp.logical_and(pl.program_id(0) == 0, pl.program_id(1) == 0))
    def _():
        _fetch_bf16(w_hbm.at[layer], w_ref, stage_ref, sem_ref)

    xb = x_ref[0].astype(BF16)
    q = jnp.dot(xb, w_ref[:, :D_ATTN], preferred_element_type=F32) * (SM_SCALE * LOG2_E)
    q_ref[0] = q.astype(BF16)
    k_ref[0] = jnp.dot(xb, w_ref[:, D_ATTN:2 * D_ATTN], preferred_element_type=F32).astype(BF16)
    vt = lax.dot_general(w_ref[:, 2 * D_ATTN:], xb, (((0,), (1,)), ((), ())), preferred_element_type=F32)
    vt_ref[0] = vt.astype(BF16)


def _qkv(layer, x, w_qkv):
    B, S, D = x.shape
    tm = ROW_TILE
    stage = STAGE_SLOTS * WIDE_STAGE_ROWS * 3 * D_ATTN * 4
    vmem = (3 * D * D_ATTN * 2 + stage + 2 * tm * D * 4 + 6 * tm * D_ATTN * 2 + tm * D * 2
            + 3 * tm * D_ATTN * 4)
    return pl.pallas_call(
        functools.partial(_qkv_kernel, layer),
        grid=(B, S // tm),
        in_specs=[
            pl.BlockSpec((1, tm, D), lambda b, i: (b, i, 0)),
            pl.BlockSpec(memory_space=pl.ANY),
        ],
        out_specs=[
            pl.BlockSpec((1, tm, D_ATTN), lambda b, i: (b, i, 0)),
            pl.BlockSpec((1, tm, D_ATTN), lambda b, i: (b, i, 0)),
            pl.BlockSpec((1, D_ATTN, tm), lambda b, i: (b, 0, i)),
        ],
        out_shape=[
            jax.ShapeDtypeStruct((B, S, D_ATTN), BF16),
            jax.ShapeDtypeStruct((B, S, D_ATTN), BF16),
            jax.ShapeDtypeStruct((B, D_ATTN, S), BF16),
        ],
        scratch_shapes=[pltpu.VMEM((D, 3 * D_ATTN), BF16),
                        pltpu.VMEM((STAGE_SLOTS, WIDE_STAGE_ROWS, 3 * D_ATTN), F32),
                        pltpu.SemaphoreType.DMA((STAGE_SLOTS,))],
        compiler_params=_params(("arbitrary", "arbitrary"), vmem),
        name="qkv_proj",
    )(x, w_qkv)


def _attn_kernel(lambda_init, q_ref, k_ref, vt_ref, diag_ref, corner_ref, lam_ref, sg_ref, o_ref,
                 qs_ref, m_ref, acc_ref, s0_ref, s1_ref, ns_ref):
    i = pl.program_id(2)
    blk = ATT_BLOCK
    nq = q_ref.shape[1] // blk
    half = blk - MAX_DISTANCE
    diag_bias = diag_ref[0]
    prev_corner = corner_ref[0]

    lane = lax.broadcasted_iota(jnp.int32, (blk, V_DIM), 1)
    for hq in range(nq):
        q = q_ref[0, hq * blk:(hq + 1) * blk, :]
        zero = jnp.zeros_like(q)
        qs_ref[2 * hq] = jnp.where(lane < HEAD_DIM, q, zero)
        qs_ref[2 * hq + 1] = jnp.where(lane >= HEAD_DIM, q, zero)

    def scores(kc, cb, nblk=1):
        rows = nblk * blk
        start = pl.multiple_of(kc * rows, rows)
        k = k_ref[0, pl.ds(start, rows), :]
        return lax.dot_general(k, qs_ref[cb], NT_DIMS, preferred_element_type=F32)

    def values(kc, nblk=1):
        rows = nblk * blk
        start = pl.multiple_of(kc * rows, rows)
        vt = vt_ref[0, :, pl.ds(start, rows)]
        return jnp.concatenate([vt, jnp.ones((SUM_ROWS, rows), BF16)], axis=0)

    def update(cb, s, vt, bias, first):
        if bias == "diag":
            s = s + diag_bias
        elif bias == "prev":
            top = s.shape[0] - half
            bottom = jnp.concatenate([s[top:, :half] + prev_corner, s[top:, half:]], axis=1)
            s = jnp.concatenate([s[:top], bottom], axis=0)
        m_new = jnp.max(s, axis=0, keepdims=True)
        if not first:
            m_prev = m_ref[cb]
            m_new = jnp.maximum(m_prev, m_new)
        p = jnp.exp2(s - m_new)
        pv = jnp.dot(vt, p.astype(BF16), preferred_element_type=F32)
        if first:
            acc_ref[cb] = pv
        else:
            acc_ref[cb] = acc_ref[cb] * jnp.exp2(m_prev - m_new) + pv
        m_ref[cb] = m_new

    base = nq * i
    near = [(d, hq, c) for d in range(nq - 1, -1, -1) for hq in range(d, nq) for c in range(2)]

    s_refs = (s0_ref, s1_ref)

    def finalize(hq):
        lp = lam_ref[...]
        lam = (jnp.exp(jnp.sum(lp[0:1] * lp[1:2], axis=-1, keepdims=True))
               - jnp.exp(jnp.sum(lp[2:3] * lp[3:4], axis=-1, keepdims=True)) + lambda_init)
        a0 = acc_ref[2 * hq]
        a1 = acc_ref[2 * hq + 1]
        inv0 = 1.0 / a0[V_DIM:V_DIM + 1]
        inv1 = lam / a1[V_DIM:V_DIM + 1]
        ot = a0[:V_DIM] * inv0 - a1[:V_DIM] * inv1
        ot = ot * lax.rsqrt(jnp.mean(ot * ot, axis=0, keepdims=True) + LN_EPS)
        gain = sg_ref[...] * (1.0 - lambda_init)
        o_ref[0, :, hq * blk:(hq + 1) * blk] = (ot * gain).astype(BF16)

    def near_phase(extra_scores, final):
        def score_job(j, d, hq, c):
            def run():
                ns_ref[j] = scores(base + d, 2 * hq + c)
            return run

        jobs = [score_job(j, d, hq, c) for j, (d, hq, c) in enumerate(near)] + extra_scores
        ahead = max(NEAR_LOOKAHEAD, len(jobs) - len(near))
        for job in jobs[:ahead]:
            job()
        vts = {}
        for j, (d, hq, c) in enumerate(near):
            if j + ahead < len(jobs):
                jobs[j + ahead]()
            if d not in vts:
                vts[d] = values(base + d)
            bias = "diag" if hq == d else "prev" if hq == d + 1 else None
            update(2 * hq + c, ns_ref[j], vts[d], bias, hq == d)
            if final and d == 0 and c == 1:
                finalize(hq)

    def score_into(kc, slot, cb):
        def run():
            s_refs[slot][cb] = scores(kc, cb, FAR_BLOCKS)
        return run

    def issue_consume(kc, slot, issue_next=True, last=False):
        vt = values(kc, FAR_BLOCKS)
        for cb in range(2 * nq):
            if issue_next:
                score_into(kc + 1, 1 - slot, cb)()
            bias = "prev" if (last and cb < 2) else None
            update(cb, s_refs[slot][cb], vt, bias, False)
            if last and cb % 2 == 1:
                finalize(cb // 2)

    @pl.when(i == 0)
    def _():
        near_phase([], final=True)

    assert nq % (2 * FAR_BLOCKS) == 0, "the pipeline pairs chunks and needs an even count per tile"
    pairs = nq // (2 * FAR_BLOCKS) * i

    @pl.when(i > 0)
    def _():
        near_phase([score_into(0, 0, cb) for cb in range(2 * nq)], final=False)

        def pair(t, carry):
            issue_consume(2 * t, 0)
            issue_consume(2 * t + 1, 1)
            return carry

        lax.fori_loop(0, pairs - 1, pair, 0)
        issue_consume(2 * pairs - 2, 0)
        issue_consume(2 * pairs - 1, 1, issue_next=False, last=True)


def _rel_bucket(rel):
    n = jnp.maximum(rel, 0)
    nf = jnp.maximum(n, 1).astype(F32)
    large = MAX_EXACT + jnp.floor(jnp.log(nf / MAX_EXACT) / math.log(MAX_DISTANCE / MAX_EXACT)
                                  * (NUM_BUCKETS - MAX_EXACT)).astype(jnp.int32)
    return jnp.where(n < MAX_EXACT, n, jnp.minimum(large, NUM_BUCKETS - 1))


def _near_bias(rel_table):
    blk = ATT_BLOCK
    half = blk - MAX_DISTANCE
    table = rel_table.astype(F32)
    table = (table - table[NUM_BUCKETS - 1]) * LOG2_E

    def tile(n, offset):
        rel = jnp.arange(n)[None, :] - jnp.arange(n)[:, None] + offset
        bucket = _rel_bucket(rel)
        bias = jnp.zeros((N_HEADS, n, n), F32)
        for b in range(NUM_BUCKETS):
            bias = jnp.where(bucket[None] == b, table[b][:, None, None], bias)
        return jnp.where(rel[None] >= 0, bias, MASK_VALUE)

    return tile(blk, 0), tile(half, half)


def _attention(q, k, vt, diag_bias, corner_bias, lam_p, subln_g, lambda_init):
    B, S, _ = q.shape
    tq = ATT_TILE
    blk = ATT_BLOCK
    nq = tq // blk
    nchain = 2 * nq
    nnear = nq * (nq + 1)
    H = N_HEADS
    rows = V_DIM + SUM_ROWS
    vmem = (4 * S * V_DIM * 2 + 2 * 2 * blk * blk * 4 + ((2 * FAR_BLOCKS + 1) * nchain + nnear) * blk * blk * 4
            + 2 * nchain * rows * blk * 4 + 4 * tq * V_DIM * 2)
    return pl.pallas_call(
        functools.partial(_attn_kernel, lambda_init),
        grid=(B, H, S // tq),
        in_specs=[
            pl.BlockSpec((1, tq, V_DIM), lambda b, h, i: (b, i, h)),
            pl.BlockSpec((1, S, V_DIM), lambda b, h, i: (b, 0, h)),
            pl.BlockSpec((1, V_DIM, S), lambda b, h, i: (b, h, 0)),
            pl.BlockSpec((1, blk, blk), lambda b, h, i: (h, 0, 0)),
            pl.BlockSpec((1,) + corner_bias.shape[1:], lambda b, h, i: (h, 0, 0)),
            pl.BlockSpec((4, HEAD_DIM), lambda b, h, i: (0, 0)),
            pl.BlockSpec((V_DIM, blk), lambda b, h, i: (0, 0)),
        ],
        out_specs=pl.BlockSpec((1, V_DIM, tq), lambda b, h, i: (b, h, i)),
        out_shape=jax.ShapeDtypeStruct((B, D_ATTN, S), BF16),
        scratch_shapes=[
            pltpu.VMEM((nchain, blk, V_DIM), BF16),
            pltpu.VMEM((nchain, 1, blk), F32),
            pltpu.VMEM((nchain, rows, blk), F32),
            pltpu.VMEM((nchain, FAR_BLOCKS * blk, blk), F32),
            pltpu.VMEM((nchain, FAR_BLOCKS * blk, blk), F32),
            pltpu.VMEM((nnear, blk, blk), F32),
        ],
        compiler_params=_params(("parallel", "parallel", "arbitrary"), vmem),
        name="diff_attn",
    )(q, k, vt, diag_bias, corner_bias, lam_p, subln_g)


def _layer1_kernel(layer, x_ref, a_ref, wo_ref, g0_ref, b0_ref, wg_hbm, wu_hbm, wd_hbm, g1_ref, b1_ref,
                   o_ref, y_ref, xb_ref, acc_ref, h_ref, wg_ref, wu_ref, wd_ref, wide_stage, tall_stage,
                   wide_sem, tall_sem):
    _load_ffn_weights(layer, (wg_hbm, wu_hbm, wd_hbm), (wg_ref, wu_ref, wd_ref), (wide_stage, tall_stage),
                      (wide_sem, tall_sem))
    tm = x_ref.shape[1] // HALVES

    def mixer(hf, y_out):
        a_t = a_ref[0, :, hf * tm:(hf + 1) * tm]
        h = lax.dot_general(a_t, wo_ref[...], (((0,), (0,)), ((), ())), preferred_element_type=F32)
        y_out[...] = _layernorm(ALPHA * x_ref[0, hf * tm:(hf + 1) * tm, :] + h, g0_ref[...], b0_ref[...])

    _two_halves(mixer, o_ref, (wg_ref, wu_ref, wd_ref, g1_ref, b1_ref), (y_ref, xb_ref, acc_ref, h_ref))


def _layer1(layer, x, at, wo, g0, b0, wg, wu, wd, g1, b1):
    B, S, D = x.shape
    tm = HALVES * ROW_TILE
    vmem = _ffn_vmem(ROW_TILE) + D * D_ATTN * 2 + 2 * tm * D_ATTN * 2 + 6 * ROW_TILE * D * 4
    return pl.pallas_call(
        functools.partial(_layer1_kernel, layer),
        grid=(B, S // tm),
        in_specs=[
            pl.BlockSpec((1, tm, D), lambda b, i: (b, i, 0)),
            pl.BlockSpec((1, D_ATTN, tm), lambda b, i: (b, 0, i)),
            _resident((D_ATTN, D)), _resident((1, D)), _resident((1, D)),
        ] + _ffn_specs(),
        out_specs=pl.BlockSpec((1, tm, D), lambda b, i: (b, i, 0)),
        out_shape=jax.ShapeDtypeStruct((B, S, D), F32),
        scratch_shapes=_ffn_scratch(ROW_TILE),
        compiler_params=_params(("arbitrary", "arbitrary"), vmem),
        name="layer1_oproj_ffn",
    )(x, at, wo, g0, b0, wg, wu, wd, g1, b1)


def kernel(x, pool_w, pool_scale, w_qkv, w_o, lam_p, subln_g, rel_table, w_gate, w_up, w_down,
           ln_mix_g, ln_mix_b, ln_ffn_g, ln_ffn_b):
    row = lambda a: a.reshape(1, -1).astype(F32)
    ffn_w = lambda i: (w_gate.astype(F32), w_up.astype(F32), w_down.astype(F32),
                       row(ln_ffn_g[i]), row(ln_ffn_b[i]))

    x2 = _layer0(0, x, pool_w[0].astype(BF16), row(pool_scale[0]), row(ln_mix_g[0]), row(ln_mix_b[0]), *ffn_w(0))

    lambda_init = 0.8 - 0.6 * math.exp(-0.3 * 1)
    q, k, vt = _qkv(0, x2, w_qkv.astype(F32))
    gain = jnp.broadcast_to(subln_g[0].astype(F32)[:, None], (V_DIM, ATT_BLOCK))
    at = _attention(q, k, vt, *_near_bias(rel_table), lam_p[0].astype(F32), gain, lambda_init)
    return _layer1(1, x2, at, w_o[0].astype(BF16), row(ln_mix_g[1]), row(ln_mix_b[1]), *ffn_w(1))
```

```python
import functools
import math

import jax
import jax.numpy as jnp
from jax import lax
from jax.experimental import pallas as pl
from jax.experimental.pallas import tpu as pltpu

D_MODEL = 1024
DEPTH = 2
ALPHA = (2.0 * DEPTH) ** 0.25
LN_EPS = 1e-5
POOL_WINDOWS = (2, 4, 8, 16)
N_GROUPS = len(POOL_WINDOWS)
GROUP_W = D_MODEL // N_GROUPS
POOL_HALO = 16
HEAD_DIM = 64
N_HEADS = D_MODEL // (2 * HEAD_DIM)
V_DIM = 2 * HEAD_DIM
D_ATTN = N_HEADS * V_DIM
NUM_BUCKETS = 32
MAX_EXACT = NUM_BUCKETS // 2
MAX_DISTANCE = 128
D_FF = 2816
SM_SCALE = HEAD_DIM ** -0.5
LOG2_E = math.log2(math.e)

FF_CHUNK = 256
ROW_TILE = 512
HALVES = 2
OVERLAP_AFTER_CHUNK = 1
STAGE_SLOTS = 6
WIDE_STAGE_ROWS = 64
TALL_STAGE_ROWS = 128
ATT_BLOCK = 256
ATT_TILE = 1024
NEAR_LOOKAHEAD = 8
FAR_BLOCKS = 2
SUM_ROWS = 16
MASK_VALUE = -1e30
V7X_VMEM_BYTES = 64 * 1024 * 1024
VMEM_RESERVED_BYTES = 6 * 1024 * 1024
VMEM_TEMP_FACTOR = 1.5

F32 = jnp.float32
BF16 = jnp.bfloat16
NT_DIMS = (((1,), (1,)), ((), ()))


def _params(semantics, vmem_bytes):
    limit = min(int(vmem_bytes * VMEM_TEMP_FACTOR), V7X_VMEM_BYTES - VMEM_RESERVED_BYTES)
    return pltpu.CompilerParams(dimension_semantics=semantics, vmem_limit_bytes=limit)


def _resident(shape):
    return pl.BlockSpec(shape, lambda *_: (0,) * len(shape), pipeline_mode=pl.Buffered(1))


def _layernorm(z, g, b):
    mu = jnp.mean(z, axis=-1, keepdims=True)
    zc = z - mu
    var = jnp.mean(zc * zc, axis=-1, keepdims=True)
    return zc * lax.rsqrt(var + LN_EPS) * g + b


def _ffn_ln(y_ref, wg_ref, wu_ref, wd_ref, g_ref, b_ref, xb_ref, acc_ref, h_ref, overlap=None):
    xb_ref[...] = y_ref[...].astype(BF16)
    nc = wg_ref.shape[1] // FF_CHUNK

    def hidden(c):
        cols = slice(c * FF_CHUNK, (c + 1) * FF_CHUNK)
        xb = xb_ref[...]
        gate = jnp.dot(xb, wg_ref[:, cols], preferred_element_type=F32)
        up = jnp.dot(xb, wu_ref[:, cols], preferred_element_type=F32)
        h_ref[c % 2] = (gate * jax.nn.sigmoid(gate) * up).astype(BF16)

    def down(c):
        rows = slice(c * FF_CHUNK, (c + 1) * FF_CHUNK)
        y = jnp.dot(h_ref[c % 2], wd_ref[rows, :], preferred_element_type=F32)
        if c == 0:
            acc_ref[...] = y
        else:
            acc_ref[...] += y

    hidden(0)
    for c in range(nc):
        if c + 1 < nc:
            hidden(c + 1)
        down(c)
        if overlap is not None and c == OVERLAP_AFTER_CHUNK:
            overlap()
    return _layernorm(ALPHA * y_ref[...] + acc_ref[...], g_ref[...], b_ref[...])


def _two_halves(mixer, o_ref, ffn_refs, scratch):
    y_ref, xb_ref, acc_ref, h_ref = scratch
    tm = y_ref.shape[1]
    mixer(0, y_ref.at[0])
    for h in range(HALVES):
        nxt = functools.partial(mixer, h + 1, y_ref.at[h + 1]) if h + 1 < HALVES else None
        o_ref[0, h * tm:(h + 1) * tm, :] = _ffn_ln(y_ref.at[h], *ffn_refs, xb_ref.at[h], acc_ref.at[h],
                                                   h_ref.at[h], overlap=nxt)


def _fetch_bf16(src, dst_ref, stage_ref, sem_ref):
    nslot, rows = stage_ref.shape[0], stage_ref.shape[1]
    assert src.shape[0] % rows == 0
    n = src.shape[0] // rows

    def chunk(c):
        slot = c % nslot
        return pltpu.make_async_copy(src.at[pl.ds(c * rows, rows), :], stage_ref.at[slot], sem_ref.at[slot])

    for c in range(min(nslot, n)):
        chunk(c).start()
    for c in range(n):
        chunk(c).wait()
        dst_ref[c * rows:(c + 1) * rows, :] = stage_ref[c % nslot].astype(BF16)
        if c + nslot < n:
            chunk(c + nslot).start()


def _load_ffn_weights(layer, hbm_refs, vmem_refs, stage_refs, sem_refs):
    wg_hbm, wu_hbm, wd_hbm = hbm_refs
    wg_ref, wu_ref, wd_ref = vmem_refs
    wide_stage, tall_stage = stage_refs
    wide_sem, tall_sem = sem_refs

    @pl.when(jnp.logical_and(pl.program_id(0) == 0, pl.program_id(1) == 0))
    def _():
        _fetch_bf16(wg_hbm.at[layer], wg_ref, wide_stage, wide_sem)
        _fetch_bf16(wu_hbm.at[layer], wu_ref, wide_stage, wide_sem)
        _fetch_bf16(wd_hbm.at[layer], wd_ref, tall_stage, tall_sem)


def _ffn_specs():
    hbm = pl.BlockSpec(memory_space=pl.ANY)
    return [hbm, hbm, hbm, _resident((1, D_MODEL)), _resident((1, D_MODEL))]


def _ffn_scratch(tm):
    return [pltpu.VMEM((HALVES, tm, D_MODEL), F32), pltpu.VMEM((HALVES, tm, D_MODEL), BF16),
            pltpu.VMEM((HALVES, tm, D_MODEL), F32), pltpu.VMEM((HALVES, 2, tm, FF_CHUNK), BF16),
            pltpu.VMEM((D_MODEL, D_FF), BF16), pltpu.VMEM((D_MODEL, D_FF), BF16), pltpu.VMEM((D_FF, D_MODEL), BF16),
            pltpu.VMEM((STAGE_SLOTS, WIDE_STAGE_ROWS, D_FF), F32),
            pltpu.VMEM((STAGE_SLOTS, TALL_STAGE_ROWS, D_MODEL), F32),
            pltpu.SemaphoreType.DMA((STAGE_SLOTS,)), pltpu.SemaphoreType.DMA((STAGE_SLOTS,))]


def _ffn_vmem(tm):
    rows = HALVES * tm
    stage = STAGE_SLOTS * (WIDE_STAGE_ROWS * D_FF + TALL_STAGE_ROWS * D_MODEL) * 4
    return (3 * D_MODEL * D_FF * 2 + stage + 4 * rows * D_MODEL * 4 + rows * D_MODEL * 10
            + 8 * tm * FF_CHUNK * 4)


def _layer0_kernel(layer, x_ref, halo_ref, pw_ref, ps_ref, g0_ref, b0_ref, wg_hbm, wu_hbm, wd_hbm, g1_ref, b1_ref,
                   o_ref, y_ref, xb_ref, acc_ref, h_ref, wg_ref, wu_ref, wd_ref, wide_stage, tall_stage,
                   wide_sem, tall_sem):
    _load_ffn_weights(layer, (wg_hbm, wu_hbm, wd_hbm), (wg_ref, wu_ref, wd_ref), (wide_stage, tall_stage),
                      (wide_sem, tall_sem))
    s = pl.program_id(1)
    ts = x_ref.shape[1] // HALVES

    def mixer(hf, y_out):
        x = x_ref[0, hf * ts:(hf + 1) * ts, :]
        if hf == 0:
            halo = jnp.where(s > 0, halo_ref[0], 0.0)
        else:
            halo = x_ref[0, hf * ts - POOL_HALO:hf * ts, :]
        xh = jnp.concatenate([halo, x], axis=0)
        t = (s * HALVES + hf) * ts + lax.broadcasted_iota(jnp.int32, (ts, 1), 0)
        mixed = []
        for g, win in enumerate(POOL_WINDOWS):
            a = xh[:, g * GROUP_W:(g + 1) * GROUP_W]
            shift = 1
            while shift < win:
                a = a + pltpu.roll(a, shift, 0)
                shift *= 2
            inv_cnt = 1.0 / jnp.minimum(t + 1, win).astype(F32)
            d = a[POOL_HALO:] * inv_cnt - x[:, g * GROUP_W:(g + 1) * GROUP_W]
            mixed.append(jnp.dot(d.astype(BF16), pw_ref[g], preferred_element_type=F32))
        h = jnp.concatenate(mixed, axis=-1) * ps_ref[...]
        y_out[...] = _layernorm(ALPHA * x + h, g0_ref[...], b0_ref[...])

    _two_halves(mixer, o_ref, (wg_ref, wu_ref, wd_ref, g1_ref, b1_ref), (y_ref, xb_ref, acc_ref, h_ref))


def _layer0(layer, x, pw, ps, g0, b0, wg, wu, wd, g1, b1):
    B, S, D = x.shape
    ts = HALVES * ROW_TILE
    halo_blocks = ts // POOL_HALO
    vmem = _ffn_vmem(ROW_TILE) + 10 * ROW_TILE * D * 4
    return pl.pallas_call(
        functools.partial(_layer0_kernel, layer),
        grid=(B, S // ts),
        in_specs=[
            pl.BlockSpec((1, ts, D), lambda b_, s_: (b_, s_, 0)),
            pl.BlockSpec((1, POOL_HALO, D), lambda b_, s_: (b_, jnp.maximum(s_ * halo_blocks - 1, 0), 0)),
            _resident((N_GROUPS, GROUP_W, GROUP_W)), _resident((1, D)), _resident((1, D)), _resident((1, D)),
        ] + _ffn_specs(),
        out_specs=pl.BlockSpec((1, ts, D), lambda b_, s_: (b_, s_, 0)),
        out_shape=jax.ShapeDtypeStruct((B, S, D), F32),
        scratch_shapes=_ffn_scratch(ROW_TILE),
        compiler_params=_params(("arbitrary", "arbitrary"), vmem),
        name="layer0_pool_ffn",
    )(x, x, pw, ps, g0, b0, wg, wu, wd, g1, b1)


def _qkv_kernel(layer, x_ref, w_hbm, fg_ref, fu_ref, fd_ref, q_ref, k_ref, vt_ref, bg_ref, bu_ref, bd_ref,
                w_ref, stage_ref, sem_ref):
    @pl.when(jnp.logical_and(pl.program_id(0) == 0, pl.program_id(1) == 0))
    def _():
        _fetch_bf16(w_hbm.at[layer], w_ref, stage_ref, sem_ref)

    bg_ref[...] = fg_ref[0].astype(BF16)
    bu_ref[...] = fu_ref[0].astype(BF16)
    bd_ref[...] = fd_ref[0].astype(BF16)

    xb = x_ref[0].astype(BF16)
    q = jnp.dot(xb, w_ref[:, :D_ATTN], preferred_element_type=F32) * (SM_SCALE * LOG2_E)
    q_ref[0] = q.astype(BF16)
    k_ref[0] = jnp.dot(xb, w_ref[:, D_ATTN:2 * D_ATTN], preferred_element_type=F32).astype(BF16)
    vt = lax.dot_general(w_ref[:, 2 * D_ATTN:], xb, (((0,), (1,)), ((), ())), preferred_element_type=F32)
    vt_ref[0] = vt.astype(BF16)


def _qkv(layer, x, w_qkv, ffn_layer, w_gate, w_up, w_down):
    B, S, D = x.shape
    tm = ROW_TILE
    ns = S // tm
    steps = B * ns
    rows = D // steps
    tall_rows = 2 * D_FF // steps
    assert rows * steps == D and rows % 16 == 0 and tall_rows * steps == 2 * D_FF and tall_rows % 16 == 0
    stage = STAGE_SLOTS * WIDE_STAGE_ROWS * 3 * D_ATTN * 4
    vmem = (3 * D * D_ATTN * 2 + stage + 2 * tm * D * 4 + 6 * tm * D_ATTN * 2 + tm * D * 2
            + 3 * tm * D_ATTN * 4 + 2 * (2 * rows * D_FF + tall_rows * D) * 6)
    slab_in = pl.BlockSpec((1, rows, D_FF), lambda b, i: (ffn_layer, b * ns + i, 0))
    slab_out = pl.BlockSpec((rows, D_FF), lambda b, i: (b * ns + i, 0))
    tall_in = pl.BlockSpec((1, tall_rows, D), lambda b, i: (ffn_layer, (b * ns + i) // 2, 0))
    tall_out = pl.BlockSpec((tall_rows, D), lambda b, i: ((b * ns + i) // 2, 0))
    q, k, vt, wg, wu, wd = pl.pallas_call(
        functools.partial(_qkv_kernel, layer),
        grid=(B, ns),
        in_specs=[
            pl.BlockSpec((1, tm, D), lambda b, i: (b, i, 0)),
            pl.BlockSpec(memory_space=pl.ANY),
            slab_in, slab_in, tall_in,
        ],
        out_specs=[
            pl.BlockSpec((1, tm, D_ATTN), lambda b, i: (b, i, 0)),
            pl.BlockSpec((1, tm, D_ATTN), lambda b, i: (b, i, 0)),
            pl.BlockSpec((1, D_ATTN, tm), lambda b, i: (b, 0, i)),
            slab_out, slab_out, tall_out,
        ],
        out_shape=[
            jax.ShapeDtypeStruct((B, S, D_ATTN), BF16),
            jax.ShapeDtypeStruct((B, S, D_ATTN), BF16),
            jax.ShapeDtypeStruct((B, D_ATTN, S), BF16),
            jax.ShapeDtypeStruct((D, D_FF), BF16),
            jax.ShapeDtypeStruct((D, D_FF), BF16),
            jax.ShapeDtypeStruct((D_FF, D), BF16),
        ],
        scratch_shapes=[pltpu.VMEM((D, 3 * D_ATTN), BF16),
                        pltpu.VMEM((STAGE_SLOTS, WIDE_STAGE_ROWS, 3 * D_ATTN), F32),
                        pltpu.SemaphoreType.DMA((STAGE_SLOTS,))],
        compiler_params=_params(("arbitrary", "arbitrary"), vmem),
        name="qkv_proj",
    )(x, w_qkv, w_gate, w_up, w_down)
    return q, k, vt, wg, wu, wd


def _attn_kernel(lambda_init, q_ref, k_ref, vt_ref, diag_ref, corner_ref, lam_ref, sg_ref, o_ref,
                 qs_ref, m_ref, acc_ref, s0_ref, s1_ref, ns_ref):
    i = pl.program_id(2)
    blk = ATT_BLOCK
    nq = q_ref.shape[1] // blk
    half = blk - MAX_DISTANCE
    diag_bias = diag_ref[0]
    prev_corner = corner_ref[0]

    lane = lax.broadcasted_iota(jnp.int32, (blk, V_DIM), 1)
    for hq in range(nq):
        q = q_ref[0, hq * blk:(hq + 1) * blk, :]
        zero = jnp.zeros_like(q)
        qs_ref[2 * hq] = jnp.where(lane < HEAD_DIM, q, zero)
        qs_ref[2 * hq + 1] = jnp.where(lane >= HEAD_DIM, q, zero)

    def scores(kc, cb, nblk=1):
        rows = nblk * blk
        start = pl.multiple_of(kc * rows, rows)
        k = k_ref[0, pl.ds(start, rows), :]
        return lax.dot_general(k, qs_ref[cb], NT_DIMS, preferred_element_type=F32)

    def values(kc, nblk=1):
        rows = nblk * blk
        start = pl.multiple_of(kc * rows, rows)
        vt = vt_ref[0, :, pl.ds(start, rows)]
        return jnp.concatenate([vt, jnp.ones((SUM_ROWS, rows), BF16)], axis=0)

    def update(cb, s, vt, bias, first):
        if bias == "diag":
            s = s + diag_bias
        elif bias == "prev":
            top = s.shape[0] - half
            bottom = jnp.concatenate([s[top:, :half] + prev_corner, s[top:, half:]], axis=1)
            s = jnp.concatenate([s[:top], bottom], axis=0)
        m_new = jnp.max(s, axis=0, keepdims=True)
        if not first:
            m_prev = m_ref[cb]
            m_new = jnp.maximum(m_prev, m_new)
        p = jnp.exp2(s - m_new)
        pv = jnp.dot(vt, p.astype(BF16), preferred_element_type=F32)
        if first:
            acc_ref[cb] = pv
        else:
            acc_ref[cb] = acc_ref[cb] * jnp.exp2(m_prev - m_new) + pv
        m_ref[cb] = m_new

    base = nq * i
    near = [(d, hq, c) for d in range(nq - 1, -1, -1) for hq in range(d, nq) for c in range(2)]

    s_refs = (s0_ref, s1_ref)

    def finalize(hq):
        lp = lam_ref[...]
        lam = (jnp.exp(jnp.sum(lp[0:1] * lp[1:2], axis=-1, keepdims=True))
               - jnp.exp(jnp.sum(lp[2:3] * lp[3:4], axis=-1, keepdims=True)) + lambda_init)
        a0 = acc_ref[2 * hq]
        a1 = acc_ref[2 * hq + 1]
        inv0 = 1.0 / a0[V_DIM:V_DIM + 1]
        inv1 = lam / a1[V_DIM:V_DIM + 1]
        ot = a0[:V_DIM] * inv0 - a1[:V_DIM] * inv1
        ot = ot * lax.rsqrt(jnp.mean(ot * ot, axis=0, keepdims=True) + LN_EPS)
        gain = sg_ref[...] * (1.0 - lambda_init)
        o_ref[0, :, hq * blk:(hq + 1) * blk] = (ot * gain).astype(BF16)

    def near_phase(extra_scores, final):
        def score_job(j, d, hq, c):
            def run():
                ns_ref[j] = scores(base + d, 2 * hq + c)
            return run

        jobs = [score_job(j, d, hq, c) for j, (d, hq, c) in enumerate(near)] + extra_scores
        ahead = max(NEAR_LOOKAHEAD, len(jobs) - len(near))
        for job in jobs[:ahead]:
            job()
        vts = {}
        for j, (d, hq, c) in enumerate(near):
            if j + ahead < len(jobs):
                jobs[j + ahead]()
            if d not in vts:
                vts[d] = values(base + d)
            bias = "diag" if hq == d else "prev" if hq == d + 1 else None
            update(2 * hq + c, ns_ref[j], vts[d], bias, hq == d)
            if final and d == 0 and c == 1:
                finalize(hq)

    def score_into(kc, slot, cb):
        def run():
            s_refs[slot][cb] = scores(kc, cb, FAR_BLOCKS)
        return run

    def issue_consume(kc, slot, issue_next=True, last=False):
        vt = values(kc, FAR_BLOCKS)
        for cb in range(2 * nq):
            if issue_next:
                score_into(kc + 1, 1 - slot, cb)()
            bias = "prev" if (last and cb < 2) else None
            update(cb, s_refs[slot][cb], vt, bias, False)
            if last and cb % 2 == 1:
                finalize(cb // 2)

    @pl.when(i == 0)
    def _():
        near_phase([], final=True)

    assert nq % (2 * FAR_BLOCKS) == 0, "the pipeline pairs chunks and needs an even count per tile"
    pairs = nq // (2 * FAR_BLOCKS) * i

    @pl.when(i > 0)
    def _():
        near_phase([score_into(0, 0, cb) for cb in range(2 * nq)], final=False)

        def pair(t, carry):
            issue_consume(2 * t, 0)
            issue_consume(2 * t + 1, 1)
            return carry

        lax.fori_loop(0, pairs - 1, pair, 0)
        issue_consume(2 * pairs - 2, 0)
        issue_consume(2 * pairs - 1, 1, issue_next=False, last=True)


def _rel_bucket(rel):
    n = jnp.maximum(rel, 0)
    nf = jnp.maximum(n, 1).astype(F32)
    large = MAX_EXACT + jnp.floor(jnp.log(nf / MAX_EXACT) / math.log(MAX_DISTANCE / MAX_EXACT)
                                  * (NUM_BUCKETS - MAX_EXACT)).astype(jnp.int32)
    return jnp.where(n < MAX_EXACT, n, jnp.minimum(large, NUM_BUCKETS - 1))


def _near_bias(rel_table):
    blk = ATT_BLOCK
    half = blk - MAX_DISTANCE
    table = rel_table.astype(F32)
    table = (table - table[NUM_BUCKETS - 1]) * LOG2_E

    def tile(n, offset):
        rel = jnp.arange(n)[None, :] - jnp.arange(n)[:, None] + offset
        bucket = _rel_bucket(rel)
        bias = jnp.zeros((N_HEADS, n, n), F32)
        for b in range(NUM_BUCKETS):
            bias = jnp.where(bucket[None] == b, table[b][:, None, None], bias)
        return jnp.where(rel[None] >= 0, bias, MASK_VALUE)

    return tile(blk, 0), tile(half, half)


def _attention(q, k, vt, diag_bias, corner_bias, lam_p, subln_g, lambda_init):
    B, S, _ = q.shape
    tq = ATT_TILE
    blk = ATT_BLOCK
    nq = tq // blk
    nchain = 2 * nq
    nnear = nq * (nq + 1)
    H = N_HEADS
    rows = V_DIM + SUM_ROWS
    vmem = (4 * S * V_DIM * 2 + 2 * 2 * blk * blk * 4 + ((2 * FAR_BLOCKS + 1) * nchain + nnear) * blk * blk * 4
            + 2 * nchain * rows * blk * 4 + 4 * tq * V_DIM * 2)
    return pl.pallas_call(
        functools.partial(_attn_kernel, lambda_init),
        grid=(B, H, S // tq),
        in_specs=[
            pl.BlockSpec((1, tq, V_DIM), lambda b, h, i: (b, i, h)),
            pl.BlockSpec((1, S, V_DIM), lambda b, h, i: (b, 0, h)),
            pl.BlockSpec((1, V_DIM, S), lambda b, h, i: (b, h, 0)),
            pl.BlockSpec((1, blk, blk), lambda b, h, i: (h, 0, 0)),
            pl.BlockSpec((1,) + corner_bias.shape[1:], lambda b, h, i: (h, 0, 0)),
            pl.BlockSpec((4, HEAD_DIM), lambda b, h, i: (0, 0)),
            pl.BlockSpec((V_DIM, blk), lambda b, h, i: (0, 0)),
        ],
        out_specs=pl.BlockSpec((1, V_DIM, tq), lambda b, h, i: (b, h, i)),
        out_shape=jax.ShapeDtypeStruct((B, D_ATTN, S), BF16),
        scratch_shapes=[
            pltpu.VMEM((nchain, blk, V_DIM), BF16),
            pltpu.VMEM((nchain, 1, blk), F32),
            pltpu.VMEM((nchain, rows, blk), F32),
            pltpu.VMEM((nchain, FAR_BLOCKS * blk, blk), F32),
            pltpu.VMEM((nchain, FAR_BLOCKS * blk, blk), F32),
            pltpu.VMEM((nnear, blk, blk), F32),
        ],
        compiler_params=_params(("parallel", "parallel", "arbitrary"), vmem),
        name="diff_attn",
    )(q, k, vt, diag_bias, corner_bias, lam_p, subln_g)


def _layer1_kernel(x_ref, a_ref, wo_ref, g0_ref, b0_ref, wg_ref, wu_ref, wd_ref, g1_ref, b1_ref,
                   o_ref, y_ref, xb_ref, acc_ref, h_ref):
    tm = x_ref.shape[1] // HALVES

    def mixer(hf, y_out):
        a_t = a_ref[0, :, hf * tm:(hf + 1) * tm]
        h = lax.dot_general(a_t, wo_ref[...], (((0,), (0,)), ((), ())), preferred_element_type=F32)
        y_out[...] = _layernorm(ALPHA * x_ref[0, hf * tm:(hf + 1) * tm, :] + h, g0_ref[...], b0_ref[...])

    _two_halves(mixer, o_ref, (wg_ref, wu_ref, wd_ref, g1_ref, b1_ref), (y_ref, xb_ref, acc_ref, h_ref))


def _layer1(x, at, wo, g0, b0, wg, wu, wd, g1, b1):
    B, S, D = x.shape
    tm = HALVES * ROW_TILE
    vmem = _ffn_vmem(ROW_TILE) + D * D_ATTN * 2 + 2 * tm * D_ATTN * 2 + 6 * ROW_TILE * D * 4
    return pl.pallas_call(
        _layer1_kernel,
        grid=(B, S // tm),
        in_specs=[
            pl.BlockSpec((1, tm, D), lambda b, i: (b, i, 0)),
            pl.BlockSpec((1, D_ATTN, tm), lambda b, i: (b, 0, i)),
            _resident((D_ATTN, D)), _resident((1, D)), _resident((1, D)),
            _resident((D, D_FF)), _resident((D, D_FF)), _resident((D_FF, D)), _resident((1, D)), _resident((1, D)),
        ],
        out_specs=pl.BlockSpec((1, tm, D), lambda b, i: (b, i, 0)),
        out_shape=jax.ShapeDtypeStruct((B, S, D), F32),
        scratch_shapes=_ffn_scratch(ROW_TILE)[:4],
        compiler_params=_params(("parallel", "parallel"), vmem),
        name="layer1_oproj_ffn",
    )(x, at, wo, g0, b0, wg, wu, wd, g1, b1)


def kernel(x, pool_w, pool_scale, w_qkv, w_o, lam_p, subln_g, rel_table, w_gate, w_up, w_down,
           ln_mix_g, ln_mix_b, ln_ffn_g, ln_ffn_b):
    row = lambda a: a.reshape(1, -1).astype(F32)
    wg, wu, wd = w_gate.astype(F32), w_up.astype(F32), w_down.astype(F32)

    x2 = _layer0(0, x, pool_w[0].astype(BF16), row(pool_scale[0]), row(ln_mix_g[0]), row(ln_mix_b[0]),
                 wg, wu, wd, row(ln_ffn_g[0]), row(ln_ffn_b[0]))

    lambda_init = 0.8 - 0.6 * math.exp(-0.3 * 1)
    q, k, vt, wg1, wu1, wd1 = _qkv(0, x2, w_qkv.astype(F32), 1, wg, wu, wd)
    gain = jnp.broadcast_to(subln_g[0].astype(F32)[:, None], (V_DIM, ATT_BLOCK))
    at = _attention(q, k, vt, *_near_bias(rel_table), lam_p[0].astype(F32), gain, lambda_init)
    return _layer1(x2, at, w_o[0].astype(BF16), row(ln_mix_g[1]), row(ln_mix_b[1]), wg1, wu1, wd1,
                   row(ln_ffn_g[1]), row(ln_ffn_b[1]))
```

```python
import functools
import math

import jax
import jax.numpy as jnp
from jax import lax
from jax.experimental import pallas as pl
from jax.experimental.pallas import tpu as pltpu

D_MODEL = 1024
DEPTH = 2
ALPHA = (2.0 * DEPTH) ** 0.25
LN_EPS = 1e-5
POOL_WINDOWS = (2, 4, 8, 16)
N_GROUPS = len(POOL_WINDOWS)
GROUP_W = D_MODEL // N_GROUPS
POOL_HALO = 16
HEAD_DIM = 64
N_HEADS = D_MODEL // (2 * HEAD_DIM)
V_DIM = 2 * HEAD_DIM
D_ATTN = N_HEADS * V_DIM
NUM_BUCKETS = 32
MAX_EXACT = NUM_BUCKETS // 2
MAX_DISTANCE = 128
D_FF = 2816
SM_SCALE = HEAD_DIM ** -0.5
LOG2_E = math.log2(math.e)

FF_CHUNK = 256
ROW_TILE = 512
HALVES = 2
OVERLAP_AFTER_CHUNK = 1
STAGE_SLOTS = 6
WIDE_STAGE_ROWS = 64
TALL_STAGE_ROWS = 128
ATT_BLOCK = 256
ATT_TILE = 1024
NEAR_LOOKAHEAD = 8
FAR_BLOCKS = 2
SUM_ROWS = 16
MASK_VALUE = -1e30
V7X_VMEM_BYTES = 64 * 1024 * 1024
VMEM_RESERVED_BYTES = 6 * 1024 * 1024
VMEM_TEMP_FACTOR = 1.5

F32 = jnp.float32
BF16 = jnp.bfloat16
NT_DIMS = (((1,), (1,)), ((), ()))


def _params(semantics, vmem_bytes):
    limit = min(int(vmem_bytes * VMEM_TEMP_FACTOR), V7X_VMEM_BYTES - VMEM_RESERVED_BYTES)
    return pltpu.CompilerParams(dimension_semantics=semantics, vmem_limit_bytes=limit)


def _resident(shape):
    return pl.BlockSpec(shape, lambda *_: (0,) * len(shape), pipeline_mode=pl.Buffered(1))


def _layernorm(z, g, b):
    mu = jnp.mean(z, axis=-1, keepdims=True)
    zc = z - mu
    var = jnp.mean(zc * zc, axis=-1, keepdims=True)
    return zc * lax.rsqrt(var + LN_EPS) * g + b


def _ffn_ln(y_ref, wg_ref, wu_ref, wd_ref, g_ref, b_ref, xb_ref, acc_ref, h_ref, overlap=None):
    xb_ref[...] = y_ref[...].astype(BF16)
    nc = wg_ref.shape[1] // FF_CHUNK

    def hidden(c):
        cols = slice(c * FF_CHUNK, (c + 1) * FF_CHUNK)
        xb = xb_ref[...]
        gate = jnp.dot(xb, wg_ref[:, cols], preferred_element_type=F32)
        up = jnp.dot(xb, wu_ref[:, cols], preferred_element_type=F32)
        h_ref[c % 2] = (gate * jax.nn.sigmoid(gate) * up).astype(BF16)

    def down(c):
        rows = slice(c * FF_CHUNK, (c + 1) * FF_CHUNK)
        y = jnp.dot(h_ref[c % 2], wd_ref[rows, :], preferred_element_type=F32)
        if c == 0:
            acc_ref[...] = y
        else:
            acc_ref[...] += y

    hidden(0)
    for c in range(nc):
        if c + 1 < nc:
            hidden(c + 1)
        down(c)
        if overlap is not None and c == OVERLAP_AFTER_CHUNK:
            overlap()
    return _layernorm(ALPHA * y_ref[...] + acc_ref[...], g_ref[...], b_ref[...])


def _two_halves(mixer, o_ref, ffn_refs, scratch):
    y_ref, xb_ref, acc_ref, h_ref = scratch
    tm = y_ref.shape[1]
    mixer(0, y_ref.at[0])
    for h in range(HALVES):
        nxt = functools.partial(mixer, h + 1, y_ref.at[h + 1]) if h + 1 < HALVES else None
        o_ref[0, h * tm:(h + 1) * tm, :] = _ffn_ln(y_ref.at[h], *ffn_refs, xb_ref.at[h], acc_ref.at[h],
                                                   h_ref.at[h], overlap=nxt)


def _fetch_bf16(src, dst_ref, stage_ref, sem_ref):
    nslot, rows = stage_ref.shape[0], stage_ref.shape[1]
    assert src.shape[0] % rows == 0
    n = src.shape[0] // rows

    def chunk(c):
        slot = c % nslot
        return pltpu.make_async_copy(src.at[pl.ds(c * rows, rows), :], stage_ref.at[slot], sem_ref.at[slot])

    for c in range(min(nslot, n)):
        chunk(c).start()
    for c in range(n):
        chunk(c).wait()
        dst_ref[c * rows:(c + 1) * rows, :] = stage_ref[c % nslot].astype(BF16)
        if c + nslot < n:
            chunk(c + nslot).start()


def _load_ffn_weights(layer, hbm_refs, vmem_refs, stage_refs, sem_refs):
    wg_hbm, wu_hbm, wd_hbm = hbm_refs
    wg_ref, wu_ref, wd_ref = vmem_refs
    wide_stage, tall_stage = stage_refs
    wide_sem, tall_sem = sem_refs

    @pl.when(jnp.logical_and(pl.program_id(0) == 0, pl.program_id(1) == 0))
    def _():
        _fetch_bf16(wg_hbm.at[layer], wg_ref, wide_stage, wide_sem)
        _fetch_bf16(wu_hbm.at[layer], wu_ref, wide_stage, wide_sem)
        _fetch_bf16(wd_hbm.at[layer], wd_ref, tall_stage, tall_sem)


def _ffn_specs():
    hbm = pl.BlockSpec(memory_space=pl.ANY)
    return [hbm, hbm, hbm, _resident((1, D_MODEL)), _resident((1, D_MODEL))]


def _ffn_scratch(tm):
    return [pltpu.VMEM((HALVES, tm, D_MODEL), F32), pltpu.VMEM((HALVES, tm, D_MODEL), BF16),
            pltpu.VMEM((HALVES, tm, D_MODEL), F32), pltpu.VMEM((HALVES, 2, tm, FF_CHUNK), BF16),
            pltpu.VMEM((D_MODEL, D_FF), BF16), pltpu.VMEM((D_MODEL, D_FF), BF16), pltpu.VMEM((D_FF, D_MODEL), BF16),
            pltpu.VMEM((STAGE_SLOTS, WIDE_STAGE_ROWS, D_FF), F32),
            pltpu.VMEM((STAGE_SLOTS, TALL_STAGE_ROWS, D_MODEL), F32),
            pltpu.SemaphoreType.DMA((STAGE_SLOTS,)), pltpu.SemaphoreType.DMA((STAGE_SLOTS,))]


def _ffn_vmem(tm):
    rows = HALVES * tm
    stage = STAGE_SLOTS * (WIDE_STAGE_ROWS * D_FF + TALL_STAGE_ROWS * D_MODEL) * 4
    return (3 * D_MODEL * D_FF * 2 + stage + 4 * rows * D_MODEL * 4 + rows * D_MODEL * 10
            + 8 * tm * FF_CHUNK * 4)


def _layer0_kernel(layer, x_ref, halo_ref, pw_ref, ps_ref, g0_ref, b0_ref, wg_hbm, wu_hbm, wd_hbm, g1_ref, b1_ref,
                   fg_ref, fu_ref, fd_ref, o_ref, bg_ref, bu_ref, bd_ref,
                   y_ref, xb_ref, acc_ref, h_ref, wg_ref, wu_ref, wd_ref, wide_stage, tall_stage,
                   wide_sem, tall_sem):
    _load_ffn_weights(layer, (wg_hbm, wu_hbm, wd_hbm), (wg_ref, wu_ref, wd_ref), (wide_stage, tall_stage),
                      (wide_sem, tall_sem))
    bg_ref[...] = fg_ref[0].astype(BF16)
    bu_ref[...] = fu_ref[0].astype(BF16)
    bd_ref[...] = fd_ref[0].astype(BF16)
    s = pl.program_id(1)
    ts = x_ref.shape[1] // HALVES

    def mixer(hf, y_out):
        x = x_ref[0, hf * ts:(hf + 1) * ts, :]
        if hf == 0:
            halo = jnp.where(s > 0, halo_ref[0], 0.0)
        else:
            halo = x_ref[0, hf * ts - POOL_HALO:hf * ts, :]
        xh = jnp.concatenate([halo, x], axis=0)
        t = (s * HALVES + hf) * ts + lax.broadcasted_iota(jnp.int32, (ts, 1), 0)
        mixed = []
        for g, win in enumerate(POOL_WINDOWS):
            a = xh[:, g * GROUP_W:(g + 1) * GROUP_W]
            shift = 1
            while shift < win:
                a = a + pltpu.roll(a, shift, 0)
                shift *= 2
            inv_cnt = 1.0 / jnp.minimum(t + 1, win).astype(F32)
            d = a[POOL_HALO:] * inv_cnt - x[:, g * GROUP_W:(g + 1) * GROUP_W]
            mixed.append(jnp.dot(d.astype(BF16), pw_ref[g], preferred_element_type=F32))
        h = jnp.concatenate(mixed, axis=-1) * ps_ref[...]
        y_out[...] = _layernorm(ALPHA * x + h, g0_ref[...], b0_ref[...])

    _two_halves(mixer, o_ref, (wg_ref, wu_ref, wd_ref, g1_ref, b1_ref), (y_ref, xb_ref, acc_ref, h_ref))


def _layer0(layer, x, pw, ps, g0, b0, wg, wu, wd, g1, b1):
    B, S, D = x.shape
    ts = HALVES * ROW_TILE
    halo_blocks = ts // POOL_HALO
    ns = S // ts
    steps = B * ns
    wide_rows, tall_rows = D // steps, D_FF // steps
    assert wide_rows * steps == D and tall_rows * steps == D_FF and wide_rows % 16 == 0 and tall_rows % 16 == 0
    vmem = _ffn_vmem(ROW_TILE) + 10 * ROW_TILE * D * 4 + 2 * (2 * wide_rows * D_FF + tall_rows * D) * 6
    wide_in = pl.BlockSpec((1, wide_rows, D_FF), lambda b_, s_: (layer + 1, b_ * ns + s_, 0))
    wide_out = pl.BlockSpec((wide_rows, D_FF), lambda b_, s_: (b_ * ns + s_, 0))
    tall_in = pl.BlockSpec((1, tall_rows, D), lambda b_, s_: (layer + 1, b_ * ns + s_, 0))
    tall_out = pl.BlockSpec((tall_rows, D), lambda b_, s_: (b_ * ns + s_, 0))
    return pl.pallas_call(
        functools.partial(_layer0_kernel, layer),
        grid=(B, ns),
        in_specs=[
            pl.BlockSpec((1, ts, D), lambda b_, s_: (b_, s_, 0)),
            pl.BlockSpec((1, POOL_HALO, D), lambda b_, s_: (b_, jnp.maximum(s_ * halo_blocks - 1, 0), 0)),
            _resident((N_GROUPS, GROUP_W, GROUP_W)), _resident((1, D)), _resident((1, D)), _resident((1, D)),
        ] + _ffn_specs() + [wide_in, wide_in, tall_in],
        out_specs=[pl.BlockSpec((1, ts, D), lambda b_, s_: (b_, s_, 0)), wide_out, wide_out, tall_out],
        out_shape=[jax.ShapeDtypeStruct((B, S, D), F32), jax.ShapeDtypeStruct((D, D_FF), BF16),
                   jax.ShapeDtypeStruct((D, D_FF), BF16), jax.ShapeDtypeStruct((D_FF, D), BF16)],
        scratch_shapes=_ffn_scratch(ROW_TILE),
        compiler_params=_params(("arbitrary", "arbitrary"), vmem),
        name="layer0_pool_ffn",
    )(x, x, pw, ps, g0, b0, wg, wu, wd, g1, b1, wg, wu, wd)


def _qkv_kernel(layer, x_ref, w_hbm, q_ref, k_ref, vt_ref, w_ref, stage_ref, sem_ref):
    @pl.when(jnp.logical_and(pl.program_id(0) == 0, pl.program_id(1) == 0))
    def _():
        _fetch_bf16(w_hbm.at[layer], w_ref, stage_ref, sem_ref)

    xb = x_ref[0].astype(BF16)
    q = jnp.dot(xb, w_ref[:, :D_ATTN], preferred_element_type=F32) * (SM_SCALE * LOG2_E)
    q_ref[0] = q.astype(BF16)
    k_ref[0] = jnp.dot(xb, w_ref[:, D_ATTN:2 * D_ATTN], preferred_element_type=F32).astype(BF16)
    vt = lax.dot_general(w_ref[:, 2 * D_ATTN:], xb, (((0,), (1,)), ((), ())), preferred_element_type=F32)
    vt_ref[0] = vt.astype(BF16)


def _qkv(layer, x, w_qkv):
    B, S, D = x.shape
    tm = ROW_TILE
    stage = STAGE_SLOTS * WIDE_STAGE_ROWS * 3 * D_ATTN * 4
    vmem = (3 * D * D_ATTN * 2 + stage + 2 * tm * D * 4 + 6 * tm * D_ATTN * 2 + tm * D * 2
            + 3 * tm * D_ATTN * 4)
    return pl.pallas_call(
        functools.partial(_qkv_kernel, layer),
        grid=(B, S // tm),
        in_specs=[
            pl.BlockSpec((1, tm, D), lambda b, i: (b, i, 0)),
            pl.BlockSpec(memory_space=pl.ANY),
        ],
        out_specs=[
            pl.BlockSpec((1, tm, D_ATTN), lambda b, i: (b, i, 0)),
            pl.BlockSpec((1, tm, D_ATTN), lambda b, i: (b, i, 0)),
            pl.BlockSpec((1, D_ATTN, tm), lambda b, i: (b, 0, i)),
        ],
        out_shape=[
            jax.ShapeDtypeStruct((B, S, D_ATTN), BF16),
            jax.ShapeDtypeStruct((B, S, D_ATTN), BF16),
            jax.ShapeDtypeStruct((B, D_ATTN, S), BF16),
        ],
        scratch_shapes=[pltpu.VMEM((D, 3 * D_ATTN), BF16),
                        pltpu.VMEM((STAGE_SLOTS, WIDE_STAGE_ROWS, 3 * D_ATTN), F32),
                        pltpu.SemaphoreType.DMA((STAGE_SLOTS,))],
        compiler_params=_params(("arbitrary", "arbitrary"), vmem),
        name="qkv_proj",
    )(x, w_qkv)


def _attn_kernel(lambda_init, q_ref, k_ref, vt_ref, diag_ref, corner_ref, lam_ref, sg_ref, o_ref,
                 qs_ref, m_ref, acc_ref, s0_ref, s1_ref, ns_ref):
    i = pl.program_id(2)
    blk = ATT_BLOCK
    nq = q_ref.shape[1] // blk
    half = blk - MAX_DISTANCE
    diag_bias = diag_ref[0]
    prev_corner = corner_ref[0]

    lane = lax.broadcasted_iota(jnp.int32, (blk, V_DIM), 1)
    for hq in range(nq):
        q = q_ref[0, hq * blk:(hq + 1) * blk, :]
        zero = jnp.zeros_like(q)
        qs_ref[2 * hq] = jnp.where(lane < HEAD_DIM, q, zero)
        qs_ref[2 * hq + 1] = jnp.where(lane >= HEAD_DIM, q, zero)

    def scores(kc, cb, nblk=1):
        rows = nblk * blk
        start = pl.multiple_of(kc * rows, rows)
        k = k_ref[0, pl.ds(start, rows), :]
        return lax.dot_general(k, qs_ref[cb], NT_DIMS, preferred_element_type=F32)

    def values(kc, nblk=1):
        rows = nblk * blk
        start = pl.multiple_of(kc * rows, rows)
        vt = vt_ref[0, :, pl.ds(start, rows)]
        return jnp.concatenate([vt, jnp.ones((SUM_ROWS, rows), BF16)], axis=0)

    def update(cb, s, vt, bias, first):
        if bias == "diag":
            s = s + diag_bias
        elif bias == "prev":
            top = s.shape[0] - half
            bottom = jnp.concatenate([s[top:, :half] + prev_corner, s[top:, half:]], axis=1)
            s = jnp.concatenate([s[:top], bottom], axis=0)
        m_new = jnp.max(s, axis=0, keepdims=True)
        if not first:
            m_prev = m_ref[cb]
            m_new = jnp.maximum(m_prev, m_new)
        p = jnp.exp2(s - m_new)
        pv = jnp.dot(vt, p.astype(BF16), preferred_element_type=F32)
        if first:
            acc_ref[cb] = pv
        else:
            acc_ref[cb] = acc_ref[cb] * jnp.exp2(m_prev - m_new) + pv
        m_ref[cb] = m_new

    base = nq * i
    near = [(d, hq, c) for d in range(nq - 1, -1, -1) for hq in range(d, nq) for c in range(2)]

    s_refs = (s0_ref, s1_ref)

    def finalize(hq):
        lp = lam_ref[...]
        lam = (jnp.exp(jnp.sum(lp[0:1] * lp[1:2], axis=-1, keepdims=True))
               - jnp.exp(jnp.sum(lp[2:3] * lp[3:4], axis=-1, keepdims=True)) + lambda_init)
        a0 = acc_ref[2 * hq]
        a1 = acc_ref[2 * hq + 1]
        inv0 = 1.0 / a0[V_DIM:V_DIM + 1]
        inv1 = lam / a1[V_DIM:V_DIM + 1]
        ot = a0[:V_DIM] * inv0 - a1[:V_DIM] * inv1
        ot = ot * lax.rsqrt(jnp.mean(ot * ot, axis=0, keepdims=True) + LN_EPS)
        gain = sg_ref[...] * (1.0 - lambda_init)
        o_ref[0, :, hq * blk:(hq + 1) * blk] = (ot * gain).astype(BF16)

    def near_phase(extra_scores, final):
        def score_job(j, d, hq, c):
            def run():
                ns_ref[j] = scores(base + d, 2 * hq + c)
            return run

        jobs = [score_job(j, d, hq, c) for j, (d, hq, c) in enumerate(near)] + extra_scores
        ahead = max(NEAR_LOOKAHEAD, len(jobs) - len(near))
        for job in jobs[:ahead]:
            job()
        vts = {}
        for j, (d, hq, c) in enumerate(near):
            if j + ahead < len(jobs):
                jobs[j + ahead]()
            if d not in vts:
                vts[d] = values(base + d)
            bias = "diag" if hq == d else "prev" if hq == d + 1 else None
            update(2 * hq + c, ns_ref[j], vts[d], bias, hq == d)
            if final and d == 0 and c == 1:
                finalize(hq)

    def score_into(kc, slot, cb):
        def run():
            s_refs[slot][cb] = scores(kc, cb, FAR_BLOCKS)
        return run

    def issue_consume(kc, slot, issue_next=True, last=False):
        vt = values(kc, FAR_BLOCKS)
        for cb in range(2 * nq):
            if issue_next:
                score_into(kc + 1, 1 - slot, cb)()
            bias = "prev" if (last and cb < 2) else None
            update(cb, s_refs[slot][cb], vt, bias, False)
            if last and cb % 2 == 1:
                finalize(cb // 2)

    @pl.when(i == 0)
    def _():
        near_phase([], final=True)

    assert nq % (2 * FAR_BLOCKS) == 0, "the pipeline pairs chunks and needs an even count per tile"
    pairs = nq // (2 * FAR_BLOCKS) * i

    @pl.when(i > 0)
    def _():
        near_phase([score_into(0, 0, cb) for cb in range(2 * nq)], final=False)

        def pair(t, carry):
            issue_consume(2 * t, 0)
            issue_consume(2 * t + 1, 1)
            return carry

        lax.fori_loop(0, pairs - 1, pair, 0)
        issue_consume(2 * pairs - 2, 0)
        issue_consume(2 * pairs - 1, 1, issue_next=False, last=True)


def _rel_bucket(rel):
    n = jnp.maximum(rel, 0)
    nf = jnp.maximum(n, 1).astype(F32)
    large = MAX_EXACT + jnp.floor(jnp.log(nf / MAX_EXACT) / math.log(MAX_DISTANCE / MAX_EXACT)
                                  * (NUM_BUCKETS - MAX_EXACT)).astype(jnp.int32)
    return jnp.where(n < MAX_EXACT, n, jnp.minimum(large, NUM_BUCKETS - 1))


def _near_bias(rel_table):
    blk = ATT_BLOCK
    half = blk - MAX_DISTANCE
    table = rel_table.astype(F32)
    table = (table - table[NUM_BUCKETS - 1]) * LOG2_E

    def tile(n, offset):
        rel = jnp.arange(n)[None, :] - jnp.arange(n)[:, None] + offset
        bucket = _rel_bucket(rel)
        bias = jnp.zeros((N_HEADS, n, n), F32)
        for b in range(NUM_BUCKETS):
            bias = jnp.where(bucket[None] == b, table[b][:, None, None], bias)
        return jnp.where(rel[None] >= 0, bias, MASK_VALUE)

    return tile(blk, 0), tile(half, half)


def _attention(q, k, vt, diag_bias, corner_bias, lam_p, subln_g, lambda_init):
    B, S, _ = q.shape
    tq = ATT_TILE
    blk = ATT_BLOCK
    nq = tq // blk
    nchain = 2 * nq
    nnear = nq * (nq + 1)
    H = N_HEADS
    rows = V_DIM + SUM_ROWS
    vmem = (4 * S * V_DIM * 2 + 2 * 2 * blk * blk * 4 + ((2 * FAR_BLOCKS + 1) * nchain + nnear) * blk * blk * 4
            + 2 * nchain * rows * blk * 4 + 4 * tq * V_DIM * 2)
    return pl.pallas_call(
        functools.partial(_attn_kernel, lambda_init),
        grid=(B, H, S // tq),
        in_specs=[
            pl.BlockSpec((1, tq, V_DIM), lambda b, h, i: (b, i, h)),
            pl.BlockSpec((1, S, V_DIM), lambda b, h, i: (b, 0, h)),
            pl.BlockSpec((1, V_DIM, S), lambda b, h, i: (b, h, 0)),
            pl.BlockSpec((1, blk, blk), lambda b, h, i: (h, 0, 0)),
            pl.BlockSpec((1,) + corner_bias.shape[1:], lambda b, h, i: (h, 0, 0)),
            pl.BlockSpec((4, HEAD_DIM), lambda b, h, i: (0, 0)),
            pl.BlockSpec((V_DIM, blk), lambda b, h, i: (0, 0)),
        ],
        out_specs=pl.BlockSpec((1, V_DIM, tq), lambda b, h, i: (b, h, i)),
        out_shape=jax.ShapeDtypeStruct((B, D_ATTN, S), BF16),
        scratch_shapes=[
            pltpu.VMEM((nchain, blk, V_DIM), BF16),
            pltpu.VMEM((nchain, 1, blk), F32),
            pltpu.VMEM((nchain, rows, blk), F32),
            pltpu.VMEM((nchain, FAR_BLOCKS * blk, blk), F32),
            pltpu.VMEM((nchain, FAR_BLOCKS * blk, blk), F32),
            pltpu.VMEM((nnear, blk, blk), F32),
        ],
        compiler_params=_params(("parallel", "parallel", "arbitrary"), vmem),
        name="diff_attn",
    )(q, k, vt, diag_bias, corner_bias, lam_p, subln_g)


def _layer1_kernel(x_ref, a_ref, wo_ref, g0_ref, b0_ref, wg_ref, wu_ref, wd_ref, g1_ref, b1_ref,
                   o_ref, y_ref, xb_ref, acc_ref, h_ref):
    tm = x_ref.shape[1] // HALVES

    def mixer(hf, y_out):
        a_t = a_ref[0, :, hf * tm:(hf + 1) * tm]
        h = lax.dot_general(a_t, wo_ref[...], (((0,), (0,)), ((), ())), preferred_element_type=F32)
        y_out[...] = _layernorm(ALPHA * x_ref[0, hf * tm:(hf + 1) * tm, :] + h, g0_ref[...], b0_ref[...])

    _two_halves(mixer, o_ref, (wg_ref, wu_ref, wd_ref, g1_ref, b1_ref), (y_ref, xb_ref, acc_ref, h_ref))


def _layer1(x, at, wo, g0, b0, wg, wu, wd, g1, b1):
    B, S, D = x.shape
    tm = HALVES * ROW_TILE
    vmem = _ffn_vmem(ROW_TILE) + D * D_ATTN * 2 + 2 * tm * D_ATTN * 2 + 6 * ROW_TILE * D * 4
    return pl.pallas_call(
        _layer1_kernel,
        grid=(B, S // tm),
        in_specs=[
            pl.BlockSpec((1, tm, D), lambda b, i: (b, i, 0)),
            pl.BlockSpec((1, D_ATTN, tm), lambda b, i: (b, 0, i)),
            _resident((D_ATTN, D)), _resident((1, D)), _resident((1, D)),
            _resident((D, D_FF)), _resident((D, D_FF)), _resident((D_FF, D)), _resident((1, D)), _resident((1, D)),
        ],
        out_specs=pl.BlockSpec((1, tm, D), lambda b, i: (b, i, 0)),
        out_shape=jax.ShapeDtypeStruct((B, S, D), F32),
        scratch_shapes=_ffn_scratch(ROW_TILE)[:4],
        compiler_params=_params(("parallel", "parallel"), vmem),
        name="layer1_oproj_ffn",
    )(x, at, wo, g0, b0, wg, wu, wd, g1, b1)


def kernel(x, pool_w, pool_scale, w_qkv, w_o, lam_p, subln_g, rel_table, w_gate, w_up, w_down,
           ln_mix_g, ln_mix_b, ln_ffn_g, ln_ffn_b):
    row = lambda a: a.reshape(1, -1).astype(F32)
    wg, wu, wd = w_gate.astype(F32), w_up.astype(F32), w_down.astype(F32)

    x2, wg1, wu1, wd1 = _layer0(0, x, pool_w[0].astype(BF16), row(pool_scale[0]), row(ln_mix_g[0]),
                                row(ln_mix_b[0]), wg, wu, wd, row(ln_ffn_g[0]), row(ln_ffn_b[0]))

    lambda_init = 0.8 - 0.6 * math.exp(-0.3 * 1)
    q, k, vt = _qkv(0, x2, w_qkv.astype(F32))
    gain = jnp.broadcast_to(subln_g[0].astype(F32)[:, None], (V_DIM, ATT_BLOCK))
    at = _attention(q, k, vt, *_near_bias(rel_table), lam_p[0].astype(F32), gain, lambda_init)
    return _layer1(x2, at, w_o[0].astype(BF16), row(ln_mix_g[1]), row(ln_mix_b[1]), wg1, wu1, wd1,
                   row(ln_ffn_g[1]), row(ln_ffn_b[1]))
```

```python
import functools
import math

import jax
import jax.numpy as jnp
from jax import lax
from jax.experimental import pallas as pl
from jax.experimental.pallas import tpu as pltpu

D_MODEL = 1024
DEPTH = 2
ALPHA = (2.0 * DEPTH) ** 0.25
LN_EPS = 1e-5
POOL_WINDOWS = (2, 4, 8, 16)
N_GROUPS = len(POOL_WINDOWS)
GROUP_W = D_MODEL // N_GROUPS
POOL_HALO = 16
HEAD_DIM = 64
N_HEADS = D_MODEL // (2 * HEAD_DIM)
V_DIM = 2 * HEAD_DIM
D_ATTN = N_HEADS * V_DIM
NUM_BUCKETS = 32
MAX_EXACT = NUM_BUCKETS // 2
MAX_DISTANCE = 128
D_FF = 2816
SM_SCALE = HEAD_DIM ** -0.5
LOG2_E = math.log2(math.e)

FF_CHUNK = 256
ROW_TILE = 512
HALVES = 2
OVERLAP_AFTER_CHUNK = 1
STAGE_SLOTS = 6
WIDE_STAGE_ROWS = 64
TALL_STAGE_ROWS = 128
ATT_BLOCK = 256
ATT_TILE = 1024
NEAR_LOOKAHEAD = 8
FAR_BLOCKS = 2
SUM_ROWS = 16
MASK_VALUE = -1e30
V7X_VMEM_BYTES = 64 * 1024 * 1024
VMEM_RESERVED_BYTES = 6 * 1024 * 1024
VMEM_TEMP_FACTOR = 1.5

F32 = jnp.float32
BF16 = jnp.bfloat16
NT_DIMS = (((1,), (1,)), ((), ()))


def _params(semantics, vmem_bytes):
    limit = min(int(vmem_bytes * VMEM_TEMP_FACTOR), V7X_VMEM_BYTES - VMEM_RESERVED_BYTES)
    return pltpu.CompilerParams(dimension_semantics=semantics, vmem_limit_bytes=limit)


def _resident(shape):
    return pl.BlockSpec(shape, lambda *_: (0,) * len(shape), pipeline_mode=pl.Buffered(1))


def _layernorm(z, g, b):
    mu = jnp.mean(z, axis=-1, keepdims=True)
    zc = z - mu
    var = jnp.mean(zc * zc, axis=-1, keepdims=True)
    return zc * lax.rsqrt(var + LN_EPS) * g + b


def _ffn_ln(y_ref, wg_ref, wu_ref, wd_ref, g_ref, b_ref, xb_ref, acc_ref, h_ref, overlap=None):
    xb_ref[...] = y_ref[...].astype(BF16)
    nc = wg_ref.shape[1] // FF_CHUNK

    def hidden(c):
        cols = slice(c * FF_CHUNK, (c + 1) * FF_CHUNK)
        xb = xb_ref[...]
        gate = jnp.dot(xb, wg_ref[:, cols], preferred_element_type=F32)
        up = jnp.dot(xb, wu_ref[:, cols], preferred_element_type=F32)
        h_ref[c % 2] = (gate * jax.nn.sigmoid(gate) * up).astype(BF16)

    def down(c):
        rows = slice(c * FF_CHUNK, (c + 1) * FF_CHUNK)
        y = jnp.dot(h_ref[c % 2], wd_ref[rows, :], preferred_element_type=F32)
        if c == 0:
            acc_ref[...] = y
        else:
            acc_ref[...] += y

    hidden(0)
    for c in range(nc):
        if c + 1 < nc:
            hidden(c + 1)
        down(c)
        if overlap is not None and c == OVERLAP_AFTER_CHUNK:
            overlap()
    return _layernorm(ALPHA * y_ref[...] + acc_ref[...], g_ref[...], b_ref[...])


def _two_halves(mixer, o_ref, ffn_refs, scratch):
    y_ref, xb_ref, acc_ref, h_ref = scratch
    tm = y_ref.shape[1]
    mixer(0, y_ref.at[0])
    for h in range(HALVES):
        nxt = functools.partial(mixer, h + 1, y_ref.at[h + 1]) if h + 1 < HALVES else None
        o_ref[0, h * tm:(h + 1) * tm, :] = _ffn_ln(y_ref.at[h], *ffn_refs, xb_ref.at[h], acc_ref.at[h],
                                                   h_ref.at[h], overlap=nxt)


def _fetch_bf16(src, dst_ref, stage_ref, sem_ref):
    nslot, rows = stage_ref.shape[0], stage_ref.shape[1]
    assert src.shape[0] % rows == 0
    n = src.shape[0] // rows

    def chunk(c):
        slot = c % nslot
        return pltpu.make_async_copy(src.at[pl.ds(c * rows, rows), :], stage_ref.at[slot], sem_ref.at[slot])

    for c in range(min(nslot, n)):
        chunk(c).start()
    for c in range(n):
        chunk(c).wait()
        dst_ref[c * rows:(c + 1) * rows, :] = stage_ref[c % nslot].astype(BF16)
        if c + nslot < n:
            chunk(c + nslot).start()


def _load_ffn_weights(layer, hbm_refs, vmem_refs, stage_refs, sem_refs):
    wg_hbm, wu_hbm, wd_hbm = hbm_refs
    wg_ref, wu_ref, wd_ref = vmem_refs
    wide_stage, tall_stage = stage_refs
    wide_sem, tall_sem = sem_refs

    @pl.when(jnp.logical_and(pl.program_id(0) == 0, pl.program_id(1) == 0))
    def _():
        _fetch_bf16(wg_hbm.at[layer], wg_ref, wide_stage, wide_sem)
        _fetch_bf16(wu_hbm.at[layer], wu_ref, wide_stage, wide_sem)
        _fetch_bf16(wd_hbm.at[layer], wd_ref, tall_stage, tall_sem)


def _ffn_specs():
    hbm = pl.BlockSpec(memory_space=pl.ANY)
    return [hbm, hbm, hbm, _resident((1, D_MODEL)), _resident((1, D_MODEL))]


def _ffn_scratch(tm):
    return [pltpu.VMEM((HALVES, tm, D_MODEL), F32), pltpu.VMEM((HALVES, tm, D_MODEL), BF16),
            pltpu.VMEM((HALVES, tm, D_MODEL), F32), pltpu.VMEM((HALVES, 2, tm, FF_CHUNK), BF16),
            pltpu.VMEM((D_MODEL, D_FF), BF16), pltpu.VMEM((D_MODEL, D_FF), BF16), pltpu.VMEM((D_FF, D_MODEL), BF16),
            pltpu.VMEM((STAGE_SLOTS, WIDE_STAGE_ROWS, D_FF), F32),
            pltpu.VMEM((STAGE_SLOTS, TALL_STAGE_ROWS, D_MODEL), F32),
            pltpu.SemaphoreType.DMA((STAGE_SLOTS,)), pltpu.SemaphoreType.DMA((STAGE_SLOTS,))]


def _ffn_vmem(tm):
    rows = HALVES * tm
    stage = STAGE_SLOTS * (WIDE_STAGE_ROWS * D_FF + TALL_STAGE_ROWS * D_MODEL) * 4
    return (3 * D_MODEL * D_FF * 2 + stage + 4 * rows * D_MODEL * 4 + rows * D_MODEL * 10
            + 8 * tm * FF_CHUNK * 4)


def _layer0_kernel(layer, x_ref, halo_ref, pw_ref, ps_ref, g0_ref, b0_ref, wg_hbm, wu_hbm, wd_hbm, g1_ref, b1_ref,
                   fg_ref, fu_ref, fd_ref, fo_ref, o_ref, bg_ref, bu_ref, bd_ref, bo_ref,
                   y_ref, xb_ref, acc_ref, h_ref, wg_ref, wu_ref, wd_ref, wide_stage, tall_stage,
                   wide_sem, tall_sem):
    _load_ffn_weights(layer, (wg_hbm, wu_hbm, wd_hbm), (wg_ref, wu_ref, wd_ref), (wide_stage, tall_stage),
                      (wide_sem, tall_sem))
    bg_ref[...] = fg_ref[0].astype(BF16)
    bu_ref[...] = fu_ref[0].astype(BF16)
    bd_ref[...] = fd_ref[0].astype(BF16)
    bo_ref[...] = fo_ref[0].astype(BF16)
    s = pl.program_id(1)
    ts = x_ref.shape[1] // HALVES

    def mixer(hf, y_out):
        x = x_ref[0, hf * ts:(hf + 1) * ts, :]
        if hf == 0:
            halo = jnp.where(s > 0, halo_ref[0], 0.0)
        else:
            halo = x_ref[0, hf * ts - POOL_HALO:hf * ts, :]
        xh = jnp.concatenate([halo, x], axis=0)
        t = (s * HALVES + hf) * ts + lax.broadcasted_iota(jnp.int32, (ts, 1), 0)
        mixed = []
        for g, win in enumerate(POOL_WINDOWS):
            a = xh[:, g * GROUP_W:(g + 1) * GROUP_W]
            shift = 1
            while shift < win:
                a = a + pltpu.roll(a, shift, 0)
                shift *= 2
            inv_cnt = 1.0 / jnp.minimum(t + 1, win).astype(F32)
            d = a[POOL_HALO:] * inv_cnt - x[:, g * GROUP_W:(g + 1) * GROUP_W]
            mixed.append(jnp.dot(d.astype(BF16), pw_ref[g], preferred_element_type=F32))
        h = jnp.concatenate(mixed, axis=-1) * ps_ref[...]
        y_out[...] = _layernorm(ALPHA * x + h, g0_ref[...], b0_ref[...])

    _two_halves(mixer, o_ref, (wg_ref, wu_ref, wd_ref, g1_ref, b1_ref), (y_ref, xb_ref, acc_ref, h_ref))


def _layer0(layer, x, pw, ps, g0, b0, wg, wu, wd, g1, b1, wo):
    B, S, D = x.shape
    ts = HALVES * ROW_TILE
    halo_blocks = ts // POOL_HALO
    ns = S // ts
    steps = B * ns
    wide_rows, tall_rows = D // steps, D_FF // steps
    assert wide_rows * steps == D and tall_rows * steps == D_FF and wide_rows % 16 == 0 and tall_rows % 16 == 0
    vmem = _ffn_vmem(ROW_TILE) + 10 * ROW_TILE * D * 4 + 2 * (2 * wide_rows * D_FF + tall_rows * D) * 6
    wide_in = pl.BlockSpec((1, wide_rows, D_FF), lambda b_, s_: (layer + 1, b_ * ns + s_, 0))
    wide_out = pl.BlockSpec((wide_rows, D_FF), lambda b_, s_: (b_ * ns + s_, 0))
    tall_in = pl.BlockSpec((1, tall_rows, D), lambda b_, s_: (layer + 1, b_ * ns + s_, 0))
    tall_out = pl.BlockSpec((tall_rows, D), lambda b_, s_: (b_ * ns + s_, 0))
    return pl.pallas_call(
        functools.partial(_layer0_kernel, layer),
        grid=(B, ns),
        in_specs=[
            pl.BlockSpec((1, ts, D), lambda b_, s_: (b_, s_, 0)),
            pl.BlockSpec((1, POOL_HALO, D), lambda b_, s_: (b_, jnp.maximum(s_ * halo_blocks - 1, 0), 0)),
            _resident((N_GROUPS, GROUP_W, GROUP_W)), _resident((1, D)), _resident((1, D)), _resident((1, D)),
        ] + _ffn_specs() + [wide_in, wide_in, tall_in,
                            pl.BlockSpec((1, wide_rows, D), lambda b_, s_: (0, b_ * ns + s_, 0))],
        out_specs=[pl.BlockSpec((1, ts, D), lambda b_, s_: (b_, s_, 0)), wide_out, wide_out, tall_out,
                   pl.BlockSpec((wide_rows, D), lambda b_, s_: (b_ * ns + s_, 0))],
        out_shape=[jax.ShapeDtypeStruct((B, S, D), F32), jax.ShapeDtypeStruct((D, D_FF), BF16),
                   jax.ShapeDtypeStruct((D, D_FF), BF16), jax.ShapeDtypeStruct((D_FF, D), BF16),
                   jax.ShapeDtypeStruct((D_ATTN, D), BF16)],
        scratch_shapes=_ffn_scratch(ROW_TILE),
        compiler_params=_params(("arbitrary", "arbitrary"), vmem),
        name="layer0_pool_ffn",
    )(x, x, pw, ps, g0, b0, wg, wu, wd, g1, b1, wg, wu, wd, wo)


def _qkv_kernel(layer, x_ref, w_hbm, q_ref, k_ref, vt_ref, w_ref, stage_ref, sem_ref):
    @pl.when(jnp.logical_and(pl.program_id(0) == 0, pl.program_id(1) == 0))
    def _():
        _fetch_bf16(w_hbm.at[layer], w_ref, stage_ref, sem_ref)

    xb = x_ref[0].astype(BF16)
    q = jnp.dot(xb, w_ref[:, :D_ATTN], preferred_element_type=F32) * (SM_SCALE * LOG2_E)
    q_ref[0] = q.astype(BF16)
    k_ref[0] = jnp.dot(xb, w_ref[:, D_ATTN:2 * D_ATTN], preferred_element_type=F32).astype(BF16)
    vt = lax.dot_general(w_ref[:, 2 * D_ATTN:], xb, (((0,), (1,)), ((), ())), preferred_element_type=F32)
    vt_ref[0] = vt.astype(BF16)


def _qkv(layer, x, w_qkv):
    B, S, D = x.shape
    tm = ROW_TILE
    stage = STAGE_SLOTS * WIDE_STAGE_ROWS * 3 * D_ATTN * 4
    vmem = (3 * D * D_ATTN * 2 + stage + 2 * tm * D * 4 + 6 * tm * D_ATTN * 2 + tm * D * 2
            + 3 * tm * D_ATTN * 4)
    return pl.pallas_call(
        functools.partial(_qkv_kernel, layer),
        grid=(B, S // tm),
        in_specs=[
            pl.BlockSpec((1, tm, D), lambda b, i: (b, i, 0)),
            pl.BlockSpec(memory_space=pl.ANY),
        ],
        out_specs=[
            pl.BlockSpec((1, tm, D_ATTN), lambda b, i: (b, i, 0)),
            pl.BlockSpec((1, tm, D_ATTN), lambda b, i: (b, i, 0)),
            pl.BlockSpec((1, D_ATTN, tm), lambda b, i: (b, 0, i)),
        ],
        out_shape=[
            jax.ShapeDtypeStruct((B, S, D_ATTN), BF16),
            jax.ShapeDtypeStruct((B, S, D_ATTN), BF16),
            jax.ShapeDtypeStruct((B, D_ATTN, S), BF16),
        ],
        scratch_shapes=[pltpu.VMEM((D, 3 * D_ATTN), BF16),
                        pltpu.VMEM((STAGE_SLOTS, WIDE_STAGE_ROWS, 3 * D_ATTN), F32),
                        pltpu.SemaphoreType.DMA((STAGE_SLOTS,))],
        compiler_params=_params(("arbitrary", "arbitrary"), vmem),
        name="qkv_proj",
    )(x, w_qkv)


def _attn_kernel(lambda_init, q_ref, k_ref, vt_ref, diag_ref, corner_ref, lam_ref, sg_ref, o_ref,
                 qs_ref, m_ref, acc_ref, s0_ref, s1_ref, ns_ref):
    i = pl.program_id(2)
    blk = ATT_BLOCK
    nq = q_ref.shape[1] // blk
    half = blk - MAX_DISTANCE
    diag_bias = diag_ref[0]
    prev_corner = corner_ref[0]

    lane = lax.broadcasted_iota(jnp.int32, (blk, V_DIM), 1)
    for hq in range(nq):
        q = q_ref[0, hq * blk:(hq + 1) * blk, :]
        zero = jnp.zeros_like(q)
        qs_ref[2 * hq] = jnp.where(lane < HEAD_DIM, q, zero)
        qs_ref[2 * hq + 1] = jnp.where(lane >= HEAD_DIM, q, zero)

    def scores(kc, cb, nblk=1):
        rows = nblk * blk
        start = pl.multiple_of(kc * rows, rows)
        k = k_ref[0, pl.ds(start, rows), :]
        return lax.dot_general(k, qs_ref[cb], NT_DIMS, preferred_element_type=F32)

    def values(kc, nblk=1):
        rows = nblk * blk
        start = pl.multiple_of(kc * rows, rows)
        vt = vt_ref[0, :, pl.ds(start, rows)]
        return jnp.concatenate([vt, jnp.ones((SUM_ROWS, rows), BF16)], axis=0)

    def update(cb, s, vt, bias, first):
        if bias == "diag":
            s = s + diag_bias
        elif bias == "prev":
            top = s.shape[0] - half
            bottom = jnp.concatenate([s[top:, :half] + prev_corner, s[top:, half:]], axis=1)
            s = jnp.concatenate([s[:top], bottom], axis=0)
        m_new = jnp.max(s, axis=0, keepdims=True)
        if not first:
            m_prev = m_ref[cb]
            m_new = jnp.maximum(m_prev, m_new)
        p = jnp.exp2(s - m_new)
        pv = jnp.dot(vt, p.astype(BF16), preferred_element_type=F32)
        if first:
            acc_ref[cb] = pv
        else:
            acc_ref[cb] = acc_ref[cb] * jnp.exp2(m_prev - m_new) + pv
        m_ref[cb] = m_new

    base = nq * i
    near = [(d, hq, c) for d in range(nq - 1, -1, -1) for hq in range(d, nq) for c in range(2)]

    s_refs = (s0_ref, s1_ref)

    def finalize(hq):
        lp = lam_ref[...]
        lam = (jnp.exp(jnp.sum(lp[0:1] * lp[1:2], axis=-1, keepdims=True))
               - jnp.exp(jnp.sum(lp[2:3] * lp[3:4], axis=-1, keepdims=True)) + lambda_init)
        a0 = acc_ref[2 * hq]
        a1 = acc_ref[2 * hq + 1]
        inv0 = 1.0 / a0[V_DIM:V_DIM + 1]
        inv1 = lam / a1[V_DIM:V_DIM + 1]
        ot = a0[:V_DIM] * inv0 - a1[:V_DIM] * inv1
        ot = ot * lax.rsqrt(jnp.mean(ot * ot, axis=0, keepdims=True) + LN_EPS)
        gain = sg_ref[...] * (1.0 - lambda_init)
        o_ref[0, :, hq * blk:(hq + 1) * blk] = (ot * gain).astype(BF16)

    def near_phase(extra_scores, final):
        def score_job(j, d, hq, c):
            def run():
                ns_ref[j] = scores(base + d, 2 * hq + c)
            return run

        jobs = [score_job(j, d, hq, c) for j, (d, hq, c) in enumerate(near)] + extra_scores
        ahead = max(NEAR_LOOKAHEAD, len(jobs) - len(near))
        for job in jobs[:ahead]:
            job()
        vts = {}
        for j, (d, hq, c) in enumerate(near):
            if j + ahead < len(jobs):
                jobs[j + ahead]()
            if d not in vts:
                vts[d] = values(base + d)
            bias = "diag" if hq == d else "prev" if hq == d + 1 else None
            update(2 * hq + c, ns_ref[j], vts[d], bias, hq == d)
            if final and d == 0 and c == 1:
                finalize(hq)

    def score_into(kc, slot, cb):
        def run():
            s_refs[slot][cb] = scores(kc, cb, FAR_BLOCKS)
        return run

    def issue_consume(kc, slot, issue_next=True, last=False):
        vt = values(kc, FAR_BLOCKS)
        for cb in range(2 * nq):
            if issue_next:
                score_into(kc + 1, 1 - slot, cb)()
            bias = "prev" if (last and cb < 2) else None
            update(cb, s_refs[slot][cb], vt, bias, False)
            if last and cb % 2 == 1:
                finalize(cb // 2)

    @pl.when(i == 0)
    def _():
        near_phase([], final=True)

    assert nq % (2 * FAR_BLOCKS) == 0, "the pipeline pairs chunks and needs an even count per tile"
    pairs = nq // (2 * FAR_BLOCKS) * i

    @pl.when(i > 0)
    def _():
        near_phase([score_into(0, 0, cb) for cb in range(2 * nq)], final=False)

        def pair(t, carry):
            issue_consume(2 * t, 0)
            issue_consume(2 * t + 1, 1)
            return carry

        lax.fori_loop(0, pairs - 1, pair, 0)
        issue_consume(2 * pairs - 2, 0)
        issue_consume(2 * pairs - 1, 1, issue_next=False, last=True)


def _rel_bucket(rel):
    n = jnp.maximum(rel, 0)
    nf = jnp.maximum(n, 1).astype(F32)
    large = MAX_EXACT + jnp.floor(jnp.log(nf / MAX_EXACT) / math.log(MAX_DISTANCE / MAX_EXACT)
                                  * (NUM_BUCKETS - MAX_EXACT)).astype(jnp.int32)
    return jnp.where(n < MAX_EXACT, n, jnp.minimum(large, NUM_BUCKETS - 1))


def _near_bias(rel_table):
    blk = ATT_BLOCK
    half = blk - MAX_DISTANCE
    table = rel_table.astype(F32)
    table = (table - table[NUM_BUCKETS - 1]) * LOG2_E

    def tile(n, offset):
        rel = jnp.arange(n)[None, :] - jnp.arange(n)[:, None] + offset
        bucket = _rel_bucket(rel)
        bias = jnp.zeros((N_HEADS, n, n), F32)
        for b in range(NUM_BUCKETS):
            bias = jnp.where(bucket[None] == b, table[b][:, None, None], bias)
        return jnp.where(rel[None] >= 0, bias, MASK_VALUE)

    return tile(blk, 0), tile(half, half)


def _attention(q, k, vt, diag_bias, corner_bias, lam_p, subln_g, lambda_init):
    B, S, _ = q.shape
    tq = ATT_TILE
    blk = ATT_BLOCK
    nq = tq // blk
    nchain = 2 * nq
    nnear = nq * (nq + 1)
    H = N_HEADS
    rows = V_DIM + SUM_ROWS
    vmem = (4 * S * V_DIM * 2 + 2 * 2 * blk * blk * 4 + ((2 * FAR_BLOCKS + 1) * nchain + nnear) * blk * blk * 4
            + 2 * nchain * rows * blk * 4 + 4 * tq * V_DIM * 2)
    return pl.pallas_call(
        functools.partial(_attn_kernel, lambda_init),
        grid=(B, H, S // tq),
        in_specs=[
            pl.BlockSpec((1, tq, V_DIM), lambda b, h, i: (b, i, h)),
            pl.BlockSpec((1, S, V_DIM), lambda b, h, i: (b, 0, h)),
            pl.BlockSpec((1, V_DIM, S), lambda b, h, i: (b, h, 0)),
            pl.BlockSpec((1, blk, blk), lambda b, h, i: (h, 0, 0)),
            pl.BlockSpec((1,) + corner_bias.shape[1:], lambda b, h, i: (h, 0, 0)),
            pl.BlockSpec((4, HEAD_DIM), lambda b, h, i: (0, 0)),
            pl.BlockSpec((V_DIM, blk), lambda b, h, i: (0, 0)),
        ],
        out_specs=pl.BlockSpec((1, V_DIM, tq), lambda b, h, i: (b, h, i)),
        out_shape=jax.ShapeDtypeStruct((B, D_ATTN, S), BF16),
        scratch_shapes=[
            pltpu.VMEM((nchain, blk, V_DIM), BF16),
            pltpu.VMEM((nchain, 1, blk), F32),
            pltpu.VMEM((nchain, rows, blk), F32),
            pltpu.VMEM((nchain, FAR_BLOCKS * blk, blk), F32),
            pltpu.VMEM((nchain, FAR_BLOCKS * blk, blk), F32),
            pltpu.VMEM((nnear, blk, blk), F32),
        ],
        compiler_params=_params(("parallel", "parallel", "arbitrary"), vmem),
        name="diff_attn",
    )(q, k, vt, diag_bias, corner_bias, lam_p, subln_g)


def _layer1_kernel(x_ref, a_ref, wo_ref, g0_ref, b0_ref, wg_ref, wu_ref, wd_ref, g1_ref, b1_ref,
                   o_ref, y_ref, xb_ref, acc_ref, h_ref):
    tm = x_ref.shape[1] // HALVES

    def mixer(hf, y_out):
        a_t = a_ref[0, :, hf * tm:(hf + 1) * tm]
        h = lax.dot_general(a_t, wo_ref[...], (((0,), (0,)), ((), ())), preferred_element_type=F32)
        y_out[...] = _layernorm(ALPHA * x_ref[0, hf * tm:(hf + 1) * tm, :] + h, g0_ref[...], b0_ref[...])

    _two_halves(mixer, o_ref, (wg_ref, wu_ref, wd_ref, g1_ref, b1_ref), (y_ref, xb_ref, acc_ref, h_ref))


def _layer1(x, at, wo, g0, b0, wg, wu, wd, g1, b1):
    B, S, D = x.shape
    tm = HALVES * ROW_TILE
    vmem = _ffn_vmem(ROW_TILE) + D * D_ATTN * 2 + 2 * tm * D_ATTN * 2 + 6 * ROW_TILE * D * 4
    return pl.pallas_call(
        _layer1_kernel,
        grid=(B, S // tm),
        in_specs=[
            pl.BlockSpec((1, tm, D), lambda b, i: (b, i, 0)),
            pl.BlockSpec((1, D_ATTN, tm), lambda b, i: (b, 0, i)),
            _resident((D_ATTN, D)), _resident((1, D)), _resident((1, D)),
            _resident((D, D_FF)), _resident((D, D_FF)), _resident((D_FF, D)), _resident((1, D)), _resident((1, D)),
        ],
        out_specs=pl.BlockSpec((1, tm, D), lambda b, i: (b, i, 0)),
        out_shape=jax.ShapeDtypeStruct((B, S, D), F32),
        scratch_shapes=_ffn_scratch(ROW_TILE)[:4],
        compiler_params=_params(("parallel", "parallel"), vmem),
        name="layer1_oproj_ffn",
    )(x, at, wo, g0, b0, wg, wu, wd, g1, b1)


def kernel(x, pool_w, pool_scale, w_qkv, w_o, lam_p, subln_g, rel_table, w_gate, w_up, w_down,
           ln_mix_g, ln_mix_b, ln_ffn_g, ln_ffn_b):
    row = lambda a: a.reshape(1, -1).astype(F32)
    wg, wu, wd = w_gate.astype(F32), w_up.astype(F32), w_down.astype(F32)

    x2, wg1, wu1, wd1, wo1 = _layer0(0, x, pool_w[0].astype(BF16), row(pool_scale[0]), row(ln_mix_g[0]),
                                     row(ln_mix_b[0]), wg, wu, wd, row(ln_ffn_g[0]), row(ln_ffn_b[0]),
                                     w_o.astype(F32))

    lambda_init = 0.8 - 0.6 * math.exp(-0.3 * 1)
    q, k, vt = _qkv(0, x2, w_qkv.astype(F32))
    gain = jnp.broadcast_to(subln_g[0].astype(F32)[:, None], (V_DIM, ATT_BLOCK))
    at = _attention(q, k, vt, *_near_bias(rel_table), lam_p[0].astype(F32), gain, lambda_init)
    return _layer1(x2, at, wo1, row(ln_mix_g[1]), row(ln_mix_b[1]), wg1, wu1, wd1,
                   row(ln_ffn_g[1]), row(ln_ffn_b[1]))
```

```python
import functools
import math

import jax
import jax.numpy as jnp
from jax import lax
from jax.experimental import pallas as pl
from jax.experimental.pallas import tpu as pltpu

D_MODEL = 1024
DEPTH = 2
ALPHA = (2.0 * DEPTH) ** 0.25
LN_EPS = 1e-5
POOL_WINDOWS = (2, 4, 8, 16)
N_GROUPS = len(POOL_WINDOWS)
GROUP_W = D_MODEL // N_GROUPS
POOL_HALO = 16
HEAD_DIM = 64
N_HEADS = D_MODEL // (2 * HEAD_DIM)
V_DIM = 2 * HEAD_DIM
D_ATTN = N_HEADS * V_DIM
NUM_BUCKETS = 32
MAX_EXACT = NUM_BUCKETS // 2
MAX_DISTANCE = 128
D_FF = 2816
SM_SCALE = HEAD_DIM ** -0.5
LOG2_E = math.log2(math.e)

FF_CHUNK = 256
ROW_TILE = 512
HALVES = 2
OVERLAP_AFTER_CHUNK = 1
STAGE_SLOTS = 6
WIDE_STAGE_ROWS = 64
TALL_STAGE_ROWS = 128
ATT_BLOCK = 256
ATT_TILE = 1024
NEAR_LOOKAHEAD = 8
FAR_BLOCKS = 2
SUM_ROWS = 16
MASK_VALUE = -1e30
V7X_VMEM_BYTES = 64 * 1024 * 1024
VMEM_RESERVED_BYTES = 6 * 1024 * 1024
VMEM_TEMP_FACTOR = 1.5

F32 = jnp.float32
BF16 = jnp.bfloat16
NT_DIMS = (((1,), (1,)), ((), ()))


def _params(semantics, vmem_bytes):
    limit = min(int(vmem_bytes * VMEM_TEMP_FACTOR), V7X_VMEM_BYTES - VMEM_RESERVED_BYTES)
    return pltpu.CompilerParams(dimension_semantics=semantics, vmem_limit_bytes=limit)


def _resident(shape):
    return pl.BlockSpec(shape, lambda *_: (0,) * len(shape), pipeline_mode=pl.Buffered(1))


def _layernorm(z, g, b):
    mu = jnp.mean(z, axis=-1, keepdims=True)
    zc = z - mu
    var = jnp.mean(zc * zc, axis=-1, keepdims=True)
    return zc * lax.rsqrt(var + LN_EPS) * g + b


def _ffn_ln(y_ref, wg_ref, wu_ref, wd_ref, g_ref, b_ref, xb_ref, acc_ref, h_ref, overlap=None):
    xb_ref[...] = y_ref[...].astype(BF16)
    nc = wg_ref.shape[1] // FF_CHUNK

    def hidden(c):
        cols = slice(c * FF_CHUNK, (c + 1) * FF_CHUNK)
        xb = xb_ref[...]
        gate = jnp.dot(xb, wg_ref[:, cols], preferred_element_type=F32)
        up = jnp.dot(xb, wu_ref[:, cols], preferred_element_type=F32)
        h_ref[c % 2] = (gate * jax.nn.sigmoid(gate) * up).astype(BF16)

    def down(c):
        rows = slice(c * FF_CHUNK, (c + 1) * FF_CHUNK)
        y = jnp.dot(h_ref[c % 2], wd_ref[rows, :], preferred_element_type=F32)
        if c == 0:
            acc_ref[...] = y
        else:
            acc_ref[...] += y

    hidden(0)
    for c in range(nc):
        if c + 1 < nc:
            hidden(c + 1)
        down(c)
        if overlap is not None and c == OVERLAP_AFTER_CHUNK:
            overlap()
    return _layernorm(ALPHA * y_ref[...] + acc_ref[...], g_ref[...], b_ref[...])


def _two_halves(mixer, o_ref, ffn_refs, scratch):
    y_ref, xb_ref, acc_ref, h_ref = scratch
    tm = y_ref.shape[1]
    mixer(0, y_ref.at[0])
    for h in range(HALVES):
        nxt = functools.partial(mixer, h + 1, y_ref.at[h + 1]) if h + 1 < HALVES else None
        o_ref[0, h * tm:(h + 1) * tm, :] = _ffn_ln(y_ref.at[h], *ffn_refs, xb_ref.at[h], acc_ref.at[h],
                                                   h_ref.at[h], overlap=nxt)


def _fetch_bf16(src, dst_ref, stage_ref, sem_ref):
    nslot, rows = stage_ref.shape[0], stage_ref.shape[1]
    assert src.shape[0] % rows == 0
    n = src.shape[0] // rows

    def chunk(c):
        slot = c % nslot
        return pltpu.make_async_copy(src.at[pl.ds(c * rows, rows), :], stage_ref.at[slot], sem_ref.at[slot])

    for c in range(min(nslot, n)):
        chunk(c).start(priority=c % 2)
    for c in range(n):
        chunk(c).wait()
        dst_ref[c * rows:(c + 1) * rows, :] = stage_ref[c % nslot].astype(BF16)
        if c + nslot < n:
            chunk(c + nslot).start(priority=(c + nslot) % 2)


def _load_ffn_weights(layer, hbm_refs, vmem_refs, stage_refs, sem_refs):
    wg_hbm, wu_hbm, wd_hbm = hbm_refs
    wg_ref, wu_ref, wd_ref = vmem_refs
    wide_stage, tall_stage = stage_refs
    wide_sem, tall_sem = sem_refs

    @pl.when(jnp.logical_and(pl.program_id(0) == 0, pl.program_id(1) == 0))
    def _():
        _fetch_bf16(wg_hbm.at[layer], wg_ref, wide_stage, wide_sem)
        _fetch_bf16(wu_hbm.at[layer], wu_ref, wide_stage, wide_sem)
        _fetch_bf16(wd_hbm.at[layer], wd_ref, tall_stage, tall_sem)


def _ffn_specs():
    hbm = pl.BlockSpec(memory_space=pl.ANY)
    return [hbm, hbm, hbm, _resident((1, D_MODEL)), _resident((1, D_MODEL))]


def _ffn_scratch(tm):
    return [pltpu.VMEM((HALVES, tm, D_MODEL), F32), pltpu.VMEM((HALVES, tm, D_MODEL), BF16),
            pltpu.VMEM((HALVES, tm, D_MODEL), F32), pltpu.VMEM((HALVES, 2, tm, FF_CHUNK), BF16),
            pltpu.VMEM((D_MODEL, D_FF), BF16), pltpu.VMEM((D_MODEL, D_FF), BF16), pltpu.VMEM((D_FF, D_MODEL), BF16),
            pltpu.VMEM((STAGE_SLOTS, WIDE_STAGE_ROWS, D_FF), F32),
            pltpu.VMEM((STAGE_SLOTS, TALL_STAGE_ROWS, D_MODEL), F32),
            pltpu.SemaphoreType.DMA((STAGE_SLOTS,)), pltpu.SemaphoreType.DMA((STAGE_SLOTS,))]


def _ffn_vmem(tm):
    rows = HALVES * tm
    stage = STAGE_SLOTS * (WIDE_STAGE_ROWS * D_FF + TALL_STAGE_ROWS * D_MODEL) * 4
    return (3 * D_MODEL * D_FF * 2 + stage + 4 * rows * D_MODEL * 4 + rows * D_MODEL * 10
            + 8 * tm * FF_CHUNK * 4)


def _layer0_kernel(layer, x_ref, halo_ref, pw_ref, ps_ref, g0_ref, b0_ref, wg_hbm, wu_hbm, wd_hbm, g1_ref, b1_ref,
                   fg_ref, fu_ref, fd_ref, o_ref, bg_ref, bu_ref, bd_ref,
                   y_ref, xb_ref, acc_ref, h_ref, wg_ref, wu_ref, wd_ref, wide_stage, tall_stage,
                   wide_sem, tall_sem):
    _load_ffn_weights(layer, (wg_hbm, wu_hbm, wd_hbm), (wg_ref, wu_ref, wd_ref), (wide_stage, tall_stage),
                      (wide_sem, tall_sem))
    bg_ref[...] = fg_ref[0].astype(BF16)
    bu_ref[...] = fu_ref[0].astype(BF16)
    bd_ref[...] = fd_ref[0].astype(BF16)
    s = pl.program_id(1)
    ts = x_ref.shape[1] // HALVES

    def mixer(hf, y_out):
        x = x_ref[0, hf * ts:(hf + 1) * ts, :]
        if hf == 0:
            halo = jnp.where(s > 0, halo_ref[0], 0.0)
        else:
            halo = x_ref[0, hf * ts - POOL_HALO:hf * ts, :]
        xh = jnp.concatenate([halo, x], axis=0)
        t = (s * HALVES + hf) * ts + lax.broadcasted_iota(jnp.int32, (ts, 1), 0)
        mixed = []
        for g, win in enumerate(POOL_WINDOWS):
            a = xh[:, g * GROUP_W:(g + 1) * GROUP_W]
            shift = 1
            while shift < win:
                a = a + pltpu.roll(a, shift, 0)
                shift *= 2
            inv_cnt = 1.0 / jnp.minimum(t + 1, win).astype(F32)
            d = a[POOL_HALO:] * inv_cnt - x[:, g * GROUP_W:(g + 1) * GROUP_W]
            mixed.append(jnp.dot(d.astype(BF16), pw_ref[g], preferred_element_type=F32))
        h = jnp.concatenate(mixed, axis=-1) * ps_ref[...]
        y_out[...] = _layernorm(ALPHA * x + h, g0_ref[...], b0_ref[...])

    _two_halves(mixer, o_ref, (wg_ref, wu_ref, wd_ref, g1_ref, b1_ref), (y_ref, xb_ref, acc_ref, h_ref))


def _layer0(layer, x, pw, ps, g0, b0, wg, wu, wd, g1, b1):
    B, S, D = x.shape
    ts = HALVES * ROW_TILE
    halo_blocks = ts // POOL_HALO
    ns = S // ts
    steps = B * ns
    wide_rows, tall_rows = D // steps, D_FF // steps
    assert wide_rows * steps == D and tall_rows * steps == D_FF and wide_rows % 16 == 0 and tall_rows % 16 == 0
    vmem = _ffn_vmem(ROW_TILE) + 10 * ROW_TILE * D * 4 + 2 * (2 * wide_rows * D_FF + tall_rows * D) * 6
    wide_in = pl.BlockSpec((1, wide_rows, D_FF), lambda b_, s_: (layer + 1, b_ * ns + s_, 0))
    wide_out = pl.BlockSpec((wide_rows, D_FF), lambda b_, s_: (b_ * ns + s_, 0))
    tall_in = pl.BlockSpec((1, tall_rows, D), lambda b_, s_: (layer + 1, b_ * ns + s_, 0))
    tall_out = pl.BlockSpec((tall_rows, D), lambda b_, s_: (b_ * ns + s_, 0))
    return pl.pallas_call(
        functools.partial(_layer0_kernel, layer),
        grid=(B, ns),
        in_specs=[
            pl.BlockSpec((1, ts, D), lambda b_, s_: (b_, s_, 0)),
            pl.BlockSpec((1, POOL_HALO, D), lambda b_, s_: (b_, jnp.maximum(s_ * halo_blocks - 1, 0), 0)),
            _resident((N_GROUPS, GROUP_W, GROUP_W)), _resident((1, D)), _resident((1, D)), _resident((1, D)),
        ] + _ffn_specs() + [wide_in, wide_in, tall_in],
        out_specs=[pl.BlockSpec((1, ts, D), lambda b_, s_: (b_, s_, 0)), wide_out, wide_out, tall_out],
        out_shape=[jax.ShapeDtypeStruct((B, S, D), F32), jax.ShapeDtypeStruct((D, D_FF), BF16),
                   jax.ShapeDtypeStruct((D, D_FF), BF16), jax.ShapeDtypeStruct((D_FF, D), BF16)],
        scratch_shapes=_ffn_scratch(ROW_TILE),
        compiler_params=_params(("arbitrary", "arbitrary"), vmem),
        name="layer0_pool_ffn",
    )(x, x, pw, ps, g0, b0, wg, wu, wd, g1, b1, wg, wu, wd)


def _qkv_kernel(layer, x_ref, w_hbm, q_ref, k_ref, vt_ref, w_ref, stage_ref, sem_ref):
    @pl.when(jnp.logical_and(pl.program_id(0) == 0, pl.program_id(1) == 0))
    def _():
        _fetch_bf16(w_hbm.at[layer], w_ref, stage_ref, sem_ref)

    xb = x_ref[0].astype(BF16)
    q = jnp.dot(xb, w_ref[:, :D_ATTN], preferred_element_type=F32) * (SM_SCALE * LOG2_E)
    q_ref[0] = q.astype(BF16)
    k_ref[0] = jnp.dot(xb, w_ref[:, D_ATTN:2 * D_ATTN], preferred_element_type=F32).astype(BF16)
    vt = lax.dot_general(w_ref[:, 2 * D_ATTN:], xb, (((0,), (1,)), ((), ())), preferred_element_type=F32)
    vt_ref[0] = vt.astype(BF16)


def _qkv(layer, x, w_qkv):
    B, S, D = x.shape
    tm = ROW_TILE
    stage = STAGE_SLOTS * WIDE_STAGE_ROWS * 3 * D_ATTN * 4
    vmem = (3 * D * D_ATTN * 2 + stage + 2 * tm * D * 4 + 6 * tm * D_ATTN * 2 + tm * D * 2
            + 3 * tm * D_ATTN * 4)
    return pl.pallas_call(
        functools.partial(_qkv_kernel, layer),
        grid=(B, S // tm),
        in_specs=[
            pl.BlockSpec((1, tm, D), lambda b, i: (b, i, 0)),
            pl.BlockSpec(memory_space=pl.ANY),
        ],
        out_specs=[
            pl.BlockSpec((1, tm, D_ATTN), lambda b, i: (b, i, 0)),
            pl.BlockSpec((1, tm, D_ATTN), lambda b, i: (b, i, 0)),
            pl.BlockSpec((1, D_ATTN, tm), lambda b, i: (b, 0, i)),
        ],
        out_shape=[
            jax.ShapeDtypeStruct((B, S, D_ATTN), BF16),
            jax.ShapeDtypeStruct((B, S, D_ATTN), BF16),
            jax.ShapeDtypeStruct((B, D_ATTN, S), BF16),
        ],
        scratch_shapes=[pltpu.VMEM((D, 3 * D_ATTN), BF16),
                        pltpu.VMEM((STAGE_SLOTS, WIDE_STAGE_ROWS, 3 * D_ATTN), F32),
                        pltpu.SemaphoreType.DMA((STAGE_SLOTS,))],
        compiler_params=_params(("arbitrary", "arbitrary"), vmem),
        name="qkv_proj",
    )(x, w_qkv)


def _attn_kernel(lambda_init, q_ref, k_ref, vt_ref, diag_ref, corner_ref, lam_ref, sg_ref, o_ref,
                 qs_ref, m_ref, acc_ref, s0_ref, s1_ref, ns_ref):
    i = pl.program_id(2)
    blk = ATT_BLOCK
    nq = q_ref.shape[1] // blk
    half = blk - MAX_DISTANCE
    diag_bias = diag_ref[0]
    prev_corner = corner_ref[0]

    lane = lax.broadcasted_iota(jnp.int32, (blk, V_DIM), 1)
    for hq in range(nq):
        q = q_ref[0, hq * blk:(hq + 1) * blk, :]
        zero = jnp.zeros_like(q)
        qs_ref[2 * hq] = jnp.where(lane < HEAD_DIM, q, zero)
        qs_ref[2 * hq + 1] = jnp.where(lane >= HEAD_DIM, q, zero)

    def scores(kc, cb, nblk=1):
        rows = nblk * blk
        start = pl.multiple_of(kc * rows, rows)
        k = k_ref[0, pl.ds(start, rows), :]
        return lax.dot_general(k, qs_ref[cb], NT_DIMS, preferred_element_type=F32)

    def values(kc, nblk=1):
        rows = nblk * blk
        start = pl.multiple_of(kc * rows, rows)
        vt = vt_ref[0, :, pl.ds(start, rows)]
        return jnp.concatenate([vt, jnp.ones((SUM_ROWS, rows), BF16)], axis=0)

    def update(cb, s, vt, bias, first):
        if bias == "diag":
            s = s + diag_bias
        elif bias == "prev":
            top = s.shape[0] - half
            bottom = jnp.concatenate([s[top:, :half] + prev_corner, s[top:, half:]], axis=1)
            s = jnp.concatenate([s[:top], bottom], axis=0)
        m_new = jnp.max(s, axis=0, keepdims=True)
        if not first:
            m_prev = m_ref[cb]
            m_new = jnp.maximum(m_prev, m_new)
        p = jnp.exp2(s - m_new)
        pv = jnp.dot(vt, p.astype(BF16), preferred_element_type=F32)
        if first:
            acc_ref[cb] = pv
        else:
            acc_ref[cb] = acc_ref[cb] * jnp.exp2(m_prev - m_new) + pv
        m_ref[cb] = m_new

    base = nq * i
    near = [(d, hq, c) for d in range(nq - 1, -1, -1) for hq in range(d, nq) for c in range(2)]

    s_refs = (s0_ref, s1_ref)

    def finalize(hq):
        lp = lam_ref[...]
        lam = (jnp.exp(jnp.sum(lp[0:1] * lp[1:2], axis=-1, keepdims=True))
               - jnp.exp(jnp.sum(lp[2:3] * lp[3:4], axis=-1, keepdims=True)) + lambda_init)
        a0 = acc_ref[2 * hq]
        a1 = acc_ref[2 * hq + 1]
        inv0 = 1.0 / a0[V_DIM:V_DIM + 1]
        inv1 = lam / a1[V_DIM:V_DIM + 1]
        ot = a0[:V_DIM] * inv0 - a1[:V_DIM] * inv1
        ot = ot * lax.rsqrt(jnp.mean(ot * ot, axis=0, keepdims=True) + LN_EPS)
        gain = sg_ref[...] * (1.0 - lambda_init)
        o_ref[0, :, hq * blk:(hq + 1) * blk] = (ot * gain).astype(BF16)

    def near_phase(extra_scores, final):
        def score_job(j, d, hq, c):
            def run():
                ns_ref[j] = scores(base + d, 2 * hq + c)
            return run

        jobs = [score_job(j, d, hq, c) for j, (d, hq, c) in enumerate(near)] + extra_scores
        ahead = max(NEAR_LOOKAHEAD, len(jobs) - len(near))
        for job in jobs[:ahead]:
            job()
        vts = {}
        for j, (d, hq, c) in enumerate(near):
            if j + ahead < len(jobs):
                jobs[j + ahead]()
            if d not in vts:
                vts[d] = values(base + d)
            bias = "diag" if hq == d else "prev" if hq == d + 1 else None
            update(2 * hq + c, ns_ref[j], vts[d], bias, hq == d)
            if final and d == 0 and c == 1:
                finalize(hq)

    def score_into(kc, slot, cb):
        def run():
            s_refs[slot][cb] = scores(kc, cb, FAR_BLOCKS)
        return run

    def issue_consume(kc, slot, issue_next=True, last=False):
        vt = values(kc, FAR_BLOCKS)
        for cb in range(2 * nq):
            if issue_next:
                score_into(kc + 1, 1 - slot, cb)()
            bias = "prev" if (last and cb < 2) else None
            update(cb, s_refs[slot][cb], vt, bias, False)
            if last and cb % 2 == 1:
                finalize(cb // 2)

    @pl.when(i == 0)
    def _():
        near_phase([], final=True)

    assert nq % (2 * FAR_BLOCKS) == 0, "the pipeline pairs chunks and needs an even count per tile"
    pairs = nq // (2 * FAR_BLOCKS) * i

    @pl.when(i > 0)
    def _():
        near_phase([score_into(0, 0, cb) for cb in range(2 * nq)], final=False)

        def pair(t, carry):
            issue_consume(2 * t, 0)
            issue_consume(2 * t + 1, 1)
            return carry

        lax.fori_loop(0, pairs - 1, pair, 0)
        issue_consume(2 * pairs - 2, 0)
        issue_consume(2 * pairs - 1, 1, issue_next=False, last=True)


def _rel_bucket(rel):
    n = jnp.maximum(rel, 0)
    nf = jnp.maximum(n, 1).astype(F32)
    large = MAX_EXACT + jnp.floor(jnp.log(nf / MAX_EXACT) / math.log(MAX_DISTANCE / MAX_EXACT)
                                  * (NUM_BUCKETS - MAX_EXACT)).astype(jnp.int32)
    return jnp.where(n < MAX_EXACT, n, jnp.minimum(large, NUM_BUCKETS - 1))


def _near_bias(rel_table):
    blk = ATT_BLOCK
    half = blk - MAX_DISTANCE
    table = rel_table.astype(F32)
    table = (table - table[NUM_BUCKETS - 1]) * LOG2_E

    def tile(n, offset):
        rel = jnp.arange(n)[None, :] - jnp.arange(n)[:, None] + offset
        bucket = _rel_bucket(rel)
        bias = jnp.zeros((N_HEADS, n, n), F32)
        for b in range(NUM_BUCKETS):
            bias = jnp.where(bucket[None] == b, table[b][:, None, None], bias)
        return jnp.where(rel[None] >= 0, bias, MASK_VALUE)

    return tile(blk, 0), tile(half, half)


def _attention(q, k, vt, diag_bias, corner_bias, lam_p, subln_g, lambda_init):
    B, S, _ = q.shape
    tq = ATT_TILE
    blk = ATT_BLOCK
    nq = tq // blk
    nchain = 2 * nq
    nnear = nq * (nq + 1)
    H = N_HEADS
    rows = V_DIM + SUM_ROWS
    vmem = (4 * S * V_DIM * 2 + 2 * 2 * blk * blk * 4 + ((2 * FAR_BLOCKS + 1) * nchain + nnear) * blk * blk * 4
            + 2 * nchain * rows * blk * 4 + 4 * tq * V_DIM * 2)
    return pl.pallas_call(
        functools.partial(_attn_kernel, lambda_init),
        grid=(B, H, S // tq),
        in_specs=[
            pl.BlockSpec((1, tq, V_DIM), lambda b, h, i: (b, i, h)),
            pl.BlockSpec((1, S, V_DIM), lambda b, h, i: (b, 0, h)),
            pl.BlockSpec((1, V_DIM, S), lambda b, h, i: (b, h, 0)),
            pl.BlockSpec((1, blk, blk), lambda b, h, i: (h, 0, 0)),
            pl.BlockSpec((1,) + corner_bias.shape[1:], lambda b, h, i: (h, 0, 0)),
            pl.BlockSpec((4, HEAD_DIM), lambda b, h, i: (0, 0)),
            pl.BlockSpec((V_DIM, blk), lambda b, h, i: (0, 0)),
        ],
        out_specs=pl.BlockSpec((1, V_DIM, tq), lambda b, h, i: (b, h, i)),
        out_shape=jax.ShapeDtypeStruct((B, D_ATTN, S), BF16),
        scratch_shapes=[
            pltpu.VMEM((nchain, blk, V_DIM), BF16),
            pltpu.VMEM((nchain, 1, blk), F32),
            pltpu.VMEM((nchain, rows, blk), F32),
            pltpu.VMEM((nchain, FAR_BLOCKS * blk, blk), F32),
            pltpu.VMEM((nchain, FAR_BLOCKS * blk, blk), F32),
            pltpu.VMEM((nnear, blk, blk), F32),
        ],
        compiler_params=_params(("parallel", "parallel", "arbitrary"), vmem),
        name="diff_attn",
    )(q, k, vt, diag_bias, corner_bias, lam_p, subln_g)


def _layer1_kernel(x_ref, a_ref, wo_ref, g0_ref, b0_ref, wg_ref, wu_ref, wd_ref, g1_ref, b1_ref,
                   o_ref, y_ref, xb_ref, acc_ref, h_ref):
    tm = x_ref.shape[1] // HALVES

    def mixer(hf, y_out):
        a_t = a_ref[0, :, hf * tm:(hf + 1) * tm]
        h = lax.dot_general(a_t, wo_ref[...], (((0,), (0,)), ((), ())), preferred_element_type=F32)
        y_out[...] = _layernorm(ALPHA * x_ref[0, hf * tm:(hf + 1) * tm, :] + h, g0_ref[...], b0_ref[...])

    _two_halves(mixer, o_ref, (wg_ref, wu_ref, wd_ref, g1_ref, b1_ref), (y_ref, xb_ref, acc_ref, h_ref))


def _layer1(x, at, wo, g0, b0, wg, wu, wd, g1, b1):
    B, S, D = x.shape
    tm = HALVES * ROW_TILE
    vmem = _ffn_vmem(ROW_TILE) + D * D_ATTN * 2 + 2 * tm * D_ATTN * 2 + 6 * ROW_TILE * D * 4
    return pl.pallas_call(
        _layer1_kernel,
        grid=(B, S // tm),
        in_specs=[
            pl.BlockSpec((1, tm, D), lambda b, i: (b, i, 0)),
            pl.BlockSpec((1, D_ATTN, tm), lambda b, i: (b, 0, i)),
            _resident((D_ATTN, D)), _resident((1, D)), _resident((1, D)),
            _resident((D, D_FF)), _resident((D, D_FF)), _resident((D_FF, D)), _resident((1, D)), _resident((1, D)),
        ],
        out_specs=pl.BlockSpec((1, tm, D), lambda b, i: (b, i, 0)),
        out_shape=jax.ShapeDtypeStruct((B, S, D), F32),
        scratch_shapes=_ffn_scratch(ROW_TILE)[:4],
        compiler_params=_params(("parallel", "parallel"), vmem),
        name="layer1_oproj_ffn",
    )(x, at, wo, g0, b0, wg, wu, wd, g1, b1)


def kernel(x, pool_w, pool_scale, w_qkv, w_o, lam_p, subln_g, rel_table, w_gate, w_up, w_down,
           ln_mix_g, ln_mix_b, ln_ffn_g, ln_ffn_b):
    row = lambda a: a.reshape(1, -1).astype(F32)
    wg, wu, wd = w_gate.astype(F32), w_up.astype(F32), w_down.astype(F32)

    x2, wg1, wu1, wd1 = _layer0(0, x, pool_w[0].astype(BF16), row(pool_scale[0]), row(ln_mix_g[0]),
                                row(ln_mix_b[0]), wg, wu, wd, row(ln_ffn_g[0]), row(ln_ffn_b[0]))

    lambda_init = 0.8 - 0.6 * math.exp(-0.3 * 1)
    q, k, vt = _qkv(0, x2, w_qkv.astype(F32))
    gain = jnp.broadcast_to(subln_g[0].astype(F32)[:, None], (V_DIM, ATT_BLOCK))
    at = _attention(q, k, vt, *_near_bias(rel_table), lam_p[0].astype(F32), gain, lambda_init)
    return _layer1(x2, at, w_o[0].astype(BF16), row(ln_mix_g[1]), row(ln_mix_b[1]), wg1, wu1, wd1,
                   row(ln_ffn_g[1]), row(ln_ffn_b[1]))
```
